```python
import functools
import jax, jax.numpy as jnp
from jax import lax
import numpy as np

D_MODEL = 2048
BATCH = 4
SEQ = 2048
DEPTH = 1
DEC_BATCH = 8
DEC_SEQ = 1
PAST_LEN = 16384
PAGE_SIZE = 128

RET_HEADS = 8
RET_QK_DIM = 128
RET_V_DIM = 256
RET_CHUNK = 128
ROPE_BASE = 10000.0
ATT_HEADS = 16
ATT_KV_HEADS = 4
HEAD_DIM = 128
GROUP = ATT_HEADS // ATT_KV_HEADS
IDX_HEADS = 16
IDX_DIM = 64
IDX_W_SCALE = (IDX_HEADS ** -0.5) * (IDX_DIM ** -0.5)
TOPK_MAX = 256
Q_BLOCK = 128
D_FF = 5632
EPS = 1e-6

RET_QK_W = RET_HEADS * RET_QK_DIM
RET_V_W = RET_HEADS * RET_V_DIM
ATT_Q_W = ATT_HEADS * HEAD_DIM
ATT_KV_W = ATT_KV_HEADS * HEAD_DIM
IDX_Q_W = IDX_HEADS * IDX_DIM
SPLITS = (RET_QK_W, RET_QK_W, RET_V_W, RET_V_W,
          ATT_Q_W, ATT_KV_W, ATT_KV_W,
          IDX_Q_W, IDX_DIM, IDX_HEADS,
          D_MODEL, D_MODEL)
IN_W = sum(SPLITS)

kernel_name = "retention_dsa_gated_macaron_step"


def rms_norm(x, g):
    xf = x.astype(jnp.float32)
    y = xf * lax.rsqrt(jnp.mean(xf * xf, axis=-1, keepdims=True) + EPS)
    return (y * g.astype(jnp.float32)).astype(x.dtype)


def swiglu(x, w1, w2):
    gate, up = jnp.split(x @ w1, 2, axis=-1)
    return (jax.nn.silu(gate) * up) @ w2


def split_cols(a):
    points, acc = [], 0
    for w in SPLITS[:-1]:
        acc += w
        points.append(acc)
    return jnp.split(a, points, axis=-1)


def rotary(x, pos):
    half = x.shape[-1] // 2
    inv = ROPE_BASE ** (-jnp.arange(half, dtype=jnp.float32) / half)
    ang = pos.astype(jnp.float32)[:, None] * inv[None, :]
    cos = jnp.cos(ang)[None, :, None, :]
    sin = jnp.sin(ang)[None, :, None, :]
    xf = x.astype(jnp.float32)
    x1, x2 = xf[..., :half], xf[..., half:]
    return jnp.concatenate([x1 * cos - x2 * sin, x1 * sin + x2 * cos], axis=-1).astype(x.dtype)


def retention_chunk(state, q, k, v, log_g):
    c = q.shape[1]
    i = jnp.arange(c, dtype=jnp.float32)
    rel = i[:, None] - i[None, :]
    causal = rel >= 0
    dmat = jnp.where(causal[None], jnp.exp(log_g[:, None, None] * jnp.where(causal, rel, 0.0)[None]), 0.0)
    scores = jnp.einsum('bihd,bjhd->bhij', q, k) * dmat[None]
    inner = jnp.einsum('bhij,bjhe->bihe', scores, v)
    q_dec = jnp.exp(log_g[None, :] * (i[:, None] + 1.0))
    cross = jnp.einsum('bihd,bhde->bihe', q * q_dec[None, :, :, None], state)
    k_dec = jnp.exp(log_g[None, :] * (c - 1.0 - i)[:, None])
    new_state = (jnp.exp(log_g * c)[None, :, None, None] * state
                 + jnp.einsum('bjhd,bjhe->bhde', k * k_dec[None, :, :, None], v))
    return new_state, inner + cross


def retention(q, k, v, g, state0, pos, chunk, gn, w_out):
    b, t = q.shape[:2]
    dt = q.dtype
    q = rotary(q.reshape(b, t, RET_HEADS, RET_QK_DIM), pos).astype(jnp.float32)
    k = rotary(k.reshape(b, t, RET_HEADS, RET_QK_DIM), pos).astype(jnp.float32) * (RET_QK_DIM ** -0.5)
    v = v.reshape(b, t, RET_HEADS, RET_V_DIM).astype(jnp.float32)
    log_g = jnp.log1p(-jnp.exp2(-5.0 - jnp.arange(RET_HEADS, dtype=jnp.float32)))
    nc = t // chunk
    blocks = lambda a: a.reshape(b, nc, chunk, *a.shape[2:]).swapaxes(0, 1)
    step = lambda s, xs: retention_chunk(s, xs[0], xs[1], xs[2], log_g)
    final, o = lax.scan(step, state0.astype(jnp.float32), (blocks(q), blocks(k), blocks(v)))
    o = o.swapaxes(0, 1).reshape(b, t, RET_HEADS, RET_V_DIM)
    o = rms_norm(o, gn).reshape(b, t, RET_V_W).astype(dt) * jax.nn.silu(g)
    return o @ w_out, final.astype(state0.dtype)


def indexer_select(qi, wi, ki, q_pos, n_keep):
    s = jax.nn.relu(jnp.einsum('bthd,bsd->bths', qi, ki).astype(jnp.float32))
    s = jnp.einsum('bths,bth->bts', s, wi.astype(jnp.float32))
    key_pos = jnp.arange(ki.shape[1])
    visible = key_pos[None, :] <= q_pos[:, None]
    s = jnp.where(visible[None], s, -jnp.inf)
    _, idx = lax.top_k(s, n_keep)
    valid = idx <= q_pos[None, :, None]
    return idx, valid


def attend_selected(q, k_sel, v_sel, valid):
    b, t = q.shape[:2]
    s = jnp.einsum('btkgd,btskd->btkgs', q, k_sel).astype(jnp.float32)
    s = jnp.where(valid[:, :, None, None, :], s, -jnp.inf)
    p = jax.nn.softmax(s, axis=-1).astype(v_sel.dtype)
    o = jnp.einsum('btkgs,btskd->btkgd', p, v_sel)
    return o.reshape(b, t, ATT_Q_W)


_take = jax.vmap(lambda a, i: a[i])


def prompt_sparse_attention(q, k, v, qi, wi, ki):
    b, t = q.shape[:2]
    nb = t // Q_BLOCK
    n_keep = min(TOPK_MAX, t // 4)
    to_blocks = lambda a: a.reshape(b, nb, Q_BLOCK, *a.shape[2:]).swapaxes(0, 1)

    def one_block(xs):
        qb, qib, wib, t0 = xs
        q_pos = t0 + jnp.arange(Q_BLOCK)
        idx, valid = indexer_select(qib, wib, ki, q_pos, n_keep)
        return attend_selected(qb, _take(k, idx), _take(v, idx), valid)

    out = lax.map(one_block, (to_blocks(q), to_blocks(qi), to_blocks(wi), jnp.arange(nb) * Q_BLOCK))
    return out.swapaxes(0, 1).reshape(b, t, ATT_Q_W)


def decode_sparse_attention(q, k, v, qi, wi, ki, pool_k, pool_v, pool_ki, page_table, layer):
    db, t = q.shape[:2]
    past = page_table.shape[1] * PAGE_SIZE
    past_ki = pool_ki[layer, page_table].reshape(db, past, IDX_DIM).astype(ki.dtype)
    ki_all = jnp.concatenate([past_ki, ki], axis=1)
    n_keep = min(TOPK_MAX, (past + t) // 4)
    q_pos = past + jnp.arange(t)
    idx, valid = indexer_select(qi, wi, ki_all, q_pos, n_keep)
    in_past = (idx < past)[..., None, None]
    pidx = jnp.minimum(idx, past - 1)
    phys = jax.vmap(lambda pt, i: pt[i])(page_table, pidx // PAGE_SIZE)
    off = pidx % PAGE_SIZE
    nidx = jnp.clip(idx - past, 0, t - 1)
    select = lambda pool, new: jnp.where(in_past, pool[layer, phys, off].astype(new.dtype), _take(new, nidx))
    return attend_selected(q, select(pool_k, k), select(pool_v, v), valid)


def layer_forward(x, pos, ret_state0, ret_chunk, sparse_attn, p):
    (ffn1_norm, ffn1_w1, ffn1_w2, mix_norm, w_in, q_norm, k_norm, ret_norm,
     w_ret_out, w_att_out, w_o, ffn2_norm, ffn2_w1, ffn2_w2) = p
    b, t, _ = x.shape
    x = x + 0.5 * swiglu(rms_norm(x, ffn1_norm), ffn1_w1, ffn1_w2)
    h = rms_norm(x, mix_norm)
    rq, rk, rv, rg, aq, ak, av, iq, ik, iw, g_ret, g_att = split_cols(h @ w_in)
    ret_out, ret_state = retention(rq, rk, rv, rg, ret_state0, pos, ret_chunk, ret_norm, w_ret_out)
    aq = rms_norm(aq.reshape(b, t, ATT_HEADS, HEAD_DIM), q_norm) * (HEAD_DIM ** -0.5)
    ak = rms_norm(ak.reshape(b, t, ATT_KV_HEADS, HEAD_DIM), k_norm)
    av = av.reshape(b, t, ATT_KV_HEADS, HEAD_DIM)
    iq = iq.reshape(b, t, IDX_HEADS, IDX_DIM)
    iw = iw * IDX_W_SCALE
    att = sparse_attn(aq.reshape(b, t, ATT_KV_HEADS, GROUP, HEAD_DIM), ak, av, iq, iw, ik)
    att_out = att @ w_att_out
    merged = jax.nn.sigmoid(g_ret) * ret_out + jax.nn.sigmoid(g_att) * att_out
    x = x + merged @ w_o
    x = x + 0.5 * swiglu(rms_norm(x, ffn2_norm), ffn2_w1, ffn2_w2)
    return x, ret_state, ak, av, ik


def setup_inputs(seed: int = 0) -> dict:
    key = jax.random.key(seed)
    ks = jax.random.split(key, 32)
    n_pages = PAST_LEN // PAGE_SIZE
    used = DEC_BATCH * n_pages
    n_phys = used + max(1, used // 4)
    nrm = lambda k, shape, scale: jax.random.normal(k, shape, jnp.float32) * scale
    gain = lambda k, n: 1.0 + 0.02 * jax.random.normal(k, (DEPTH, n), jnp.float32)
    page_table = jax.random.permutation(ks[0], n_phys)[:used].reshape(DEC_BATCH, n_pages).astype(jnp.int32)
    return {
        "x_prompt": nrm(ks[1], (BATCH, SEQ, D_MODEL), 1.0),
        "x_sample": nrm(ks[2], (DEC_BATCH, DEC_SEQ, D_MODEL), 1.0),
        "state_ret": nrm(ks[3], (DEPTH, DEC_BATCH, RET_HEADS, RET_QK_DIM, RET_V_DIM), 0.1),
        "cache_k": nrm(ks[4], (DEPTH, n_phys, PAGE_SIZE, ATT_KV_HEADS, HEAD_DIM), 1.0),
        "cache_v": nrm(ks[5], (DEPTH, n_phys, PAGE_SIZE, ATT_KV_HEADS, HEAD_DIM), 1.0),
        "cache_idx_k": nrm(ks[6], (DEPTH, n_phys, PAGE_SIZE, IDX_DIM), 1.0),
        "page_table": page_table,
        "ffn1_norm": gain(ks[7], D_MODEL),
        "ffn1_w1": nrm(ks[8], (DEPTH, D_MODEL, 2 * D_FF), D_MODEL ** -0.5),
        "ffn1_w2": nrm(ks[9], (DEPTH, D_FF, D_MODEL), D_FF ** -0.5),
        "mix_norm": gain(ks[10], D_MODEL),
        "w_in": nrm(ks[11], (DEPTH, D_MODEL, IN_W), D_MODEL ** -0.5),
        "q_norm": gain(ks[12], HEAD_DIM),
        "k_norm": gain(ks[13], HEAD_DIM),
        "ret_norm": gain(ks[14], RET_V_DIM),
        "w_ret_out": nrm(ks[15], (DEPTH, RET_V_W, D_MODEL), RET_V_W ** -0.5),
        "w_att_out": nrm(ks[16], (DEPTH, ATT_Q_W, D_MODEL), ATT_Q_W ** -0.5),
        "w_o": nrm(ks[17], (DEPTH, D_MODEL, D_MODEL), D_MODEL ** -0.5),
        "ffn2_norm": gain(ks[18], D_MODEL),
        "ffn2_w1": nrm(ks[19], (DEPTH, D_MODEL, 2 * D_FF), D_MODEL ** -0.5),
        "ffn2_w2": nrm(ks[20], (DEPTH, D_FF, D_MODEL), D_FF ** -0.5),
    }


def reference(x_prompt, x_sample, state_ret, cache_k, cache_v, cache_idx_k, page_table,
              ffn1_norm, ffn1_w1, ffn1_w2, mix_norm, w_in, q_norm, k_norm, ret_norm,
              w_ret_out, w_att_out, w_o, ffn2_norm, ffn2_w1, ffn2_w2):
    b, tp = x_prompt.shape[:2]
    ts = x_sample.shape[1]
    past = page_table.shape[1] * PAGE_SIZE
    pos_p = jnp.arange(tp)
    pos_s = past + jnp.arange(ts)
    weights = (ffn1_norm, ffn1_w1, ffn1_w2, mix_norm, w_in, q_norm, k_norm, ret_norm,
               w_ret_out, w_att_out, w_o, ffn2_norm, ffn2_w1, ffn2_w2)
    yp, ys = x_prompt, x_sample
    sp_l, kp_l, vp_l, ip_l, ss_l, ks_l, vs_l, is_l = [], [], [], [], [], [], [], []
    for l in range(DEPTH):
        p = tuple(w[l] for w in weights)
        zero_state = jnp.zeros((b, RET_HEADS, RET_QK_DIM, RET_V_DIM), x_prompt.dtype)
        yp, sp, kp, vp, ip = layer_forward(yp, pos_p, zero_state, min(RET_CHUNK, tp),
                                           prompt_sparse_attention, p)
        dec_attn = functools.partial(decode_sparse_attention, pool_k=cache_k, pool_v=cache_v,
                                     pool_ki=cache_idx_k, page_table=page_table, layer=l)
        ys, ss, ks_, vs, is_ = layer_forward(ys, pos_s, state_ret[l], ts, dec_attn, p)
        sp_l.append(sp); kp_l.append(kp); vp_l.append(vp); ip_l.append(ip)
        ss_l.append(ss); ks_l.append(ks_); vs_l.append(vs); is_l.append(is_)
    ret_state_prompt = jnp.stack(sp_l)
    k_prompt = jnp.stack(kp_l)
    v_prompt = jnp.stack(vp_l)
    idx_k_prompt = jnp.stack(ip_l)
    ret_state_sample = jnp.stack(ss_l)
    k_sample = jnp.stack(ks_l)
    v_sample = jnp.stack(vs_l)
    idx_k_sample = jnp.stack(is_l)
    return (yp, ys, ret_state_prompt, k_prompt, v_prompt, idx_k_prompt,
            ret_state_sample, k_sample, v_sample, idx_k_sample)
```

```python
import functools

import jax
import jax.numpy as jnp
from jax import lax
from jax.experimental import pallas as pl
from jax.experimental.pallas import tpu as pltpu

RET_HEADS = 8
RET_QK_DIM = 128
RET_V_DIM = 256
ATT_HEADS = 16
ATT_KV_HEADS = 4
HEAD_DIM = 128
GROUP = ATT_HEADS // ATT_KV_HEADS
IDX_HEADS = 16
IDX_DIM = 64
IDX_W_SCALE = (IDX_HEADS ** -0.5) * (IDX_DIM ** -0.5)
TOPK_MAX = 256
PAGE_SIZE = 128
ROPE_BASE = 10000.0
EPS = 1e-6

RET_QK_W = RET_HEADS * RET_QK_DIM
RET_V_W = RET_HEADS * RET_V_DIM
ATT_Q_W = ATT_HEADS * HEAD_DIM
ATT_KV_W = ATT_KV_HEADS * HEAD_DIM
IDX_Q_W = IDX_HEADS * IDX_DIM
MAIN_W = 2 * RET_QK_W + 2 * RET_V_W + ATT_Q_W + 2 * ATT_KV_W + IDX_Q_W

LANES = 128
TN = 512
T_RQ, T_RK, T_RV, T_RG, T_AQ, T_AK, T_AV, T_IQ, T_IKW, T_END = 0, 2, 4, 8, 12, 16, 17, 18, 20, 21
VMEM_LIMIT = 56 * 1024 * 1024
INT_MIN = -2 ** 31
NEG_BIG = -1e30

bf16 = jnp.bfloat16
f32 = jnp.float32


def _params(sem, vmem=VMEM_LIMIT):
    return pltpu.CompilerParams(dimension_semantics=sem, vmem_limit_bytes=vmem)


def _sigmoid(x):
    return 1.0 / (1.0 + jnp.exp(-x))


def _dot(a, b):
    return jnp.dot(a, b, preferred_element_type=f32)


def _dot_nt(a, b):
    return lax.dot_general(a, b, (((1,), (1,)), ((), ())), preferred_element_type=f32)


def _rms(x, gain):
    ms = jnp.mean(x * x, axis=-1, keepdims=True)
    return x * lax.rsqrt(ms + EPS) * gain


def _ffn_kernel(x_ref, g_ref, w1g_ref, w1u_ref, w2_ref, o_ref, h_ref, acc_ref):
    j = pl.program_id(1)

    @pl.when(j == 0)
    def _():
        h_ref[...] = _rms(x_ref[...], g_ref[...]).astype(bf16)
        acc_ref[...] = jnp.zeros_like(acc_ref)

    h = h_ref[...]
    gate = _dot(h, w1g_ref[...])
    up = _dot(h, w1u_ref[...])
    act = (gate * _sigmoid(gate) * up).astype(bf16)
    acc_ref[...] += _dot(act, w2_ref[...])

    @pl.when(j == pl.num_programs(1) - 1)
    def _():
        o_ref[...] = x_ref[...] + 0.5 * acc_ref[...]


def _ffn(x, gain, w1, w2, tm, tf):
    m, d = x.shape
    dff = w2.shape[0]
    nf = dff // tf
    return pl.pallas_call(
        _ffn_kernel,
        grid=(m // tm, nf),
        in_specs=[
            pl.BlockSpec((tm, d), lambda i, j: (i, 0)),
            pl.BlockSpec((1, d), lambda i, j: (0, 0)),
            pl.BlockSpec((d, tf), lambda i, j: (0, j)),
            pl.BlockSpec((d, tf), lambda i, j: (0, j + nf)),
            pl.BlockSpec((tf, d), lambda i, j: (j, 0)),
        ],
        out_specs=pl.BlockSpec((tm, d), lambda i, j: (i, 0)),
        out_shape=jax.ShapeDtypeStruct((m, d), f32),
        scratch_shapes=[pltpu.VMEM((tm, d), bf16), pltpu.VMEM((tm, d), f32)],
        compiler_params=_params(("arbitrary", "arbitrary")),
    )(x, gain, w1, w1, w2)


def _proj_kernel(x_ref, g_ref, w_ref, cos_ref, sin_ref, qn_ref, kn_ref,
                 p_ref, k32_ref, v32_ref, ikw_ref, h_ref, *, base):
    j = pl.program_id(1)

    @pl.when(j == 0)
    def _():
        h_ref[...] = _rms(x_ref[...], g_ref[...]).astype(bf16)

    res = _dot(h_ref[...], w_ref[...])
    heads = [slice(c * LANES, (c + 1) * LANES) for c in range(TN // LANES)]

    @pl.when(j < base)
    def _():
        p_ref[...] = _sigmoid(res).astype(bf16)

    @pl.when((j >= base + T_RQ) & (j < base + T_RV))
    def _():
        cos, sin = cos_ref[...], sin_ref[...]
        scale = jnp.where(j >= base + T_RK, RET_QK_DIM ** -0.5, 1.0).astype(f32)
        for sl in heads:
            x = res[:, sl]
            p_ref[:, sl] = ((x * cos + pltpu.roll(x, LANES // 2, 1) * sin) * scale).astype(bf16)

    @pl.when(((j >= base + T_RV) & (j < base + T_RG)) | ((j >= base + T_IQ) & (j < base + T_IKW)))
    def _():
        p_ref[...] = res.astype(bf16)

    @pl.when((j >= base + T_RG) & (j < base + T_AQ))
    def _():
        p_ref[...] = (res * _sigmoid(res)).astype(bf16)

    @pl.when((j >= base + T_AQ) & (j < base + T_AK))
    def _():
        for sl in heads:
            p_ref[:, sl] = (_rms(res[:, sl], qn_ref[...]) * HEAD_DIM ** -0.5).astype(bf16)

    @pl.when(j == base + T_AK)
    def _():
        for sl in heads:
            y = _rms(res[:, sl], kn_ref[...])
            k32_ref[:, sl] = y
            p_ref[:, sl] = y.astype(bf16)

    @pl.when(j == base + T_AV)
    def _():
        v32_ref[...] = res
        p_ref[...] = res.astype(bf16)

    @pl.when(j == base + T_IKW)
    def _():
        ikw_ref[...] = res[:, :LANES]
        p_ref[...] = res.astype(bf16)


def _proj(x, gain, w_pad, cos2, sin2, qn, kn, tm):
    m, d = x.shape
    nt = w_pad.shape[1] // TN
    base = nt - T_END
    tab_blocks = cos2.shape[0] // tm
    kern = functools.partial(_proj_kernel, base=base)
    return pl.pallas_call(
        kern,
        grid=(m // tm, nt),
        in_specs=[
            pl.BlockSpec((tm, d), lambda i, j: (i, 0)),
            pl.BlockSpec((1, d), lambda i, j: (0, 0)),
            pl.BlockSpec((d, TN), lambda i, j: (0, j)),
            pl.BlockSpec((tm, LANES), lambda i, j: (i % tab_blocks, 0)),
            pl.BlockSpec((tm, LANES), lambda i, j: (i % tab_blocks, 0)),
            pl.BlockSpec((1, LANES), lambda i, j: (0, 0)),
            pl.BlockSpec((1, LANES), lambda i, j: (0, 0)),
        ],
        out_specs=[
            pl.BlockSpec((tm, TN), lambda i, j: (i, j)),
            pl.BlockSpec((tm, TN), lambda i, j: (i, 0)),
            pl.BlockSpec((tm, TN), lambda i, j: (i, 0)),
            pl.BlockSpec((tm, LANES), lambda i, j: (i, 0)),
        ],
        out_shape=[
            jax.ShapeDtypeStruct((m, nt * TN), bf16),
            jax.ShapeDtypeStruct((m, ATT_KV_W), f32),
            jax.ShapeDtypeStruct((m, ATT_KV_W), f32),
            jax.ShapeDtypeStruct((m, LANES), f32),
        ],
        scratch_shapes=[pltpu.VMEM((tm, d), bf16)],
        compiler_params=_params(("arbitrary", "arbitrary")),
    )(x, gain, w_pad, cos2, sin2, qn, kn)


def _ret_kernel(lg_ref, q_ref, k_ref, v_ref, sg_ref, gn_ref, og_ref, st_ref, state_ref, *, chunk):
    h = pl.program_id(1)
    c = pl.program_id(2)
    lg = lg_ref[h]

    @pl.when(c == 0)
    def _():
        state_ref[...] = jnp.zeros_like(state_ref)

    q, k, v = q_ref[...], k_ref[...], v_ref[...]
    ii = lax.broadcasted_iota(jnp.int32, (chunk, chunk), 0)
    jj = lax.broadcasted_iota(jnp.int32, (chunk, chunk), 1)
    rel = (ii - jj).astype(f32)
    dmat = jnp.where(rel >= 0, jnp.exp(lg * jnp.maximum(rel, 0.0)), 0.0)
    scores = _dot_nt(q, k) * dmat
    inner = _dot(scores.astype(bf16), v)
    i1 = lax.broadcasted_iota(jnp.int32, (chunk, 1), 0).astype(f32)
    q_dec = jnp.exp(lg * (i1 + 1.0))
    k_dec = jnp.exp(lg * (chunk - 1.0 - i1))
    state = state_ref[...]
    cross = _dot((q.astype(f32) * q_dec).astype(bf16), state.astype(bf16))
    kd = (k.astype(f32) * k_dec).astype(bf16)
    new_state = jnp.exp(lg * chunk) * state + lax.dot_general(
        kd, v, (((0,), (0,)), ((), ())), preferred_element_type=f32)
    state_ref[...] = new_state
    o = _rms(inner + cross, gn_ref[...])
    og_ref[...] = (o * sg_ref[...].astype(f32)).astype(bf16)

    @pl.when(c == pl.num_programs(2) - 1)
    def _():
        st_ref[0, 0] = new_state


def _retention(p, log_g, gn, batch, seq, base, chunk):
    nc = seq // chunk
    qb = (base + T_RQ) * (TN // RET_QK_DIM)
    kb = (base + T_RK) * (TN // RET_QK_DIM)
    vb = (base + T_RV) * (TN // RET_V_DIM)
    gb = (base + T_RG) * (TN // RET_V_DIM)
    kern = functools.partial(_ret_kernel, chunk=chunk)
    grid_spec = pltpu.PrefetchScalarGridSpec(
        num_scalar_prefetch=1,
        grid=(batch, RET_HEADS, nc),
        in_specs=[
            pl.BlockSpec((chunk, RET_QK_DIM), lambda b, h, c, lg: (b * nc + c, qb + h)),
            pl.BlockSpec((chunk, RET_QK_DIM), lambda b, h, c, lg: (b * nc + c, kb + h)),
            pl.BlockSpec((chunk, RET_V_DIM), lambda b, h, c, lg: (b * nc + c, vb + h)),
            pl.BlockSpec((chunk, RET_V_DIM), lambda b, h, c, lg: (b * nc + c, gb + h)),
            pl.BlockSpec((1, RET_V_DIM), lambda b, h, c, lg: (0, 0)),
        ],
        out_specs=[
            pl.BlockSpec((chunk, RET_V_DIM), lambda b, h, c, lg: (b * nc + c, h)),
            pl.BlockSpec((1, 1, RET_QK_DIM, RET_V_DIM), lambda b, h, c, lg: (b, h, 0, 0)),
        ],
        scratch_shapes=[pltpu.VMEM((RET_QK_DIM, RET_V_DIM), f32)],
    )
    return pl.pallas_call(
        kern,
        grid_spec=grid_spec,
        out_shape=[
            jax.ShapeDtypeStruct((batch * seq, RET_V_W), bf16),
            jax.ShapeDtypeStruct((batch, RET_HEADS, RET_QK_DIM, RET_V_DIM), f32),
        ],
        compiler_params=_params(("arbitrary", "arbitrary", "arbitrary")),
    )(log_g, p, p, p, p, gn)


def _sortable_key(score):
    bits = pltpu.bitcast(score, jnp.int32)
    return jnp.where(bits >= 0, bits, bits ^ jnp.int32(0x7FFFFFFF))


def _kth_largest_key(count_ge, n_keep, shape):
    def body(it, t_u):
        bit = jnp.left_shift(jnp.int32(1), 31 - it)
        cand_u = t_u | bit
        cnt = count_ge(cand_u ^ jnp.int32(INT_MIN))
        return jnp.where(cnt >= n_keep, cand_u, t_u)
    t_u = lax.fori_loop(0, 32, body, jnp.zeros(shape, jnp.int32))
    return t_u ^ jnp.int32(INT_MIN)


def _attn_kernel(aq_ref, k_ref, v_ref, iq_ref, ikw_k_ref, ikw_q_ref, o_ref, key_ref, *, tq, seq, n_keep):
    qb = pl.program_id(1)
    ikw = ikw_k_ref[...]
    lane = lax.broadcasted_iota(jnp.int32, (seq, LANES), 1)
    ik_lo = jnp.where(lane < IDX_DIM, ikw, 0.0).astype(bf16)
    ik_hi = jnp.where(lane >= IDX_DIM, pltpu.roll(ikw, IDX_DIM, 1), 0.0).astype(bf16)
    iwq = ikw_q_ref[...] * IDX_W_SCALE

    scores = jnp.zeros((tq, seq), f32)
    for hp in range(IDX_HEADS // 2):
        qp = iq_ref[:, hp * LANES:(hp + 1) * LANES]
        for sub, ikx in enumerate((ik_lo, ik_hi)):
            col = IDX_DIM + 2 * hp + sub
            s = jnp.maximum(_dot_nt(qp, ikx), 0.0)
            scores = scores + s * iwq[:, col:col + 1]

    q_pos = qb * tq + lax.broadcasted_iota(jnp.int32, (tq, seq), 0)
    k_pos = lax.broadcasted_iota(jnp.int32, (tq, seq), 1)
    visible = k_pos <= q_pos
    key_ref[...] = _sortable_key(jnp.where(visible, scores, -jnp.inf))

    def count_ge(t):
        return jnp.sum(jnp.where(key_ref[...] >= t, 1.0, 0.0), axis=1, keepdims=True)

    thr = _kth_largest_key(count_ge, n_keep, (tq, 1))
    bias = jnp.where((key_ref[...] >= thr) & visible, 0.0, -jnp.inf)

    for g in range(ATT_KV_HEADS):
        kg = k_ref[:, g * HEAD_DIM:(g + 1) * HEAD_DIM]
        vg = v_ref[:, g * HEAD_DIM:(g + 1) * HEAD_DIM]
        for i in range(GROUP):
            hd = g * GROUP + i
            q = aq_ref[:, hd * HEAD_DIM:(hd + 1) * HEAD_DIM]
            s = _dot_nt(q, kg) + bias
            m = jnp.max(s, axis=1, keepdims=True)
            p = jnp.exp(s - m)
            l = jnp.sum(p, axis=1, keepdims=True)
            o = _dot(p.astype(bf16), vg) / l
            o_ref[:, hd * HEAD_DIM:(hd + 1) * HEAD_DIM] = o.astype(bf16)


def _prompt_attention(p, ikw, batch, seq, base, tq, n_keep):
    nq = seq // tq
    kern = functools.partial(_attn_kernel, tq=tq, seq=seq, n_keep=n_keep)
    aq_blk = (base + T_AQ) * TN // ATT_Q_W
    iq_blk = (base + T_IQ) * TN // IDX_Q_W
    return pl.pallas_call(
        kern,
        grid=(batch, nq),
        in_specs=[
            pl.BlockSpec((tq, ATT_Q_W), lambda b, t: (b * nq + t, aq_blk)),
            pl.BlockSpec((seq, ATT_KV_W), lambda b, t: (b, base + T_AK)),
            pl.BlockSpec((seq, ATT_KV_W), lambda b, t: (b, base + T_AV)),
            pl.BlockSpec((tq, IDX_Q_W), lambda b, t: (b * nq + t, iq_blk)),
            pl.BlockSpec((seq, LANES), lambda b, t: (b, 0)),
            pl.BlockSpec((tq, LANES), lambda b, t: (b * nq + t, 0)),
        ],
        out_specs=pl.BlockSpec((tq, ATT_Q_W), lambda b, t: (b * nq + t, 0)),
        out_shape=jax.ShapeDtypeStruct((batch * seq, ATT_Q_W), bf16),
        scratch_shapes=[pltpu.VMEM((tq, seq), jnp.int32)],
        compiler_params=_params(("arbitrary", "arbitrary")),
    )(p, p, p, p, ikw, ikw)


def _oproj_kernel(og_ref, att_ref, gr_ref, ga_ref, x_ref, wr_ref, wa_ref, wo_ref, o_ref):
    ret_out = _dot(og_ref[...], wr_ref[...])
    att_out = _dot(att_ref[...], wa_ref[...])
    merged = gr_ref[...].astype(f32) * ret_out + ga_ref[...].astype(f32) * att_out
    o_ref[...] = x_ref[...] + _dot(merged.astype(bf16), wo_ref[...])


def _oproj(og, att, p, x, wr, wa, wo, tm):
    m, d = x.shape
    const = lambda shape: pl.BlockSpec(shape, lambda i: (0, 0), pipeline_mode=pl.Buffered(1))
    return pl.pallas_call(
        _oproj_kernel,
        grid=(m // tm,),
        in_specs=[
            pl.BlockSpec((tm, RET_V_W), lambda i: (i, 0)),
            pl.BlockSpec((tm, ATT_Q_W), lambda i: (i, 0)),
            pl.BlockSpec((tm, d), lambda i: (i, 0)),
            pl.BlockSpec((tm, d), lambda i: (i, 1)),
            pl.BlockSpec((tm, d), lambda i: (i, 0)),
            const(wr.shape), const(wa.shape), const(wo.shape),
        ],
        out_specs=pl.BlockSpec((tm, d), lambda i: (i, 0)),
        out_shape=jax.ShapeDtypeStruct((m, d), f32),
        compiler_params=_params(("arbitrary",)),
    )(og, att, p, p, x, wr, wa, wo)


def _dec_ret_kernel(lg_ref, q_ref, k_ref, v_ref, sg_ref, gn_ref, st_ref, og_ref, nst_ref):
    for h in range(RET_HEADS):
        lg = lg_ref[h]
        gamma = jnp.exp(lg)
        q = q_ref[0, :, h * RET_QK_DIM:(h + 1) * RET_QK_DIM]
        k = k_ref[0, :, h * RET_QK_DIM:(h + 1) * RET_QK_DIM]
        v = v_ref[0, :, h * RET_V_DIM:(h + 1) * RET_V_DIM]
        sg = sg_ref[0, :, h * RET_V_DIM:(h + 1) * RET_V_DIM]
        state = st_ref[0, 0, h]
        qk = jnp.sum(q * k, axis=1, keepdims=True)
        q8 = jnp.broadcast_to((q * gamma).astype(bf16), (8, RET_QK_DIM))
        cross = _dot(q8, state.astype(bf16))[0:1]
        k_col = jnp.transpose(jnp.broadcast_to(k, (8, RET_QK_DIM)))[:, 0:1]
        nst_ref[0, h] = gamma * state + k_col * v
        o = _rms(qk * v + cross, gn_ref[...])
        og_ref[0, :, h * RET_V_DIM:(h + 1) * RET_V_DIM] = o * sg


def _decode_retention(q, k, v, sg, gn, state, log_g):
    db = q.shape[0]
    row = lambda w: pl.BlockSpec((1, 1, w), lambda b, lg: (b, 0, 0))
    grid_spec = pltpu.PrefetchScalarGridSpec(
        num_scalar_prefetch=1,
        grid=(db,),
        in_specs=[row(RET_QK_W), row(RET_QK_W), row(RET_V_W), row(RET_V_W),
                  pl.BlockSpec((1, RET_V_DIM), lambda b, lg: (0, 0)),
                  pl.BlockSpec((1, 1, RET_HEADS, RET_QK_DIM, RET_V_DIM), lambda b, lg: (0, b, 0, 0, 0))],
        out_specs=[row(RET_V_W),
                   pl.BlockSpec((1, RET_HEADS, RET_QK_DIM, RET_V_DIM), lambda b, lg: (b, 0, 0, 0))],
    )
    return pl.pallas_call(
        _dec_ret_kernel,
        grid_spec=grid_spec,
        out_shape=[jax.ShapeDtypeStruct((db, 1, RET_V_W), f32),
                   jax.ShapeDtypeStruct((db, RET_HEADS, RET_QK_DIM, RET_V_DIM), f32)],
        compiler_params=_params(("arbitrary",)),
    )(log_g, q, k, v, sg, gn, state)


def _dec_score_kernel(pt_ref, qi_ref, w_ref, ik_ref, s_ref):
    s = jnp.maximum(_dot_nt(qi_ref[0], ik_ref[0].astype(bf16)), 0.0)
    s_ref[0, 0] = jnp.sum(s * w_ref[0], axis=0, keepdims=True)


def _decode_scores(page_table, qi, w, pool_ik):
    db, n_pages = page_table.shape
    grid_spec = pltpu.PrefetchScalarGridSpec(
        num_scalar_prefetch=1,
        grid=(db, n_pages),
        in_specs=[
            pl.BlockSpec((1, IDX_HEADS, IDX_DIM), lambda b, p, pt: (b, 0, 0)),
            pl.BlockSpec((1, IDX_HEADS, 1), lambda b, p, pt: (b, 0, 0)),
            pl.BlockSpec((1, PAGE_SIZE, IDX_DIM), lambda b, p, pt: (pt[b, p], 0, 0)),
        ],
        out_specs=pl.BlockSpec((1, 1, 1, PAGE_SIZE), lambda b, p, pt: (b, p, 0, 0)),
    )
    return pl.pallas_call(
        _dec_score_kernel,
        grid_spec=grid_spec,
        out_shape=jax.ShapeDtypeStruct((db, n_pages, 1, PAGE_SIZE), f32),
        compiler_params=_params(("arbitrary", "arbitrary")),
    )(page_table, qi, w, pool_ik)


def _dec_select_kernel(s_ref, qi_ref, w_ref, ikn_ref, bias_ref, nbias_ref, *, n_keep):
    ikn = ikn_ref[0].astype(bf16).astype(f32)
    dots = jnp.sum(qi_ref[0].astype(f32) * ikn, axis=1, keepdims=True)
    s_new = jnp.sum(jnp.maximum(dots, 0.0) * w_ref[0], axis=0, keepdims=True)
    keys = _sortable_key(s_ref[0])
    key_new = _sortable_key(s_new)

    def count_ge(t):
        c = jnp.sum(jnp.where(keys >= t, 1.0, 0.0), axis=1, keepdims=True)
        return jnp.sum(c, axis=0, keepdims=True) + jnp.where(key_new >= t, 1.0, 0.0)

    thr = _kth_largest_key(count_ge, n_keep, (1, 1))
    bias_ref[0] = jnp.where(keys >= thr, 0.0, -jnp.inf)
    nbias_ref[0] = jnp.broadcast_to(jnp.where(key_new >= thr, 0.0, -jnp.inf), (1, LANES))


def _decode_select(scores, qi, w, ik_new, n_keep):
    db, n_pages, _ = scores.shape
    kern = functools.partial(_dec_select_kernel, n_keep=n_keep)
    return pl.pallas_call(
        kern,
        grid=(db,),
        in_specs=[
            pl.BlockSpec((1, n_pages, PAGE_SIZE), lambda b: (b, 0, 0)),
            pl.BlockSpec((1, IDX_HEADS, IDX_DIM), lambda b: (b, 0, 0)),
            pl.BlockSpec((1, IDX_HEADS, 1), lambda b: (b, 0, 0)),
            pl.BlockSpec((1, 1, IDX_DIM), lambda b: (b, 0, 0)),
        ],
        out_specs=[pl.BlockSpec((1, n_pages, PAGE_SIZE), lambda b: (b, 0, 0)),
                   pl.BlockSpec((1, 1, LANES), lambda b: (b, 0, 0))],
        out_shape=[jax.ShapeDtypeStruct((db, n_pages, PAGE_SIZE), f32),
                   jax.ShapeDtypeStruct((db, 1, LANES), f32)],
        compiler_params=_params(("arbitrary",)),
    )(scores, qi, w, ik_new)


def _dec_attn_kernel(pt_ref, q_ref, k_ref, v_ref, bias_ref, kn_ref, vn_ref, nbias_ref, o_ref,
                     m_ref, l_ref, acc_ref):
    p = pl.program_id(1)

    @pl.when(p == 0)
    def _():
        m_ref[...] = jnp.full_like(m_ref, NEG_BIG)
        l_ref[...] = jnp.zeros_like(l_ref)
        acc_ref[...] = jnp.zeros_like(acc_ref)

    q = q_ref[0]
    kp = k_ref[0].astype(bf16)
    vp = v_ref[0].astype(bf16)
    grp = lax.broadcasted_iota(jnp.int32, (ATT_HEADS, 1), 0) // GROUP
    s = jnp.zeros((ATT_HEADS, PAGE_SIZE), f32)
    for g in range(ATT_KV_HEADS):
        sg = _dot_nt(q, kp[:, g * HEAD_DIM:(g + 1) * HEAD_DIM])
        s = jnp.where(grp == g, sg, s)
    s = s + bias_ref[0, 0]
    m_old = m_ref[...]
    m_new = jnp.maximum(m_old, jnp.max(s, axis=1, keepdims=True))
    pr = jnp.exp(s - m_new)
    alpha = jnp.exp(m_old - m_new)
    prb = pr.astype(bf16)
    pv = jnp.zeros((ATT_HEADS, HEAD_DIM), f32)
    for g in range(ATT_KV_HEADS):
        og = _dot(prb, vp[:, g * HEAD_DIM:(g + 1) * HEAD_DIM])
        pv = jnp.where(grp == g, og, pv)
    l_new = alpha * l_ref[...] + jnp.sum(pr, axis=1, keepdims=True)
    acc_new = alpha * acc_ref[...] + pv
    m_ref[...] = m_new
    l_ref[...] = l_new
    acc_ref[...] = acc_new

    @pl.when(p == pl.num_programs(1) - 1)
    def _():
        s_n = jnp.sum(q.astype(f32) * kn_ref[0].astype(f32), axis=1, keepdims=True) + nbias_ref[0][:, 0:1]
        m_f = jnp.maximum(m_new, s_n)
        a = jnp.exp(m_new - m_f)
        p_n = jnp.exp(s_n - m_f)
        p_nb = p_n.astype(bf16).astype(f32)
        o_ref[0] = (a * acc_new + p_nb * vn_ref[0].astype(f32)) / (a * l_new + p_n)


def _decode_attention(page_table, q, pool_k, pool_v, bias, k_new, v_new, nbias):
    db, n_pages = page_table.shape
    seqb = lambda shape: pl.BlockSpec(shape, lambda b, p, pt: (b, 0, 0))
    grid_spec = pltpu.PrefetchScalarGridSpec(
        num_scalar_prefetch=1,
        grid=(db, n_pages),
        in_specs=[
            seqb((1, ATT_HEADS, HEAD_DIM)),
            pl.BlockSpec((1, PAGE_SIZE, ATT_KV_W), lambda b, p, pt: (pt[b, p], 0, 0)),
            pl.BlockSpec((1, PAGE_SIZE, ATT_KV_W), lambda b, p, pt: (pt[b, p], 0, 0)),
            pl.BlockSpec((1, 1, 1, PAGE_SIZE), lambda b, p, pt: (b, p, 0, 0)),
            seqb((1, ATT_HEADS, HEAD_DIM)),
            seqb((1, ATT_HEADS, HEAD_DIM)),
            seqb((1, 1, LANES)),
        ],
        out_specs=seqb((1, ATT_HEADS, HEAD_DIM)),
        scratch_shapes=[pltpu.VMEM((ATT_HEADS, 1), f32), pltpu.VMEM((ATT_HEADS, 1), f32),
                        pltpu.VMEM((ATT_HEADS, HEAD_DIM), f32)],
    )
    return pl.pallas_call(
        _dec_attn_kernel,
        grid_spec=grid_spec,
        out_shape=jax.ShapeDtypeStruct((db, ATT_HEADS, HEAD_DIM), f32),
        compiler_params=_params(("arbitrary", "arbitrary")),
    )(page_table, q, pool_k, pool_v, bias, k_new, v_new, nbias)


def _rope_tables(pos):
    half = RET_QK_DIM // 2
    inv = ROPE_BASE ** (-jnp.arange(half, dtype=f32) / half)
    ang = pos.astype(f32)[:, None] * inv[None, :]
    cos, sin = jnp.cos(ang), jnp.sin(ang)
    return jnp.concatenate([cos, cos], axis=1), jnp.concatenate([-sin, sin], axis=1)


def _pad_w_in(w_in):
    d = w_in.shape[0]
    small = w_in[:, MAIN_W:MAIN_W + IDX_DIM + IDX_HEADS]
    pad = jnp.zeros((d, TN - IDX_DIM - IDX_HEADS), w_in.dtype)
    gates = w_in[:, MAIN_W + IDX_DIM + IDX_HEADS:]
    return jnp.concatenate([gates, w_in[:, :MAIN_W], small, pad], axis=1).astype(bf16)


def _pick(n, pref):
    for t in pref:
        if n % t == 0:
            return t
    return n


def kernel(x_prompt, x_sample, state_ret, cache_k, cache_v, cache_idx_k, page_table, ffn1_norm, ffn1_w1, ffn1_w2, mix_norm, w_in, q_norm, k_norm, ret_norm, w_ret_out, w_att_out, w_o, ffn2_norm, ffn2_w1, ffn2_w2):
    batch, seq, d = x_prompt.shape
    db, ts, _ = x_sample.shape
    depth = w_in.shape[0]
    assert depth == 1 and ts == 1 and d % (2 * TN) == 0
    n_pages = page_table.shape[1]
    past = n_pages * PAGE_SIZE
    n_phys = cache_k.shape[1]
    base = 2 * (d // TN)
    m = batch * seq
    ms = 16

    log_g = jnp.log1p(-jnp.exp2(-5.0 - jnp.arange(RET_HEADS, dtype=f32)))
    w1a, w2a = ffn1_w1[0].astype(bf16), ffn1_w2[0].astype(bf16)
    w1b, w2b = ffn2_w1[0].astype(bf16), ffn2_w2[0].astype(bf16)
    w_pad = _pad_w_in(w_in[0])
    wr, wa, wo = w_ret_out[0].astype(bf16), w_att_out[0].astype(bf16), w_o[0].astype(bf16)
    g1, g2, g3 = ffn1_norm, mix_norm, ffn2_norm
    qn, kn, gn = q_norm, k_norm, ret_norm

    tm = _pick(m, (512, 256, 128))
    tf = _pick(ffn1_w2.shape[1], (512, 256, 128))
    tmp = _pick(seq, (1024, 512, 256, 128))
    chunk = _pick(seq, (256, 128))
    tq = _pick(seq, (128,))
    tmo = _pick(m, (256, 128))

    xp = x_prompt.reshape(m, d)
    x1 = _ffn(xp, g1, w1a, w2a, tm, tf)
    cos_p, sin_p = _rope_tables(jnp.arange(seq))
    p, k32, v32, ikw = _proj(x1, g2, w_pad, cos_p, sin_p, qn, kn, tmp)
    og, st_p = _retention(p, log_g, gn, batch, seq, base, chunk)
    att = _prompt_attention(p, ikw, batch, seq, base, tq, min(TOPK_MAX, seq // 4))
    x2 = _oproj(og, att, p, x1, wr, wa, wo, tmo)
    yp = _ffn(x2, g3, w1b, w2b, tm, tf).reshape(batch, seq, d)

    xs = jnp.zeros((ms, d), f32).at[:db].set(x_sample.reshape(db, d))
    s1 = _ffn(xs, g1, w1a, w2a, ms, tf)
    cos_s, sin_s = _rope_tables(jnp.full((ms,), past, jnp.int32))
    ps, k32s, v32s, ikws = _proj(s1, g2, w_pad, cos_s, sin_s, qn, kn, ms)
    seg = lambda t0, t1: ps[:db, (base + t0) * TN:(base + t1) * TN]
    row3 = lambda a: a.astype(f32).reshape(db, 1, a.shape[-1])
    og_s, st_s = _decode_retention(row3(seg(T_RQ, T_RK)), row3(seg(T_RK, T_RV)), row3(seg(T_RV, T_RG)),
                                   row3(seg(T_RG, T_AQ)), gn, state_ret, log_g)
    qi = seg(T_IQ, T_IKW).reshape(db, IDX_HEADS, IDX_DIM)
    wi = (ikws[:db, IDX_DIM:IDX_DIM + IDX_HEADS] * IDX_W_SCALE).reshape(db, IDX_HEADS, 1)
    ik_new = ikws[:db, :IDX_DIM].reshape(db, 1, IDX_DIM)
    scores = _decode_scores(page_table, qi, wi, cache_idx_k.reshape(n_phys, PAGE_SIZE, IDX_DIM))
    bias, nbias = _decode_select(scores.reshape(db, n_pages, PAGE_SIZE), qi, wi, ik_new,
                                 min(TOPK_MAX, (past + ts) // 4))
    aq_s = seg(T_AQ, T_AK).reshape(db, ATT_HEADS, HEAD_DIM)
    expand = lambda a: jnp.repeat(a[:db].reshape(db, ATT_KV_HEADS, HEAD_DIM), GROUP, axis=1).astype(bf16)
    att_s = _decode_attention(page_table, aq_s,
                              cache_k.reshape(n_phys, PAGE_SIZE, ATT_KV_W),
                              cache_v.reshape(n_phys, PAGE_SIZE, ATT_KV_W),
                              bias.reshape(db, n_pages, 1, PAGE_SIZE), expand(k32s), expand(v32s), nbias)
    pad_rows = lambda a: jnp.pad(a.reshape(db, -1).astype(bf16), ((0, ms - db), (0, 0)))
    s2 = _oproj(pad_rows(og_s), pad_rows(att_s), ps, s1, wr, wa, wo, ms)
    ys = _ffn(s2, g3, w1b, w2b, ms, tf)[:db].reshape(db, ts, d)

    return (yp, ys,
            st_p[None],
            k32.reshape(1, batch, seq, ATT_KV_HEADS, HEAD_DIM),
            v32.reshape(1, batch, seq, ATT_KV_HEADS, HEAD_DIM),
            ikw[:, :IDX_DIM].reshape(1, batch, seq, IDX_DIM),
            st_s[None],
            k32s[:db].reshape(1, db, ts, ATT_KV_HEADS, HEAD_DIM),
            v32s[:db].reshape(1, db, ts, ATT_KV_HEADS, HEAD_DIM),
            ikws[:db, :IDX_DIM].reshape(1, db, ts, IDX_DIM))
```

```python
import functools

import jax
import jax.numpy as jnp
from jax import lax
from jax.experimental import pallas as pl
from jax.experimental.pallas import tpu as pltpu

RET_HEADS = 8
RET_QK_DIM = 128
RET_V_DIM = 256
ATT_HEADS = 16
ATT_KV_HEADS = 4
HEAD_DIM = 128
GROUP = ATT_HEADS // ATT_KV_HEADS
IDX_HEADS = 16
IDX_DIM = 64
IDX_W_SCALE = (IDX_HEADS ** -0.5) * (IDX_DIM ** -0.5)
TOPK_MAX = 256
PAGE_SIZE = 128
ROPE_BASE = 10000.0
EPS = 1e-6

RET_QK_W = RET_HEADS * RET_QK_DIM
RET_V_W = RET_HEADS * RET_V_DIM
ATT_Q_W = ATT_HEADS * HEAD_DIM
ATT_KV_W = ATT_KV_HEADS * HEAD_DIM
IDX_Q_W = IDX_HEADS * IDX_DIM
MAIN_W = 2 * RET_QK_W + 2 * RET_V_W + ATT_Q_W + 2 * ATT_KV_W + IDX_Q_W

LANES = 128
TN = 512
T_RQ, T_RK, T_RV, T_RG, T_AQ, T_AK, T_AV, T_IQ, T_IKW, T_END = 0, 2, 4, 8, 12, 16, 17, 18, 20, 21
VMEM_LIMIT = 56 * 1024 * 1024
INT_MIN = -2 ** 31
NEG_BIG = -1e30

bf16 = jnp.bfloat16
f32 = jnp.float32


def _params(sem, vmem=VMEM_LIMIT):
    return pltpu.CompilerParams(dimension_semantics=sem, vmem_limit_bytes=vmem)


def _sigmoid(x):
    return 1.0 / (1.0 + jnp.exp(-x))


def _dot(a, b):
    return jnp.dot(a, b, preferred_element_type=f32)


def _dot_nt(a, b):
    return lax.dot_general(a, b, (((1,), (1,)), ((), ())), preferred_element_type=f32)


def _rms(x, gain):
    ms = jnp.mean(x * x, axis=-1, keepdims=True)
    return x * lax.rsqrt(ms + EPS) * gain


def _ffn_kernel(x_ref, g_ref, w1g_ref, w1u_ref, w2_ref, o_ref, h_ref, acc_ref):
    j = pl.program_id(1)

    @pl.when(j == 0)
    def _():
        h_ref[...] = _rms(x_ref[...], g_ref[...]).astype(bf16)
        acc_ref[...] = jnp.zeros_like(acc_ref)

    h = h_ref[...]
    gate = _dot(h, w1g_ref[...])
    up = _dot(h, w1u_ref[...])
    act = (gate * _sigmoid(gate) * up).astype(bf16)
    acc_ref[...] += _dot(act, w2_ref[...])

    @pl.when(j == pl.num_programs(1) - 1)
    def _():
        o_ref[...] = x_ref[...] + 0.5 * acc_ref[...]


def _ffn(x, gain, w1, w2, tm, tf):
    m, d = x.shape
    dff = w2.shape[0]
    nf = dff // tf
    return pl.pallas_call(
        _ffn_kernel,
        grid=(m // tm, nf),
        in_specs=[
            pl.BlockSpec((tm, d), lambda i, j: (i, 0)),
            pl.BlockSpec((1, d), lambda i, j: (0, 0)),
            pl.BlockSpec((d, tf), lambda i, j: (0, j)),
            pl.BlockSpec((d, tf), lambda i, j: (0, j + nf)),
            pl.BlockSpec((tf, d), lambda i, j: (j, 0)),
        ],
        out_specs=pl.BlockSpec((tm, d), lambda i, j: (i, 0)),
        out_shape=jax.ShapeDtypeStruct((m, d), f32),
        scratch_shapes=[pltpu.VMEM((tm, d), bf16), pltpu.VMEM((tm, d), f32)],
        compiler_params=_params(("arbitrary", "arbitrary")),
    )(x, gain, w1, w1, w2)


def _proj_kernel(x_ref, g_ref, w_ref, cos_ref, sin_ref, qn_ref, kn_ref,
                 p_ref, k32_ref, v32_ref, ikw_ref, h_ref, *, base):
    j = pl.program_id(1)

    @pl.when(j == 0)
    def _():
        h_ref[...] = _rms(x_ref[...], g_ref[...]).astype(bf16)

    res = _dot(h_ref[...], w_ref[...])
    heads = [slice(c * LANES, (c + 1) * LANES) for c in range(TN // LANES)]

    @pl.when(j < base)
    def _():
        p_ref[...] = _sigmoid(res).astype(bf16)

    @pl.when((j >= base + T_RQ) & (j < base + T_RV))
    def _():
        cos, sin = cos_ref[...], sin_ref[...]
        scale = jnp.where(j >= base + T_RK, RET_QK_DIM ** -0.5, 1.0).astype(f32)
        for sl in heads:
            x = res[:, sl]
            p_ref[:, sl] = ((x * cos + pltpu.roll(x, LANES // 2, 1) * sin) * scale).astype(bf16)

    @pl.when(((j >= base + T_RV) & (j < base + T_RG)) | ((j >= base + T_IQ) & (j < base + T_IKW)))
    def _():
        p_ref[...] = res.astype(bf16)

    @pl.when((j >= base + T_RG) & (j < base + T_AQ))
    def _():
        p_ref[...] = (res * _sigmoid(res)).astype(bf16)

    @pl.when((j >= base + T_AQ) & (j < base + T_AK))
    def _():
        for sl in heads:
            p_ref[:, sl] = (_rms(res[:, sl], qn_ref[...]) * HEAD_DIM ** -0.5).astype(bf16)

    @pl.when(j == base + T_AK)
    def _():
        for sl in heads:
            y = _rms(res[:, sl], kn_ref[...])
            k32_ref[:, sl] = y
            p_ref[:, sl] = y.astype(bf16)

    @pl.when(j == base + T_AV)
    def _():
        v32_ref[...] = res
        p_ref[...] = res.astype(bf16)

    @pl.when(j == base + T_IKW)
    def _():
        ikw_ref[...] = res[:, :LANES]
        p_ref[...] = res.astype(bf16)


def _proj(x, gain, w_pad, cos2, sin2, qn, kn, tm):
    m, d = x.shape
    nt = w_pad.shape[1] // TN
    base = nt - T_END
    tab_blocks = cos2.shape[0] // tm
    kern = functools.partial(_proj_kernel, base=base)
    return pl.pallas_call(
        kern,
        grid=(m // tm, nt),
        in_specs=[
            pl.BlockSpec((tm, d), lambda i, j: (i, 0)),
            pl.BlockSpec((1, d), lambda i, j: (0, 0)),
            pl.BlockSpec((d, TN), lambda i, j: (0, j)),
            pl.BlockSpec((tm, LANES), lambda i, j: (i % tab_blocks, 0)),
            pl.BlockSpec((tm, LANES), lambda i, j: (i % tab_blocks, 0)),
            pl.BlockSpec((1, LANES), lambda i, j: (0, 0)),
            pl.BlockSpec((1, LANES), lambda i, j: (0, 0)),
        ],
        out_specs=[
            pl.BlockSpec((tm, TN), lambda i, j: (i, j)),
            pl.BlockSpec((tm, TN), lambda i, j: (i, 0)),
            pl.BlockSpec((tm, TN), lambda i, j: (i, 0)),
            pl.BlockSpec((tm, LANES), lambda i, j: (i, 0)),
        ],
        out_shape=[
            jax.ShapeDtypeStruct((m, nt * TN), bf16),
            jax.ShapeDtypeStruct((m, ATT_KV_W), f32),
            jax.ShapeDtypeStruct((m, ATT_KV_W), f32),
            jax.ShapeDtypeStruct((m, LANES), f32),
        ],
        scratch_shapes=[pltpu.VMEM((tm, d), bf16)],
        compiler_params=_params(("arbitrary", "arbitrary")),
    )(x, gain, w_pad, cos2, sin2, qn, kn)


def _ret_kernel(lg_ref, q_ref, k_ref, v_ref, sg_ref, gn_ref, og_ref, st_ref, state_ref, *, chunk):
    h = pl.program_id(1)
    c = pl.program_id(2)
    lg = lg_ref[h]

    @pl.when(c == 0)
    def _():
        state_ref[...] = jnp.zeros_like(state_ref)

    q, k, v = q_ref[...], k_ref[...], v_ref[...]
    ii = lax.broadcasted_iota(jnp.int32, (chunk, chunk), 0)
    jj = lax.broadcasted_iota(jnp.int32, (chunk, chunk), 1)
    rel = (ii - jj).astype(f32)
    dmat = jnp.where(rel >= 0, jnp.exp(lg * jnp.maximum(rel, 0.0)), 0.0)
    scores = _dot_nt(q, k) * dmat
    inner = _dot(scores.astype(bf16), v)
    i1 = lax.broadcasted_iota(jnp.int32, (chunk, 1), 0).astype(f32)
    q_dec = jnp.exp(lg * (i1 + 1.0))
    k_dec = jnp.exp(lg * (chunk - 1.0 - i1))
    state = state_ref[...]
    cross = _dot((q.astype(f32) * q_dec).astype(bf16), state.astype(bf16))
    kd = (k.astype(f32) * k_dec).astype(bf16)
    new_state = jnp.exp(lg * chunk) * state + lax.dot_general(
        kd, v, (((0,), (0,)), ((), ())), preferred_element_type=f32)
    state_ref[...] = new_state
    o = _rms(inner + cross, gn_ref[...])
    og_ref[...] = (o * sg_ref[...].astype(f32)).astype(bf16)

    @pl.when(c == pl.num_programs(2) - 1)
    def _():
        st_ref[0, 0] = new_state


def _retention(p, log_g, gn, batch, seq, base, chunk):
    nc = seq // chunk
    qb = (base + T_RQ) * (TN // RET_QK_DIM)
    kb = (base + T_RK) * (TN // RET_QK_DIM)
    vb = (base + T_RV) * (TN // RET_V_DIM)
    gb = (base + T_RG) * (TN // RET_V_DIM)
    kern = functools.partial(_ret_kernel, chunk=chunk)
    grid_spec = pltpu.PrefetchScalarGridSpec(
        num_scalar_prefetch=1,
        grid=(batch, RET_HEADS, nc),
        in_specs=[
            pl.BlockSpec((chunk, RET_QK_DIM), lambda b, h, c, lg: (b * nc + c, qb + h)),
            pl.BlockSpec((chunk, RET_QK_DIM), lambda b, h, c, lg: (b * nc + c, kb + h)),
            pl.BlockSpec((chunk, RET_V_DIM), lambda b, h, c, lg: (b * nc + c, vb + h)),
            pl.BlockSpec((chunk, RET_V_DIM), lambda b, h, c, lg: (b * nc + c, gb + h)),
            pl.BlockSpec((1, RET_V_DIM), lambda b, h, c, lg: (0, 0)),
        ],
        out_specs=[
            pl.BlockSpec((chunk, RET_V_DIM), lambda b, h, c, lg: (b * nc + c, h)),
            pl.BlockSpec((1, 1, RET_QK_DIM, RET_V_DIM), lambda b, h, c, lg: (b, h, 0, 0)),
        ],
        scratch_shapes=[pltpu.VMEM((RET_QK_DIM, RET_V_DIM), f32)],
    )
    return pl.pallas_call(
        kern,
        grid_spec=grid_spec,
        out_shape=[
            jax.ShapeDtypeStruct((batch * seq, RET_V_W), bf16),
            jax.ShapeDtypeStruct((batch, RET_HEADS, RET_QK_DIM, RET_V_DIM), f32),
        ],
        compiler_params=_params(("arbitrary", "arbitrary", "arbitrary")),
    )(log_g, p, p, p, p, gn)


def _sortable_key(score):
    bits = lax.bitcast_convert_type(score, jnp.int32)
    return jnp.where(bits >= 0, bits, bits ^ jnp.int32(0x7FFFFFFF))


def _kth_largest_key(count_ge, n_keep, shape):
    def body(it, t_u):
        bit = jnp.left_shift(jnp.int32(1), 31 - it)
        cand_u = t_u | bit
        cnt = count_ge(cand_u ^ jnp.int32(INT_MIN))
        return jnp.where(cnt >= n_keep, cand_u, t_u)
    t_u = lax.fori_loop(0, 32, body, jnp.zeros(shape, jnp.int32))
    return t_u ^ jnp.int32(INT_MIN)


def _attn_kernel(aq_ref, k_ref, v_ref, iq_ref, ikw_k_ref, ikw_q_ref, o_ref, key_ref, *, tq, seq, n_keep):
    qb = pl.program_id(1)
    ikw = ikw_k_ref[...]
    lane = lax.broadcasted_iota(jnp.int32, (seq, LANES), 1)
    ik_lo = jnp.where(lane < IDX_DIM, ikw, 0.0).astype(bf16)
    ik_hi = jnp.where(lane >= IDX_DIM, pltpu.roll(ikw, IDX_DIM, 1), 0.0).astype(bf16)
    iwq = ikw_q_ref[...] * IDX_W_SCALE

    scores = jnp.zeros((tq, seq), f32)
    for hp in range(IDX_HEADS // 2):
        qp = iq_ref[:, hp * LANES:(hp + 1) * LANES]
        for sub, ikx in enumerate((ik_lo, ik_hi)):
            col = IDX_DIM + 2 * hp + sub
            s = jnp.maximum(_dot_nt(qp, ikx), 0.0)
            scores = scores + s * iwq[:, col:col + 1]

    q_pos = qb * tq + lax.broadcasted_iota(jnp.int32, (tq, seq), 0)
    k_pos = lax.broadcasted_iota(jnp.int32, (tq, seq), 1)
    visible = k_pos <= q_pos
    key_ref[...] = _sortable_key(jnp.where(visible, scores, -jnp.inf))

    def count_ge(t):
        return jnp.sum(jnp.where(key_ref[...] >= t, 1.0, 0.0), axis=1, keepdims=True)

    thr = _kth_largest_key(count_ge, n_keep, (tq, 1))
    bias = jnp.where((key_ref[...] >= thr) & visible, 0.0, -jnp.inf)

    for g in range(ATT_KV_HEADS):
        kg = k_ref[:, g * HEAD_DIM:(g + 1) * HEAD_DIM]
        vg = v_ref[:, g * HEAD_DIM:(g + 1) * HEAD_DIM]
        for i in range(GROUP):
            hd = g * GROUP + i
            q = aq_ref[:, hd * HEAD_DIM:(hd + 1) * HEAD_DIM]
            s = _dot_nt(q, kg) + bias
            m = jnp.max(s, axis=1, keepdims=True)
            p = jnp.exp(s - m)
            l = jnp.sum(p, axis=1, keepdims=True)
            o = _dot(p.astype(bf16), vg) / l
            o_ref[:, hd * HEAD_DIM:(hd + 1) * HEAD_DIM] = o.astype(bf16)


def _prompt_attention(p, ikw, batch, seq, base, tq, n_keep):
    nq = seq // tq
    kern = functools.partial(_attn_kernel, tq=tq, seq=seq, n_keep=n_keep)
    aq_blk = (base + T_AQ) * TN // ATT_Q_W
    iq_blk = (base + T_IQ) * TN // IDX_Q_W
    return pl.pallas_call(
        kern,
        grid=(batch, nq),
        in_specs=[
            pl.BlockSpec((tq, ATT_Q_W), lambda b, t: (b * nq + t, aq_blk)),
            pl.BlockSpec((seq, ATT_KV_W), lambda b, t: (b, base + T_AK)),
            pl.BlockSpec((seq, ATT_KV_W), lambda b, t: (b, base + T_AV)),
            pl.BlockSpec((tq, IDX_Q_W), lambda b, t: (b * nq + t, iq_blk)),
            pl.BlockSpec((seq, LANES), lambda b, t: (b, 0)),
            pl.BlockSpec((tq, LANES), lambda b, t: (b * nq + t, 0)),
        ],
        out_specs=pl.BlockSpec((tq, ATT_Q_W), lambda b, t: (b * nq + t, 0)),
        out_shape=jax.ShapeDtypeStruct((batch * seq, ATT_Q_W), bf16),
        scratch_shapes=[pltpu.VMEM((tq, seq), jnp.int32)],
        compiler_params=_params(("arbitrary", "arbitrary")),
    )(p, p, p, p, ikw, ikw)


def _oproj_kernel(og_ref, att_ref, gr_ref, ga_ref, x_ref, wr_ref, wa_ref, wo_ref, o_ref):
    ret_out = _dot(og_ref[...], wr_ref[...])
    att_out = _dot(att_ref[...], wa_ref[...])
    merged = gr_ref[...].astype(f32) * ret_out + ga_ref[...].astype(f32) * att_out
    o_ref[...] = x_ref[...] + _dot(merged.astype(bf16), wo_ref[...])


def _oproj(og, att, p, x, wr, wa, wo, tm):
    m, d = x.shape
    const = lambda shape: pl.BlockSpec(shape, lambda i: (0, 0), pipeline_mode=pl.Buffered(1))
    return pl.pallas_call(
        _oproj_kernel,
        grid=(m // tm,),
        in_specs=[
            pl.BlockSpec((tm, RET_V_W), lambda i: (i, 0)),
            pl.BlockSpec((tm, ATT_Q_W), lambda i: (i, 0)),
            pl.BlockSpec((tm, d), lambda i: (i, 0)),
            pl.BlockSpec((tm, d), lambda i: (i, 1)),
            pl.BlockSpec((tm, d), lambda i: (i, 0)),
            const(wr.shape), const(wa.shape), const(wo.shape),
        ],
        out_specs=pl.BlockSpec((tm, d), lambda i: (i, 0)),
        out_shape=jax.ShapeDtypeStruct((m, d), f32),
        compiler_params=_params(("arbitrary",)),
    )(og, att, p, p, x, wr, wa, wo)


def _dec_ret_kernel(lg_ref, q_ref, k_ref, v_ref, sg_ref, gn_ref, st_ref, og_ref, nst_ref):
    for h in range(RET_HEADS):
        lg = lg_ref[h]
        gamma = jnp.exp(lg)
        q = q_ref[0, :, h * RET_QK_DIM:(h + 1) * RET_QK_DIM]
        k = k_ref[0, :, h * RET_QK_DIM:(h + 1) * RET_QK_DIM]
        v = v_ref[0, :, h * RET_V_DIM:(h + 1) * RET_V_DIM]
        sg = sg_ref[0, :, h * RET_V_DIM:(h + 1) * RET_V_DIM]
        state = st_ref[0, 0, h]
        qk = jnp.sum(q * k, axis=1, keepdims=True)
        q8 = jnp.broadcast_to((q * gamma).astype(bf16), (8, RET_QK_DIM))
        cross = _dot(q8, state.astype(bf16))[0:1]
        k_col = jnp.transpose(jnp.broadcast_to(k, (8, RET_QK_DIM)))[:, 0:1]
        nst_ref[0, h] = gamma * state + k_col * v
        o = _rms(qk * v + cross, gn_ref[...])
        og_ref[0, :, h * RET_V_DIM:(h + 1) * RET_V_DIM] = o * sg


def _decode_retention(q, k, v, sg, gn, state, log_g):
    db = q.shape[0]
    row = lambda w: pl.BlockSpec((1, 1, w), lambda b, lg: (b, 0, 0))
    grid_spec = pltpu.PrefetchScalarGridSpec(
        num_scalar_prefetch=1,
        grid=(db,),
        in_specs=[row(RET_QK_W), row(RET_QK_W), row(RET_V_W), row(RET_V_W),
                  pl.BlockSpec((1, RET_V_DIM), lambda b, lg: (0, 0)),
                  pl.BlockSpec((1, 1, RET_HEADS, RET_QK_DIM, RET_V_DIM), lambda b, lg: (0, b, 0, 0, 0))],
        out_specs=[row(RET_V_W),
                   pl.BlockSpec((1, RET_HEADS, RET_QK_DIM, RET_V_DIM), lambda b, lg: (b, 0, 0, 0))],
    )
    return pl.pallas_call(
        _dec_ret_kernel,
        grid_spec=grid_spec,
        out_shape=[jax.ShapeDtypeStruct((db, 1, RET_V_W), f32),
                   jax.ShapeDtypeStruct((db, RET_HEADS, RET_QK_DIM, RET_V_DIM), f32)],
        compiler_params=_params(("arbitrary",)),
    )(log_g, q, k, v, sg, gn, state)


def _dec_score_kernel(pt_ref, qi_ref, w_ref, pool_ref, s_ref, buf_ref, sem_ref, *, n_pages):
    b = pl.program_id(0)
    slot = b % 2

    def page_copy(seq, p, slot_):
        dst = buf_ref.at[slot_, :, pl.ds(pl.multiple_of(p * PAGE_SIZE, PAGE_SIZE), PAGE_SIZE)]
        return pltpu.make_async_copy(pool_ref.at[pt_ref[seq, p]], dst, sem_ref.at[slot_])

    def start_all(seq, slot_):
        def body(p, c):
            page_copy(seq, p, slot_).start()
            return c
        lax.fori_loop(0, n_pages, body, 0)

    @pl.when(b == 0)
    def _():
        start_all(0, 0)

    @pl.when(b + 1 < pl.num_programs(0))
    def _():
        start_all(b + 1, 1 - slot)

    def wait_body(p, c):
        page_copy(b, p, slot).wait()
        return c
    lax.fori_loop(0, n_pages, wait_body, 0)

    s = jnp.maximum(_dot(qi_ref[0], buf_ref[slot].astype(bf16)), 0.0)
    row = jnp.sum(s * w_ref[0], axis=0, keepdims=True)
    for p in range(n_pages):
        s_ref[0, p:p + 1, :] = row[:, p * PAGE_SIZE:(p + 1) * PAGE_SIZE]


def _decode_scores(page_table, qi, w, pool_ikt):
    db, n_pages = page_table.shape
    kern = functools.partial(_dec_score_kernel, n_pages=n_pages)
    grid_spec = pltpu.PrefetchScalarGridSpec(
        num_scalar_prefetch=1,
        grid=(db,),
        in_specs=[
            pl.BlockSpec((1, IDX_HEADS, IDX_DIM), lambda b, pt: (b, 0, 0)),
            pl.BlockSpec((1, IDX_HEADS, 1), lambda b, pt: (b, 0, 0)),
            pl.BlockSpec(memory_space=pl.ANY),
        ],
        out_specs=pl.BlockSpec((1, n_pages, PAGE_SIZE), lambda b, pt: (b, 0, 0)),
        scratch_shapes=[pltpu.VMEM((2, IDX_DIM, n_pages * PAGE_SIZE), f32), pltpu.SemaphoreType.DMA((2,))],
    )
    return pl.pallas_call(
        kern,
        grid_spec=grid_spec,
        out_shape=jax.ShapeDtypeStruct((db, n_pages, PAGE_SIZE), f32),
        compiler_params=_params(("arbitrary",)),
    )(page_table, qi, w, pool_ikt)


def _dec_select_kernel(s_ref, qi_ref, w_ref, ikn_ref, idx_ref, cnt_ref, nsel_ref, rank_ref,
                       *, n_keep, kmax):
    db, n_pages, _ = s_ref.shape
    ikn = ikn_ref[...].astype(bf16).astype(f32)
    dots = jnp.sum(qi_ref[...].astype(f32) * ikn, axis=2, keepdims=True)
    s_new = jnp.sum(jnp.maximum(dots, 0.0) * w_ref[...], axis=1, keepdims=True)
    keys = _sortable_key(s_ref[...])
    key_new = _sortable_key(s_new)

    def count_ge(t):
        c = jnp.sum(jnp.where(keys >= t, 1.0, 0.0), axis=1, keepdims=True)
        return jnp.sum(c, axis=2, keepdims=True) + jnp.where(key_new >= t, 1.0, 0.0)

    thr = _kth_largest_key(count_ge, n_keep, (db, 1, 1))
    sel = jnp.where(keys >= thr, 1.0, 0.0)
    nsel_ref[...] = jnp.broadcast_to(jnp.where(key_new >= thr, 1.0, 0.0), (db, 1, LANES))

    pp = lax.broadcasted_iota(jnp.int32, (n_pages, n_pages), 0)
    pq = lax.broadcasted_iota(jnp.int32, (n_pages, n_pages), 1)
    earlier_page = jnp.where(pq < pp, 1.0, 0.0).astype(bf16)
    oo = lax.broadcasted_iota(jnp.int32, (PAGE_SIZE, PAGE_SIZE), 0)
    oq = lax.broadcasted_iota(jnp.int32, (PAGE_SIZE, PAGE_SIZE), 1)
    earlier_off = jnp.where(oo < oq, 1.0, 0.0).astype(bf16)
    r_iota = lax.broadcasted_iota(jnp.int32, (kmax, PAGE_SIZE), 0).astype(f32)
    lane = lax.broadcasted_iota(jnp.int32, (kmax, PAGE_SIZE), 1).astype(f32)
    for b in range(db):
        sb = sel[b]
        per_page = jnp.sum(sb, axis=1, keepdims=True)
        before = _dot(earlier_page, jnp.broadcast_to(per_page, (n_pages, PAGE_SIZE)).astype(bf16))
        within = _dot(sb.astype(bf16), earlier_off)
        rank_ref[b] = jnp.where(sb > 0.0, before + within, -1.0)

        def body(p, acc, b=b):
            row = rank_ref[b, pl.ds(p, 1), :]
            return jnp.where(row == r_iota, lane + jnp.asarray(p * PAGE_SIZE, f32), acc)

        acc = lax.fori_loop(0, n_pages, body, jnp.zeros((kmax, PAGE_SIZE), f32))
        idx_ref[b] = jnp.sum(acc, axis=1, keepdims=True).astype(jnp.int32)
        total = jnp.sum(per_page, axis=0, keepdims=True)
        cnt_ref[b] = jnp.broadcast_to(total, (1, LANES)).astype(jnp.int32)


def _decode_select(scores, qi, w, ik_new, n_keep):
    db, n_pages, _ = scores.shape
    kmax = -(-n_keep // 8) * 8
    kern = functools.partial(_dec_select_kernel, n_keep=n_keep, kmax=kmax)
    full = lambda shape: pl.BlockSpec(shape, lambda i: (0,) * len(shape))
    return pl.pallas_call(
        kern,
        grid=(1,),
        in_specs=[full(scores.shape), full(qi.shape), full(w.shape), full(ik_new.shape)],
        out_specs=[full((db, kmax, 1)), full((db, 1, LANES)), full((db, 1, LANES))],
        out_shape=[jax.ShapeDtypeStruct((db, kmax, 1), jnp.int32),
                   jax.ShapeDtypeStruct((db, 1, LANES), jnp.int32),
                   jax.ShapeDtypeStruct((db, 1, LANES), f32)],
        scratch_shapes=[pltpu.VMEM((db, n_pages, PAGE_SIZE), f32)],
        compiler_params=_params(("arbitrary",)),
    )(scores, qi, w, ik_new)


def _dec_attn_kernel(idx_ref, pt_ref, cnt_ref, q_ref, kn_ref, vn_ref, nsel_ref, kpool_ref, vpool_ref, o_ref,
                     kbuf_ref, vbuf_ref, sem_ref, *, kmax):
    b = pl.program_id(0)
    slot = b % 2
    page_shift = PAGE_SIZE.bit_length() - 1

    def row_copies(seq, r, slot_):
        pos = idx_ref[seq, r]
        phys = pt_ref[seq, lax.shift_right_logical(pos, page_shift)]
        src0 = pl.multiple_of((pos & (PAGE_SIZE - 1)) * ATT_KV_HEADS, ATT_KV_HEADS)
        dst0 = pl.multiple_of(r * ATT_KV_HEADS, ATT_KV_HEADS)
        return (pltpu.make_async_copy(kpool_ref.at[phys, pl.ds(src0, ATT_KV_HEADS), :],
                                      kbuf_ref.at[slot_, pl.ds(dst0, ATT_KV_HEADS), :], sem_ref.at[0, slot_]),
                pltpu.make_async_copy(vpool_ref.at[phys, pl.ds(src0, ATT_KV_HEADS), :],
                                      vbuf_ref.at[slot_, pl.ds(dst0, ATT_KV_HEADS), :], sem_ref.at[1, slot_]))

    def start_all(seq, slot_):
        def body(r, c):
            ck, cv = row_copies(seq, r, slot_)
            ck.start()
            cv.start()
            return c
        lax.fori_loop(0, kmax, body, 0)

    @pl.when(b == 0)
    def _():
        start_all(0, 0)

    @pl.when(b + 1 < pl.num_programs(0))
    def _():
        start_all(b + 1, 1 - slot)

    def wait_body(r, c):
        ck, cv = row_copies(b, r, slot)
        ck.wait()
        cv.wait()
        return c
    lax.fori_loop(0, kmax, wait_body, 0)

    q = q_ref[0]
    s = _dot_nt(q, kbuf_ref[slot].astype(bf16))
    col = lax.broadcasted_iota(jnp.int32, s.shape, 1)
    head = lax.broadcasted_iota(jnp.int32, s.shape, 0)
    ok = (col % ATT_KV_HEADS == head // GROUP) & (col // ATT_KV_HEADS < cnt_ref[b])
    s = jnp.where(ok, s, -jnp.inf)
    s_n = jnp.sum(q.astype(f32) * kn_ref[0].astype(f32), axis=1, keepdims=True)
    s_n = jnp.where(nsel_ref[0][:, 0:1] > 0.0, s_n, -jnp.inf)
    m = jnp.maximum(jnp.max(s, axis=1, keepdims=True), s_n)
    p = jnp.exp(s - m)
    p_n = jnp.exp(s_n - m)
    l = jnp.sum(p, axis=1, keepdims=True) + p_n
    pv = _dot(p.astype(bf16), vbuf_ref[slot].astype(bf16))
    o_ref[0] = (pv + p_n.astype(bf16).astype(f32) * vn_ref[0].astype(f32)) / l


def _decode_attention(idx, page_table, cnt, q, k_new, v_new, nsel, pool_k, pool_v):
    db, kmax = idx.shape
    kern = functools.partial(_dec_attn_kernel, kmax=kmax)
    seqb = lambda shape: pl.BlockSpec(shape, lambda b, *_: (b, 0, 0))
    grid_spec = pltpu.PrefetchScalarGridSpec(
        num_scalar_prefetch=3,
        grid=(db,),
        in_specs=[
            seqb((1, ATT_HEADS, HEAD_DIM)),
            seqb((1, ATT_HEADS, HEAD_DIM)),
            seqb((1, ATT_HEADS, HEAD_DIM)),
            seqb((1, 1, LANES)),
            pl.BlockSpec(memory_space=pl.ANY),
            pl.BlockSpec(memory_space=pl.ANY),
        ],
        out_specs=seqb((1, ATT_HEADS, HEAD_DIM)),
        scratch_shapes=[pltpu.VMEM((2, kmax * ATT_KV_HEADS, HEAD_DIM), f32),
                        pltpu.VMEM((2, kmax * ATT_KV_HEADS, HEAD_DIM), f32),
                        pltpu.SemaphoreType.DMA((2, 2))],
    )
    return pl.pallas_call(
        kern,
        grid_spec=grid_spec,
        out_shape=jax.ShapeDtypeStruct((db, ATT_HEADS, HEAD_DIM), f32),
        compiler_params=_params(("arbitrary",)),
    )(idx, page_table, cnt, q, k_new, v_new, nsel, pool_k, pool_v)


def _rope_tables(pos):
    half = RET_QK_DIM // 2
    inv = ROPE_BASE ** (-jnp.arange(half, dtype=f32) / half)
    ang = pos.astype(f32)[:, None] * inv[None, :]
    cos, sin = jnp.cos(ang), jnp.sin(ang)
    return jnp.concatenate([cos, cos], axis=1), jnp.concatenate([-sin, sin], axis=1)


def _pad_w_in(w_in):
    d = w_in.shape[0]
    small = w_in[:, MAIN_W:MAIN_W + IDX_DIM + IDX_HEADS]
    pad = jnp.zeros((d, TN - IDX_DIM - IDX_HEADS), w_in.dtype)
    gates = w_in[:, MAIN_W + IDX_DIM + IDX_HEADS:]
    return jnp.concatenate([gates, w_in[:, :MAIN_W], small, pad], axis=1).astype(bf16)


def _pick(n, pref):
    for t in pref:
        if n % t == 0:
            return t
    return n


def kernel(x_prompt, x_sample, state_ret, cache_k, cache_v, cache_idx_k, page_table, ffn1_norm, ffn1_w1, ffn1_w2, mix_norm, w_in, q_norm, k_norm, ret_norm, w_ret_out, w_att_out, w_o, ffn2_norm, ffn2_w1, ffn2_w2):
    batch, seq, d = x_prompt.shape
    db, ts, _ = x_sample.shape
    depth = w_in.shape[0]
    assert depth == 1 and ts == 1 and d % (2 * TN) == 0
    n_pages = page_table.shape[1]
    past = n_pages * PAGE_SIZE
    n_phys = cache_k.shape[1]
    base = 2 * (d // TN)
    m = batch * seq
    ms = 16

    log_g = jnp.log1p(-jnp.exp2(-5.0 - jnp.arange(RET_HEADS, dtype=f32)))
    w1a, w2a = ffn1_w1[0].astype(bf16), ffn1_w2[0].astype(bf16)
    w1b, w2b = ffn2_w1[0].astype(bf16), ffn2_w2[0].astype(bf16)
    w_pad = _pad_w_in(w_in[0])
    wr, wa, wo = w_ret_out[0].astype(bf16), w_att_out[0].astype(bf16), w_o[0].astype(bf16)
    g1, g2, g3 = ffn1_norm, mix_norm, ffn2_norm
    qn, kn, gn = q_norm, k_norm, ret_norm

    tm = _pick(m, (512, 256, 128))
    tf = _pick(ffn1_w2.shape[1], (512, 256, 128))
    tmp = _pick(seq, (1024, 512, 256, 128))
    chunk = _pick(seq, (256, 128))
    tq = _pick(seq, (128,))
    tmo = _pick(m, (256, 128))

    xp = x_prompt.reshape(m, d)
    x1 = _ffn(xp, g1, w1a, w2a, tm, tf)
    cos_p, sin_p = _rope_tables(jnp.arange(seq))
    p, k32, v32, ikw = _proj(x1, g2, w_pad, cos_p, sin_p, qn, kn, tmp)
    og, st_p = _retention(p, log_g, gn, batch, seq, base, chunk)
    att = _prompt_attention(p, ikw, batch, seq, base, tq, min(TOPK_MAX, seq // 4))
    x2 = _oproj(og, att, p, x1, wr, wa, wo, tmo)
    yp = _ffn(x2, g3, w1b, w2b, tm, tf).reshape(batch, seq, d)

    xs = jnp.zeros((ms, d), f32).at[:db].set(x_sample.reshape(db, d))
    s1 = _ffn(xs, g1, w1a, w2a, ms, tf)
    cos_s, sin_s = _rope_tables(jnp.full((ms,), past, jnp.int32))
    ps, k32s, v32s, ikws = _proj(s1, g2, w_pad, cos_s, sin_s, qn, kn, ms)
    seg = lambda t0, t1: ps[:db, (base + t0) * TN:(base + t1) * TN]
    row3 = lambda a: a.astype(f32).reshape(db, 1, a.shape[-1])
    og_s, st_s = _decode_retention(row3(seg(T_RQ, T_RK)), row3(seg(T_RK, T_RV)), row3(seg(T_RV, T_RG)),
                                   row3(seg(T_RG, T_AQ)), gn, state_ret, log_g)
    qi = seg(T_IQ, T_IKW).reshape(db, IDX_HEADS, IDX_DIM)
    wi = (ikws[:db, IDX_DIM:IDX_DIM + IDX_HEADS] * IDX_W_SCALE).reshape(db, IDX_HEADS, 1)
    ik_new = ikws[:db, :IDX_DIM].reshape(db, 1, IDX_DIM)
    pool_ikt = jnp.swapaxes(cache_idx_k.reshape(n_phys, PAGE_SIZE, IDX_DIM), 1, 2)
    scores = _decode_scores(page_table, qi, wi, pool_ikt)
    idx, cnt, nsel = _decode_select(scores, qi, wi, ik_new, min(TOPK_MAX, (past + ts) // 4))
    aq_s = seg(T_AQ, T_AK).reshape(db, ATT_HEADS, HEAD_DIM)
    expand = lambda a: jnp.repeat(a[:db].reshape(db, ATT_KV_HEADS, HEAD_DIM), GROUP, axis=1).astype(bf16)
    pool_rows = lambda c: c.reshape(n_phys, PAGE_SIZE * ATT_KV_HEADS, HEAD_DIM)
    att_s = _decode_attention(idx.reshape(db, -1), page_table, cnt[:, 0, 0], aq_s, expand(k32s), expand(v32s),
                              nsel, pool_rows(cache_k), pool_rows(cache_v))
    pad_rows = lambda a: jnp.pad(a.reshape(db, -1).astype(bf16), ((0, ms - db), (0, 0)))
    s2 = _oproj(pad_rows(og_s), pad_rows(att_s), ps, s1, wr, wa, wo, ms)
    ys = _ffn(s2, g3, w1b, w2b, ms, tf)[:db].reshape(db, ts, d)

    return (yp, ys,
            st_p[None],
            k32.reshape(1, batch, seq, ATT_KV_HEADS, HEAD_DIM),
            v32.reshape(1, batch, seq, ATT_KV_HEADS, HEAD_DIM),
            ikw[:, :IDX_DIM].reshape(1, batch, seq, IDX_DIM),
            st_s[None],
            k32s[:db].reshape(1, db, ts, ATT_KV_HEADS, HEAD_DIM),
            v32s[:db].reshape(1, db, ts, ATT_KV_HEADS, HEAD_DIM),
            ikws[:db, :IDX_DIM].reshape(1, db, ts, IDX_DIM))
```

```python
import functools

import jax
import jax.numpy as jnp
from jax import lax
from jax.experimental import pallas as pl
from jax.experimental.pallas import tpu as pltpu

RET_HEADS = 8
RET_QK_DIM = 128
RET_V_DIM = 256
ATT_HEADS = 16
ATT_KV_HEADS = 4
HEAD_DIM = 128
GROUP = ATT_HEADS // ATT_KV_HEADS
IDX_HEADS = 16
IDX_DIM = 64
IDX_W_SCALE = (IDX_HEADS ** -0.5) * (IDX_DIM ** -0.5)
TOPK_MAX = 256
PAGE_SIZE = 128
ROPE_BASE = 10000.0
EPS = 1e-6

RET_QK_W = RET_HEADS * RET_QK_DIM
RET_V_W = RET_HEADS * RET_V_DIM
ATT_Q_W = ATT_HEADS * HEAD_DIM
ATT_KV_W = ATT_KV_HEADS * HEAD_DIM
IDX_Q_W = IDX_HEADS * IDX_DIM
MAIN_W = 2 * RET_QK_W + 2 * RET_V_W + ATT_Q_W + 2 * ATT_KV_W + IDX_Q_W

LANES = 128
TN = 512
T_RQ, T_RK, T_RV, T_RG, T_AQ, T_AK, T_AV, T_IQ, T_IKW, T_END = 0, 2, 4, 8, 12, 16, 17, 18, 20, 21
VMEM_LIMIT = 56 * 1024 * 1024
INT_MIN = -2 ** 31
NEG_BIG = -1e30

bf16 = jnp.bfloat16
f32 = jnp.float32


def _params(sem, vmem=VMEM_LIMIT):
    return pltpu.CompilerParams(dimension_semantics=sem, vmem_limit_bytes=vmem)


def _sigmoid(x):
    return 1.0 / (1.0 + jnp.exp(-x))


def _dot(a, b):
    return jnp.dot(a, b, preferred_element_type=f32)


def _dot_nt(a, b):
    return lax.dot_general(a, b, (((1,), (1,)), ((), ())), preferred_element_type=f32)


def _rms(x, gain):
    ms = jnp.mean(x * x, axis=-1, keepdims=True)
    return x * lax.rsqrt(ms + EPS) * gain


def _ffn_kernel(x_ref, g_ref, w1g_ref, w1u_ref, w2_ref, o_ref, h_ref, acc_ref):
    j = pl.program_id(1)

    @pl.when(j == 0)
    def _():
        h_ref[...] = _rms(x_ref[...], g_ref[...]).astype(bf16)
        acc_ref[...] = jnp.zeros_like(acc_ref)

    h = h_ref[...]
    gate = _dot(h, w1g_ref[...])
    up = _dot(h, w1u_ref[...])
    act = (gate * _sigmoid(gate) * up).astype(bf16)
    acc_ref[...] += _dot(act, w2_ref[...])

    @pl.when(j == pl.num_programs(1) - 1)
    def _():
        o_ref[...] = x_ref[...] + 0.5 * acc_ref[...]


def _ffn(x, gain, w1, w2, tm, tf):
    m, d = x.shape
    dff = w2.shape[0]
    nf = dff // tf
    return pl.pallas_call(
        _ffn_kernel,
        grid=(m // tm, nf),
        in_specs=[
            pl.BlockSpec((tm, d), lambda i, j: (i, 0)),
            pl.BlockSpec((1, d), lambda i, j: (0, 0)),
            pl.BlockSpec((d, tf), lambda i, j: (0, j)),
            pl.BlockSpec((d, tf), lambda i, j: (0, j + nf)),
            pl.BlockSpec((tf, d), lambda i, j: (j, 0)),
        ],
        out_specs=pl.BlockSpec((tm, d), lambda i, j: (i, 0)),
        out_shape=jax.ShapeDtypeStruct((m, d), f32),
        scratch_shapes=[pltpu.VMEM((tm, d), bf16), pltpu.VMEM((tm, d), f32)],
        compiler_params=_params(("arbitrary", "arbitrary")),
    )(x, gain, w1, w1, w2)


def _proj_kernel(x_ref, g_ref, w_ref, cos_ref, sin_ref, qn_ref, kn_ref,
                 p_ref, k32_ref, v32_ref, ikw_ref, h_ref, *, base):
    j = pl.program_id(1)

    @pl.when(j == 0)
    def _():
        h_ref[...] = _rms(x_ref[...], g_ref[...]).astype(bf16)

    res = _dot_nt(h_ref[...], w_ref[...])
    heads = [slice(c * LANES, (c + 1) * LANES) for c in range(TN // LANES)]

    @pl.when(j < base)
    def _():
        p_ref[...] = _sigmoid(res).astype(bf16)

    @pl.when((j >= base + T_RQ) & (j < base + T_RV))
    def _():
        cos, sin = cos_ref[...], sin_ref[...]
        scale = jnp.where(j >= base + T_RK, RET_QK_DIM ** -0.5, 1.0).astype(f32)
        for sl in heads:
            x = res[:, sl]
            p_ref[:, sl] = ((x * cos + pltpu.roll(x, LANES // 2, 1) * sin) * scale).astype(bf16)

    @pl.when(((j >= base + T_RV) & (j < base + T_RG)) | ((j >= base + T_IQ) & (j < base + T_IKW)))
    def _():
        p_ref[...] = res.astype(bf16)

    @pl.when((j >= base + T_RG) & (j < base + T_AQ))
    def _():
        p_ref[...] = (res * _sigmoid(res)).astype(bf16)

    @pl.when((j >= base + T_AQ) & (j < base + T_AK))
    def _():
        for sl in heads:
            p_ref[:, sl] = (_rms(res[:, sl], qn_ref[...]) * HEAD_DIM ** -0.5).astype(bf16)

    @pl.when(j == base + T_AK)
    def _():
        for sl in heads:
            y = _rms(res[:, sl], kn_ref[...])
            k32_ref[:, sl] = y
            p_ref[:, sl] = y.astype(bf16)

    @pl.when(j == base + T_AV)
    def _():
        v32_ref[...] = res
        p_ref[...] = res.astype(bf16)

    @pl.when(j == base + T_IKW)
    def _():
        ikw_ref[...] = res[:, :LANES]
        p_ref[...] = res.astype(bf16)


def _proj(x, gain, w_pad, cos2, sin2, qn, kn, tm):
    m, d = x.shape
    nt = w_pad.shape[0] // TN
    base = nt - T_END
    tab_blocks = cos2.shape[0] // tm
    kern = functools.partial(_proj_kernel, base=base)
    return pl.pallas_call(
        kern,
        grid=(m // tm, nt),
        in_specs=[
            pl.BlockSpec((tm, d), lambda i, j: (i, 0)),
            pl.BlockSpec((1, d), lambda i, j: (0, 0)),
            pl.BlockSpec((TN, d), lambda i, j: (j, 0)),
            pl.BlockSpec((tm, LANES), lambda i, j: (i % tab_blocks, 0)),
            pl.BlockSpec((tm, LANES), lambda i, j: (i % tab_blocks, 0)),
            pl.BlockSpec((1, LANES), lambda i, j: (0, 0)),
            pl.BlockSpec((1, LANES), lambda i, j: (0, 0)),
        ],
        out_specs=[
            pl.BlockSpec((tm, TN), lambda i, j: (i, j)),
            pl.BlockSpec((tm, TN), lambda i, j: (i, 0)),
            pl.BlockSpec((tm, TN), lambda i, j: (i, 0)),
            pl.BlockSpec((tm, LANES), lambda i, j: (i, 0)),
        ],
        out_shape=[
            jax.ShapeDtypeStruct((m, nt * TN), bf16),
            jax.ShapeDtypeStruct((m, ATT_KV_W), f32),
            jax.ShapeDtypeStruct((m, ATT_KV_W), f32),
            jax.ShapeDtypeStruct((m, LANES), f32),
        ],
        scratch_shapes=[pltpu.VMEM((tm, d), bf16)],
        compiler_params=_params(("arbitrary", "arbitrary")),
    )(x, gain, w_pad, cos2, sin2, qn, kn)


def _ret_kernel(lg_ref, q_ref, k_ref, v_ref, sg_ref, gn_ref, og_ref, st_ref, state_ref, *, chunk):
    h = pl.program_id(1)
    c = pl.program_id(2)
    lg = lg_ref[h]

    @pl.when(c == 0)
    def _():
        state_ref[...] = jnp.zeros_like(state_ref)

    q, k, v = q_ref[...], k_ref[...], v_ref[...]
    ii = lax.broadcasted_iota(jnp.int32, (chunk, chunk), 0)
    jj = lax.broadcasted_iota(jnp.int32, (chunk, chunk), 1)
    rel = (ii - jj).astype(f32)
    dmat = jnp.where(rel >= 0, jnp.exp(lg * jnp.maximum(rel, 0.0)), 0.0)
    scores = _dot_nt(q, k) * dmat
    inner = _dot(scores.astype(bf16), v)
    i1 = lax.broadcasted_iota(jnp.int32, (chunk, 1), 0).astype(f32)
    q_dec = jnp.exp(lg * (i1 + 1.0))
    k_dec = jnp.exp(lg * (chunk - 1.0 - i1))
    state = state_ref[...]
    cross = _dot((q.astype(f32) * q_dec).astype(bf16), state.astype(bf16))
    kd = (k.astype(f32) * k_dec).astype(bf16)
    new_state = jnp.exp(lg * chunk) * state + lax.dot_general(
        kd, v, (((0,), (0,)), ((), ())), preferred_element_type=f32)
    state_ref[...] = new_state
    o = _rms(inner + cross, gn_ref[...])
    og_ref[...] = (o * sg_ref[...].astype(f32)).astype(bf16)

    @pl.when(c == pl.num_programs(2) - 1)
    def _():
        st_ref[0, 0] = new_state


def _retention(p, log_g, gn, batch, seq, base, chunk):
    nc = seq // chunk
    qb = (base + T_RQ) * (TN // RET_QK_DIM)
    kb = (base + T_RK) * (TN // RET_QK_DIM)
    vb = (base + T_RV) * (TN // RET_V_DIM)
    gb = (base + T_RG) * (TN // RET_V_DIM)
    kern = functools.partial(_ret_kernel, chunk=chunk)
    grid_spec = pltpu.PrefetchScalarGridSpec(
        num_scalar_prefetch=1,
        grid=(batch, RET_HEADS, nc),
        in_specs=[
            pl.BlockSpec((chunk, RET_QK_DIM), lambda b, h, c, lg: (b * nc + c, qb + h)),
            pl.BlockSpec((chunk, RET_QK_DIM), lambda b, h, c, lg: (b * nc + c, kb + h)),
            pl.BlockSpec((chunk, RET_V_DIM), lambda b, h, c, lg: (b * nc + c, vb + h)),
            pl.BlockSpec((chunk, RET_V_DIM), lambda b, h, c, lg: (b * nc + c, gb + h)),
            pl.BlockSpec((1, RET_V_DIM), lambda b, h, c, lg: (0, 0)),
        ],
        out_specs=[
            pl.BlockSpec((chunk, RET_V_DIM), lambda b, h, c, lg: (b * nc + c, h)),
            pl.BlockSpec((1, 1, RET_QK_DIM, RET_V_DIM), lambda b, h, c, lg: (b, h, 0, 0)),
        ],
        scratch_shapes=[pltpu.VMEM((RET_QK_DIM, RET_V_DIM), f32)],
    )
    return pl.pallas_call(
        kern,
        grid_spec=grid_spec,
        out_shape=[
            jax.ShapeDtypeStruct((batch * seq, RET_V_W), bf16),
            jax.ShapeDtypeStruct((batch, RET_HEADS, RET_QK_DIM, RET_V_DIM), f32),
        ],
        compiler_params=_params(("arbitrary", "arbitrary", "arbitrary")),
    )(log_g, p, p, p, p, gn)


def _sortable_key(score):
    bits = lax.bitcast_convert_type(score, jnp.int32)
    return jnp.where(bits >= 0, bits, bits ^ jnp.int32(0x7FFFFFFF))


def _kth_largest_key(count_ge, n_keep, shape):
    def body(it, t_u):
        bit = jnp.left_shift(jnp.int32(1), 31 - it)
        cand_u = t_u | bit
        cnt = count_ge(cand_u ^ jnp.int32(INT_MIN))
        return jnp.where(cnt >= n_keep, cand_u, t_u)
    t_u = lax.fori_loop(0, 32, body, jnp.zeros(shape, jnp.int32))
    return t_u ^ jnp.int32(INT_MIN)


def _attn_block(aq_ref, k_ref, v_ref, iq_ref, ikw_q_ref, o_ref, key_ref, iklo_ref, ikhi_ref, qb,
                *, tq, s_eff, n_keep):
    iwq = ikw_q_ref[...] * IDX_W_SCALE
    pairs = IDX_HEADS // 2
    stack = 4
    scores = jnp.zeros((tq, s_eff), f32)
    for h0 in range(0, pairs, stack):
        qs = jnp.concatenate([iq_ref[:, hp * LANES:(hp + 1) * LANES] for hp in range(h0, h0 + stack)], axis=0)
        for sub, ik_ref in enumerate((iklo_ref, ikhi_ref)):
            s = jnp.maximum(_dot_nt(qs, ik_ref[:s_eff, :]), 0.0)
            for i in range(stack):
                col = IDX_DIM + 2 * (h0 + i) + sub
                scores = scores + s[i * tq:(i + 1) * tq] * iwq[:, col:col + 1]

    q_pos = qb * tq + lax.broadcasted_iota(jnp.int32, (tq, s_eff), 0)
    k_pos = lax.broadcasted_iota(jnp.int32, (tq, s_eff), 1)
    visible = k_pos <= q_pos
    key_ref[:, :s_eff] = _sortable_key(jnp.where(visible, scores, -jnp.inf))

    def count_ge(t):
        return jnp.sum(jnp.where(key_ref[:, :s_eff] >= t, 1.0, 0.0), axis=1, keepdims=True)

    thr = _kth_largest_key(count_ge, n_keep, (tq, 1))
    bias = jnp.where((key_ref[:, :s_eff] >= thr) & visible, 0.0, -jnp.inf)

    for g in range(ATT_KV_HEADS):
        kg = k_ref[:s_eff, g * HEAD_DIM:(g + 1) * HEAD_DIM]
        vg = v_ref[:s_eff, g * HEAD_DIM:(g + 1) * HEAD_DIM]
        heads = range(g * GROUP, (g + 1) * GROUP)
        qg = jnp.concatenate([aq_ref[:, hd * HEAD_DIM:(hd + 1) * HEAD_DIM] for hd in heads], axis=0)
        s = _dot_nt(qg, kg).reshape(GROUP, tq, s_eff) + bias[None]
        m = jnp.max(s, axis=2, keepdims=True)
        p = jnp.exp(s - m)
        l = jnp.sum(p, axis=2, keepdims=True)
        o = _dot(p.astype(bf16).reshape(GROUP * tq, s_eff), vg).reshape(GROUP, tq, HEAD_DIM) / l
        for i, hd in enumerate(heads):
            o_ref[:, hd * HEAD_DIM:(hd + 1) * HEAD_DIM] = o[i].astype(bf16)


def _attn_kernel(aq_ref, k_ref, v_ref, iq_ref, ikw_k_ref, ikw_q_ref, o_ref, key_ref, iklo_ref, ikhi_ref,
                 *, tq, seq, n_keep, n_classes):
    qb = pl.program_id(1)

    @pl.when(qb == 0)
    def _():
        ikw = ikw_k_ref[...]
        lane = lax.broadcasted_iota(jnp.int32, (seq, LANES), 1)
        iklo_ref[...] = jnp.where(lane < IDX_DIM, ikw, 0.0).astype(bf16)
        ikhi_ref[...] = jnp.where(lane >= IDX_DIM, pltpu.roll(ikw, IDX_DIM, 1), 0.0).astype(bf16)

    per = (seq // tq) // n_classes
    for c in range(n_classes):
        @pl.when(qb // per == c)
        def _(c=c):
            _attn_block(aq_ref, k_ref, v_ref, iq_ref, ikw_q_ref, o_ref, key_ref, iklo_ref, ikhi_ref, qb,
                        tq=tq, s_eff=(c + 1) * per * tq, n_keep=n_keep)


def _prompt_attention(p, ikw, batch, seq, base, tq, n_keep):
    nq = seq // tq
    kern = functools.partial(_attn_kernel, tq=tq, seq=seq, n_keep=n_keep, n_classes=_pick(nq, (8, 4, 2)))
    aq_blk = (base + T_AQ) * TN // ATT_Q_W
    iq_blk = (base + T_IQ) * TN // IDX_Q_W
    return pl.pallas_call(
        kern,
        grid=(batch, nq),
        in_specs=[
            pl.BlockSpec((tq, ATT_Q_W), lambda b, t: (b * nq + t, aq_blk)),
            pl.BlockSpec((seq, ATT_KV_W), lambda b, t: (b, base + T_AK)),
            pl.BlockSpec((seq, ATT_KV_W), lambda b, t: (b, base + T_AV)),
            pl.BlockSpec((tq, IDX_Q_W), lambda b, t: (b * nq + t, iq_blk)),
            pl.BlockSpec((seq, LANES), lambda b, t: (b, 0)),
            pl.BlockSpec((tq, LANES), lambda b, t: (b * nq + t, 0)),
        ],
        out_specs=pl.BlockSpec((tq, ATT_Q_W), lambda b, t: (b * nq + t, 0)),
        out_shape=jax.ShapeDtypeStruct((batch * seq, ATT_Q_W), bf16),
        scratch_shapes=[pltpu.VMEM((tq, seq), jnp.int32), pltpu.VMEM((seq, LANES), bf16),
                        pltpu.VMEM((seq, LANES), bf16)],
        compiler_params=_params(("arbitrary", "arbitrary")),
    )(p, p, p, p, ikw, ikw)


def _oproj_kernel(og_ref, att_ref, gr_ref, ga_ref, x_ref, wr_ref, wa_ref, wo_ref, o_ref):
    ret_out = _dot(og_ref[...], wr_ref[...])
    att_out = _dot(att_ref[...], wa_ref[...])
    merged = gr_ref[...].astype(f32) * ret_out + ga_ref[...].astype(f32) * att_out
    o_ref[...] = x_ref[...] + _dot(merged.astype(bf16), wo_ref[...])


def _oproj(og, att, p, x, wr, wa, wo, tm):
    m, d = x.shape
    const = lambda shape: pl.BlockSpec(shape, lambda i: (0, 0), pipeline_mode=pl.Buffered(1))
    return pl.pallas_call(
        _oproj_kernel,
        grid=(m // tm,),
        in_specs=[
            pl.BlockSpec((tm, RET_V_W), lambda i: (i, 0)),
            pl.BlockSpec((tm, ATT_Q_W), lambda i: (i, 0)),
            pl.BlockSpec((tm, d), lambda i: (i, 0)),
            pl.BlockSpec((tm, d), lambda i: (i, 1)),
            pl.BlockSpec((tm, d), lambda i: (i, 0)),
            const(wr.shape), const(wa.shape), const(wo.shape),
        ],
        out_specs=pl.BlockSpec((tm, d), lambda i: (i, 0)),
        out_shape=jax.ShapeDtypeStruct((m, d), f32),
        compiler_params=_params(("arbitrary",)),
    )(og, att, p, p, x, wr, wa, wo)


def _dec_ret_kernel(lg_ref, q_ref, k_ref, v_ref, sg_ref, gn_ref, st_ref, og_ref, nst_ref):
    for h in range(RET_HEADS):
        lg = lg_ref[h]
        gamma = jnp.exp(lg)
        q = q_ref[0, :, h * RET_QK_DIM:(h + 1) * RET_QK_DIM]
        k = k_ref[0, :, h * RET_QK_DIM:(h + 1) * RET_QK_DIM]
        v = v_ref[0, :, h * RET_V_DIM:(h + 1) * RET_V_DIM]
        sg = sg_ref[0, :, h * RET_V_DIM:(h + 1) * RET_V_DIM]
        state = st_ref[0, 0, h]
        qk = jnp.sum(q * k, axis=1, keepdims=True)
        q8 = jnp.broadcast_to((q * gamma).astype(bf16), (8, RET_QK_DIM))
        cross = _dot(q8, state.astype(bf16))[0:1]
        k_col = jnp.transpose(jnp.broadcast_to(k, (8, RET_QK_DIM)))[:, 0:1]
        nst_ref[0, h] = gamma * state + k_col * v
        o = _rms(qk * v + cross, gn_ref[...])
        og_ref[0, :, h * RET_V_DIM:(h + 1) * RET_V_DIM] = o * sg


def _decode_retention(q, k, v, sg, gn, state, log_g):
    db = q.shape[0]
    row = lambda w: pl.BlockSpec((1, 1, w), lambda b, lg: (b, 0, 0))
    grid_spec = pltpu.PrefetchScalarGridSpec(
        num_scalar_prefetch=1,
        grid=(db,),
        in_specs=[row(RET_QK_W), row(RET_QK_W), row(RET_V_W), row(RET_V_W),
                  pl.BlockSpec((1, RET_V_DIM), lambda b, lg: (0, 0)),
                  pl.BlockSpec((1, 1, RET_HEADS, RET_QK_DIM, RET_V_DIM), lambda b, lg: (0, b, 0, 0, 0))],
        out_specs=[row(RET_V_W),
                   pl.BlockSpec((1, RET_HEADS, RET_QK_DIM, RET_V_DIM), lambda b, lg: (b, 0, 0, 0))],
    )
    return pl.pallas_call(
        _dec_ret_kernel,
        grid_spec=grid_spec,
        out_shape=[jax.ShapeDtypeStruct((db, 1, RET_V_W), f32),
                   jax.ShapeDtypeStruct((db, RET_HEADS, RET_QK_DIM, RET_V_DIM), f32)],
        compiler_params=_params(("arbitrary",)),
    )(log_g, q, k, v, sg, gn, state)


def _dec_score_kernel(pt_ref, qi_ref, w_ref, pool_ref, s_ref, buf_ref, sem_ref, *, n_pages):
    b = pl.program_id(0)
    slot = b % 2

    def page_copy(seq, p, slot_):
        dst = buf_ref.at[slot_, :, pl.ds(pl.multiple_of(p * PAGE_SIZE, PAGE_SIZE), PAGE_SIZE)]
        return pltpu.make_async_copy(pool_ref.at[pt_ref[seq, p]], dst, sem_ref.at[slot_])

    def start_all(seq, slot_):
        def body(p, c):
            page_copy(seq, p, slot_).start()
            return c
        lax.fori_loop(0, n_pages, body, 0)

    @pl.when(b == 0)
    def _():
        start_all(0, 0)

    @pl.when(b + 1 < pl.num_programs(0))
    def _():
        start_all(b + 1, 1 - slot)

    def wait_body(p, c):
        page_copy(b, p, slot).wait()
        return c
    lax.fori_loop(0, n_pages, wait_body, 0)

    s = jnp.maximum(_dot(qi_ref[0], buf_ref[slot].astype(bf16)), 0.0)
    row = jnp.sum(s * w_ref[0], axis=0, keepdims=True)
    for p in range(n_pages):
        s_ref[0, p:p + 1, :] = row[:, p * PAGE_SIZE:(p + 1) * PAGE_SIZE]


def _decode_scores(page_table, qi, w, pool_ikt):
    db, n_pages = page_table.shape
    kern = functools.partial(_dec_score_kernel, n_pages=n_pages)
    grid_spec = pltpu.PrefetchScalarGridSpec(
        num_scalar_prefetch=1,
        grid=(db,),
        in_specs=[
            pl.BlockSpec((1, IDX_HEADS, IDX_DIM), lambda b, pt: (b, 0, 0)),
            pl.BlockSpec((1, IDX_HEADS, 1), lambda b, pt: (b, 0, 0)),
            pl.BlockSpec(memory_space=pl.ANY),
        ],
        out_specs=pl.BlockSpec((1, n_pages, PAGE_SIZE), lambda b, pt: (b, 0, 0)),
        scratch_shapes=[pltpu.VMEM((2, IDX_DIM, n_pages * PAGE_SIZE), f32), pltpu.SemaphoreType.DMA((2,))],
    )
    return pl.pallas_call(
        kern,
        grid_spec=grid_spec,
        out_shape=jax.ShapeDtypeStruct((db, n_pages, PAGE_SIZE), f32),
        compiler_params=_params(("arbitrary",)),
    )(page_table, qi, w, pool_ikt)


def _dec_select_kernel(s_ref, qi_ref, w_ref, ikn_ref, idx_ref, cnt_ref, nsel_ref, rank_ref,
                       *, n_keep, kmax):
    db, n_pages, _ = s_ref.shape
    ikn = ikn_ref[...].astype(bf16).astype(f32)
    dots = jnp.sum(qi_ref[...].astype(f32) * ikn, axis=2, keepdims=True)
    s_new = jnp.sum(jnp.maximum(dots, 0.0) * w_ref[...], axis=1, keepdims=True)
    keys = _sortable_key(s_ref[...])
    key_new = _sortable_key(s_new)

    def count_ge(t):
        c = jnp.sum(jnp.where(keys >= t, 1.0, 0.0), axis=1, keepdims=True)
        return jnp.sum(c, axis=2, keepdims=True) + jnp.where(key_new >= t, 1.0, 0.0)

    thr = _kth_largest_key(count_ge, n_keep, (db, 1, 1))
    sel = jnp.where(keys >= thr, 1.0, 0.0)
    nsel_ref[...] = jnp.broadcast_to(jnp.where(key_new >= thr, 1.0, 0.0), (db, 1, LANES))

    pp = lax.broadcasted_iota(jnp.int32, (n_pages, n_pages), 0)
    pq = lax.broadcasted_iota(jnp.int32, (n_pages, n_pages), 1)
    earlier_page = jnp.where(pq < pp, 1.0, 0.0).astype(bf16)
    oo = lax.broadcasted_iota(jnp.int32, (PAGE_SIZE, PAGE_SIZE), 0)
    oq = lax.broadcasted_iota(jnp.int32, (PAGE_SIZE, PAGE_SIZE), 1)
    earlier_off = jnp.where(oo < oq, 1.0, 0.0).astype(bf16)
    r_iota = lax.broadcasted_iota(jnp.int32, (kmax, PAGE_SIZE), 0).astype(f32)
    lane = lax.broadcasted_iota(jnp.int32, (kmax, PAGE_SIZE), 1).astype(f32)
    for b in range(db):
        sb = sel[b]
        per_page = jnp.sum(sb, axis=1, keepdims=True)
        before = _dot(earlier_page, jnp.broadcast_to(per_page, (n_pages, PAGE_SIZE)).astype(bf16))
        within = _dot(sb.astype(bf16), earlier_off)
        rank_ref[b] = jnp.where(sb > 0.0, before + within, -1.0)

        def body(p, acc, b=b):
            row = rank_ref[b, pl.ds(p, 1), :]
            return jnp.where(row == r_iota, lane + jnp.asarray(p * PAGE_SIZE, f32), acc)

        acc = lax.fori_loop(0, n_pages, body, jnp.zeros((kmax, PAGE_SIZE), f32))
        idx_ref[b] = jnp.sum(acc, axis=1, keepdims=True).astype(jnp.int32)
        total = jnp.sum(per_page, axis=0, keepdims=True)
        cnt_ref[b] = jnp.broadcast_to(total, (1, LANES)).astype(jnp.int32)


def _decode_select(scores, qi, w, ik_new, n_keep):
    db, n_pages, _ = scores.shape
    kmax = -(-n_keep // 8) * 8
    kern = functools.partial(_dec_select_kernel, n_keep=n_keep, kmax=kmax)
    full = lambda shape: pl.BlockSpec(shape, lambda i: (0,) * len(shape))
    return pl.pallas_call(
        kern,
        grid=(1,),
        in_specs=[full(scores.shape), full(qi.shape), full(w.shape), full(ik_new.shape)],
        out_specs=[full((db, kmax, 1)), full((db, 1, LANES)), full((db, 1, LANES))],
        out_shape=[jax.ShapeDtypeStruct((db, kmax, 1), jnp.int32),
                   jax.ShapeDtypeStruct((db, 1, LANES), jnp.int32),
                   jax.ShapeDtypeStruct((db, 1, LANES), f32)],
        scratch_shapes=[pltpu.VMEM((db, n_pages, PAGE_SIZE), f32)],
        compiler_params=_params(("arbitrary",)),
    )(scores, qi, w, ik_new)


def _dec_attn_kernel(idx_ref, pt_ref, cnt_ref, q_ref, kn_ref, vn_ref, nsel_ref, kpool_ref, vpool_ref, o_ref,
                     kbuf_ref, vbuf_ref, sem_ref, *, kmax):
    b = pl.program_id(0)
    slot = b % 2
    page_shift = PAGE_SIZE.bit_length() - 1

    def row_copies(seq, r, slot_):
        pos = idx_ref[seq, r]
        phys = pt_ref[seq, lax.shift_right_logical(pos, page_shift)]
        src0 = pl.multiple_of((pos & (PAGE_SIZE - 1)) * ATT_KV_HEADS, ATT_KV_HEADS)
        dst0 = pl.multiple_of(r * ATT_KV_HEADS, ATT_KV_HEADS)
        return (pltpu.make_async_copy(kpool_ref.at[phys, pl.ds(src0, ATT_KV_HEADS), :],
                                      kbuf_ref.at[slot_, pl.ds(dst0, ATT_KV_HEADS), :], sem_ref.at[0, slot_]),
                pltpu.make_async_copy(vpool_ref.at[phys, pl.ds(src0, ATT_KV_HEADS), :],
                                      vbuf_ref.at[slot_, pl.ds(dst0, ATT_KV_HEADS), :], sem_ref.at[1, slot_]))

    def start_all(seq, slot_):
        def body(r, c):
            ck, cv = row_copies(seq, r, slot_)
            ck.start()
            cv.start()
            return c
        lax.fori_loop(0, kmax, body, 0)

    @pl.when(b == 0)
    def _():
        start_all(0, 0)

    @pl.when(b + 1 < pl.num_programs(0))
    def _():
        start_all(b + 1, 1 - slot)

    def wait_body(r, c):
        ck, cv = row_copies(b, r, slot)
        ck.wait()
        cv.wait()
        return c
    lax.fori_loop(0, kmax, wait_body, 0)

    q = q_ref[0]
    s = _dot_nt(q, kbuf_ref[slot].astype(bf16))
    col = lax.broadcasted_iota(jnp.int32, s.shape, 1)
    head = lax.broadcasted_iota(jnp.int32, s.shape, 0)
    ok = (col % ATT_KV_HEADS == head // GROUP) & (col // ATT_KV_HEADS < cnt_ref[b])
    s = jnp.where(ok, s, -jnp.inf)
    s_n = jnp.sum(q.astype(f32) * kn_ref[0].astype(f32), axis=1, keepdims=True)
    s_n = jnp.where(nsel_ref[0][:, 0:1] > 0.0, s_n, -jnp.inf)
    m = jnp.maximum(jnp.max(s, axis=1, keepdims=True), s_n)
    p = jnp.exp(s - m)
    p_n = jnp.exp(s_n - m)
    l = jnp.sum(p, axis=1, keepdims=True) + p_n
    pv = _dot(p.astype(bf16), vbuf_ref[slot].astype(bf16))
    o_ref[0] = (pv + p_n.astype(bf16).astype(f32) * vn_ref[0].astype(f32)) / l


def _decode_attention(idx, page_table, cnt, q, k_new, v_new, nsel, pool_k, pool_v):
    db, kmax = idx.shape
    kern = functools.partial(_dec_attn_kernel, kmax=kmax)
    seqb = lambda shape: pl.BlockSpec(shape, lambda b, *_: (b, 0, 0))
    grid_spec = pltpu.PrefetchScalarGridSpec(
        num_scalar_prefetch=3,
        grid=(db,),
        in_specs=[
            seqb((1, ATT_HEADS, HEAD_DIM)),
            seqb((1, ATT_HEADS, HEAD_DIM)),
            seqb((1, ATT_HEADS, HEAD_DIM)),
            seqb((1, 1, LANES)),
            pl.BlockSpec(memory_space=pl.ANY),
            pl.BlockSpec(memory_space=pl.ANY),
        ],
        out_specs=seqb((1, ATT_HEADS, HEAD_DIM)),
        scratch_shapes=[pltpu.VMEM((2, kmax * ATT_KV_HEADS, HEAD_DIM), f32),
                        pltpu.VMEM((2, kmax * ATT_KV_HEADS, HEAD_DIM), f32),
                        pltpu.SemaphoreType.DMA((2, 2))],
    )
    return pl.pallas_call(
        kern,
        grid_spec=grid_spec,
        out_shape=jax.ShapeDtypeStruct((db, ATT_HEADS, HEAD_DIM), f32),
        compiler_params=_params(("arbitrary",)),
    )(idx, page_table, cnt, q, k_new, v_new, nsel, pool_k, pool_v)


def _rope_tables(pos):
    half = RET_QK_DIM // 2
    inv = ROPE_BASE ** (-jnp.arange(half, dtype=f32) / half)
    ang = pos.astype(f32)[:, None] * inv[None, :]
    cos, sin = jnp.cos(ang), jnp.sin(ang)
    return jnp.concatenate([cos, cos], axis=1), jnp.concatenate([-sin, sin], axis=1)


def _pad_w_in(w_in):
    d = w_in.shape[0]
    w_t = jnp.swapaxes(w_in, 0, 1)
    small = w_t[MAIN_W:MAIN_W + IDX_DIM + IDX_HEADS]
    pad = jnp.zeros((TN - IDX_DIM - IDX_HEADS, d), w_in.dtype)
    gates = w_t[MAIN_W + IDX_DIM + IDX_HEADS:]
    return jnp.concatenate([gates, w_t[:MAIN_W], small, pad], axis=0).astype(bf16)


def _pick(n, pref):
    for t in pref:
        if n % t == 0:
            return t
    return n


def kernel(x_prompt, x_sample, state_ret, cache_k, cache_v, cache_idx_k, page_table, ffn1_norm, ffn1_w1, ffn1_w2, mix_norm, w_in, q_norm, k_norm, ret_norm, w_ret_out, w_att_out, w_o, ffn2_norm, ffn2_w1, ffn2_w2):
    batch, seq, d = x_prompt.shape
    db, ts, _ = x_sample.shape
    depth = w_in.shape[0]
    assert depth == 1 and ts == 1 and d % (2 * TN) == 0
    n_pages = page_table.shape[1]
    past = n_pages * PAGE_SIZE
    n_phys = cache_k.shape[1]
    base = 2 * (d // TN)
    m = batch * seq
    ms = 16

    log_g = jnp.log1p(-jnp.exp2(-5.0 - jnp.arange(RET_HEADS, dtype=f32)))
    w1a, w2a = ffn1_w1[0].astype(bf16), ffn1_w2[0].astype(bf16)
    w1b, w2b = ffn2_w1[0].astype(bf16), ffn2_w2[0].astype(bf16)
    w_pad = _pad_w_in(w_in[0])
    wr, wa, wo = w_ret_out[0].astype(bf16), w_att_out[0].astype(bf16), w_o[0].astype(bf16)
    g1, g2, g3 = ffn1_norm, mix_norm, ffn2_norm
    qn, kn, gn = q_norm, k_norm, ret_norm

    tm = _pick(m, (512, 256, 128))
    tf = _pick(ffn1_w2.shape[1], (512, 256, 128))
    tmp = _pick(seq, (1024, 512, 256, 128))
    chunk = _pick(seq, (256, 128))
    tq = _pick(seq, (128,))
    tmo = _pick(m, (256, 128))

    xp = x_prompt.reshape(m, d)
    x1 = _ffn(xp, g1, w1a, w2a, tm, tf)
    cos_p, sin_p = _rope_tables(jnp.arange(seq))
    p, k32, v32, ikw = _proj(x1, g2, w_pad, cos_p, sin_p, qn, kn, tmp)
    og, st_p = _retention(p, log_g, gn, batch, seq, base, chunk)
    att = _prompt_attention(p, ikw, batch, seq, base, tq, min(TOPK_MAX, seq // 4))
    x2 = _oproj(og, att, p, x1, wr, wa, wo, tmo)
    yp = _ffn(x2, g3, w1b, w2b, tm, tf).reshape(batch, seq, d)

    xs = jnp.zeros((ms, d), f32).at[:db].set(x_sample.reshape(db, d))
    s1 = _ffn(xs, g1, w1a, w2a, ms, tf)
    cos_s, sin_s = _rope_tables(jnp.full((ms,), past, jnp.int32))
    ps, k32s, v32s, ikws = _proj(s1, g2, w_pad, cos_s, sin_s, qn, kn, ms)
    seg = lambda t0, t1: ps[:db, (base + t0) * TN:(base + t1) * TN]
    row3 = lambda a: a.astype(f32).reshape(db, 1, a.shape[-1])
    og_s, st_s = _decode_retention(row3(seg(T_RQ, T_RK)), row3(seg(T_RK, T_RV)), row3(seg(T_RV, T_RG)),
                                   row3(seg(T_RG, T_AQ)), gn, state_ret, log_g)
    qi = seg(T_IQ, T_IKW).reshape(db, IDX_HEADS, IDX_DIM)
    wi = (ikws[:db, IDX_DIM:IDX_DIM + IDX_HEADS] * IDX_W_SCALE).reshape(db, IDX_HEADS, 1)
    ik_new = ikws[:db, :IDX_DIM].reshape(db, 1, IDX_DIM)
    pool_ikt = jnp.swapaxes(cache_idx_k.reshape(n_phys, PAGE_SIZE, IDX_DIM), 1, 2)
    scores = _decode_scores(page_table, qi, wi, pool_ikt)
    idx, cnt, nsel = _decode_select(scores, qi, wi, ik_new, min(TOPK_MAX, (past + ts) // 4))
    aq_s = seg(T_AQ, T_AK).reshape(db, ATT_HEADS, HEAD_DIM)
    expand = lambda a: jnp.repeat(a[:db].reshape(db, ATT_KV_HEADS, HEAD_DIM), GROUP, axis=1).astype(bf16)
    pool_rows = lambda c: c.reshape(n_phys, PAGE_SIZE * ATT_KV_HEADS, HEAD_DIM)
    att_s = _decode_attention(idx.reshape(db, -1), page_table, cnt[:, 0, 0], aq_s, expand(k32s), expand(v32s),
                              nsel, pool_rows(cache_k), pool_rows(cache_v))
    pad_rows = lambda a: jnp.pad(a.reshape(db, -1).astype(bf16), ((0, ms - db), (0, 0)))
    s2 = _oproj(pad_rows(og_s), pad_rows(att_s), ps, s1, wr, wa, wo, ms)
    ys = _ffn(s2, g3, w1b, w2b, ms, tf)[:db].reshape(db, ts, d)

    return (yp, ys,
            st_p[None],
            k32.reshape(1, batch, seq, ATT_KV_HEADS, HEAD_DIM),
            v32.reshape(1, batch, seq, ATT_KV_HEADS, HEAD_DIM),
            ikw[:, :IDX_DIM].reshape(1, batch, seq, IDX_DIM),
            st_s[None],
            k32s[:db].reshape(1, db, ts, ATT_KV_HEADS, HEAD_DIM),
            v32s[:db].reshape(1, db, ts, ATT_KV_HEADS, HEAD_DIM),
            ikws[:db, :IDX_DIM].reshape(1, db, ts, IDX_DIM))
```

```python
import functools

import jax
import jax.numpy as jnp
from jax import lax
from jax.experimental import pallas as pl
from jax.experimental.pallas import tpu as pltpu

RET_HEADS = 8
RET_QK_DIM = 128
RET_V_DIM = 256
ATT_HEADS = 16
ATT_KV_HEADS = 4
HEAD_DIM = 128
GROUP = ATT_HEADS // ATT_KV_HEADS
IDX_HEADS = 16
IDX_DIM = 64
IDX_W_SCALE = (IDX_HEADS ** -0.5) * (IDX_DIM ** -0.5)
TOPK_MAX = 256
PAGE_SIZE = 128
ROPE_BASE = 10000.0
EPS = 1e-6

RET_QK_W = RET_HEADS * RET_QK_DIM
RET_V_W = RET_HEADS * RET_V_DIM
ATT_Q_W = ATT_HEADS * HEAD_DIM
ATT_KV_W = ATT_KV_HEADS * HEAD_DIM
IDX_Q_W = IDX_HEADS * IDX_DIM
MAIN_W = 2 * RET_QK_W + 2 * RET_V_W + ATT_Q_W + 2 * ATT_KV_W + IDX_Q_W

LANES = 128
TN = 512
T_RQ, T_RK, T_RV, T_RG, T_AQ, T_AK, T_AV, T_IQ, T_IKW, T_END = 0, 2, 4, 8, 12, 16, 17, 18, 20, 21
VMEM_LIMIT = 56 * 1024 * 1024
INT_MIN = -2 ** 31
NEG_BIG = -1e30

bf16 = jnp.bfloat16
f32 = jnp.float32


def _params(sem, vmem=VMEM_LIMIT):
    return pltpu.CompilerParams(dimension_semantics=sem, vmem_limit_bytes=vmem)


def _sigmoid(x):
    return 1.0 / (1.0 + jnp.exp(-x))


def _dot(a, b):
    return jnp.dot(a, b, preferred_element_type=f32)


def _dot_nt(a, b):
    return lax.dot_general(a, b, (((1,), (1,)), ((), ())), preferred_element_type=f32)


def _rms(x, gain):
    ms = jnp.mean(x * x, axis=-1, keepdims=True)
    return x * lax.rsqrt(ms + EPS) * gain


def _ffn_kernel(x_ref, g_ref, w1g_ref, w1u_ref, w2_ref, o_ref, h_ref, acc_ref):
    j = pl.program_id(1)

    @pl.when(j == 0)
    def _():
        h_ref[...] = _rms(x_ref[...], g_ref[...]).astype(bf16)
        acc_ref[...] = jnp.zeros_like(acc_ref)

    h = h_ref[...]
    gate = _dot(h, w1g_ref[...])
    up = _dot(h, w1u_ref[...])
    act = (gate * _sigmoid(gate) * up).astype(bf16)
    acc_ref[...] += _dot(act, w2_ref[...])

    @pl.when(j == pl.num_programs(1) - 1)
    def _():
        o_ref[...] = x_ref[...] + 0.5 * acc_ref[...]


def _ffn(x, gain, w1, w2, tm, tf):
    m, d = x.shape
    dff = w2.shape[0]
    nf = dff // tf
    return pl.pallas_call(
        _ffn_kernel,
        grid=(m // tm, nf),
        in_specs=[
            pl.BlockSpec((tm, d), lambda i, j: (i, 0)),
            pl.BlockSpec((1, d), lambda i, j: (0, 0)),
            pl.BlockSpec((d, tf), lambda i, j: (0, j)),
            pl.BlockSpec((d, tf), lambda i, j: (0, j + nf)),
            pl.BlockSpec((tf, d), lambda i, j: (j, 0)),
        ],
        out_specs=pl.BlockSpec((tm, d), lambda i, j: (i, 0)),
        out_shape=jax.ShapeDtypeStruct((m, d), f32),
        scratch_shapes=[pltpu.VMEM((tm, d), bf16), pltpu.VMEM((tm, d), f32)],
        compiler_params=_params(("arbitrary", "arbitrary")),
    )(x, gain, w1, w1, w2)


def _proj_kernel(x_ref, g_ref, w_ref, cos_ref, sin_ref, qn_ref, kn_ref,
                 p_ref, k32_ref, v32_ref, ikw_ref, h_ref, *, base):
    j = pl.program_id(1)

    @pl.when(j == 0)
    def _():
        h_ref[...] = _rms(x_ref[...], g_ref[...]).astype(bf16)

    res = _dot_nt(h_ref[...], w_ref[...])
    heads = [slice(c * LANES, (c + 1) * LANES) for c in range(TN // LANES)]

    @pl.when(j < base)
    def _():
        p_ref[...] = _sigmoid(res).astype(bf16)

    @pl.when((j >= base + T_RQ) & (j < base + T_RV))
    def _():
        cos, sin = cos_ref[...], sin_ref[...]
        scale = jnp.where(j >= base + T_RK, RET_QK_DIM ** -0.5, 1.0).astype(f32)
        for sl in heads:
            x = res[:, sl]
            p_ref[:, sl] = ((x * cos + pltpu.roll(x, LANES // 2, 1) * sin) * scale).astype(bf16)

    @pl.when(((j >= base + T_RV) & (j < base + T_RG)) | ((j >= base + T_IQ) & (j < base + T_IKW)))
    def _():
        p_ref[...] = res.astype(bf16)

    @pl.when((j >= base + T_RG) & (j < base + T_AQ))
    def _():
        p_ref[...] = (res * _sigmoid(res)).astype(bf16)

    @pl.when((j >= base + T_AQ) & (j < base + T_AK))
    def _():
        for sl in heads:
            p_ref[:, sl] = (_rms(res[:, sl], qn_ref[...]) * HEAD_DIM ** -0.5).astype(bf16)

    @pl.when(j == base + T_AK)
    def _():
        for sl in heads:
            y = _rms(res[:, sl], kn_ref[...])
            k32_ref[:, sl] = y
            p_ref[:, sl] = y.astype(bf16)

    @pl.when(j == base + T_AV)
    def _():
        v32_ref[...] = res
        p_ref[...] = res.astype(bf16)

    @pl.when(j == base + T_IKW)
    def _():
        ikw_ref[...] = res[:, :LANES]
        p_ref[...] = res.astype(bf16)


def _proj(x, gain, w_pad, cos2, sin2, qn, kn, tm):
    m, d = x.shape
    nt = w_pad.shape[0] // TN
    base = nt - T_END
    tab_blocks = cos2.shape[0] // tm
    kern = functools.partial(_proj_kernel, base=base)
    return pl.pallas_call(
        kern,
        grid=(m // tm, nt),
        in_specs=[
            pl.BlockSpec((tm, d), lambda i, j: (i, 0)),
            pl.BlockSpec((1, d), lambda i, j: (0, 0)),
            pl.BlockSpec((TN, d), lambda i, j: (j, 0)),
            pl.BlockSpec((tm, LANES), lambda i, j: (i % tab_blocks, 0)),
            pl.BlockSpec((tm, LANES), lambda i, j: (i % tab_blocks, 0)),
            pl.BlockSpec((1, LANES), lambda i, j: (0, 0)),
            pl.BlockSpec((1, LANES), lambda i, j: (0, 0)),
        ],
        out_specs=[
            pl.BlockSpec((tm, TN), lambda i, j: (i, j)),
            pl.BlockSpec((tm, TN), lambda i, j: (i, 0)),
            pl.BlockSpec((tm, TN), lambda i, j: (i, 0)),
            pl.BlockSpec((tm, LANES), lambda i, j: (i, 0)),
        ],
        out_shape=[
            jax.ShapeDtypeStruct((m, nt * TN), bf16),
            jax.ShapeDtypeStruct((m, ATT_KV_W), f32),
            jax.ShapeDtypeStruct((m, ATT_KV_W), f32),
            jax.ShapeDtypeStruct((m, LANES), f32),
        ],
        scratch_shapes=[pltpu.VMEM((tm, d), bf16)],
        compiler_params=_params(("arbitrary", "arbitrary")),
    )(x, gain, w_pad, cos2, sin2, qn, kn)


def _ret_kernel(lg_ref, q_ref, k_ref, v_ref, sg_ref, gn_ref, og_ref, st_ref, state_ref, *, chunk):
    h = pl.program_id(1)
    c = pl.program_id(2)
    lg = lg_ref[h]

    @pl.when(c == 0)
    def _():
        state_ref[...] = jnp.zeros_like(state_ref)

    q, k, v = q_ref[...], k_ref[...], v_ref[...]
    ii = lax.broadcasted_iota(jnp.int32, (chunk, chunk), 0)
    jj = lax.broadcasted_iota(jnp.int32, (chunk, chunk), 1)
    rel = (ii - jj).astype(f32)
    dmat = jnp.where(rel >= 0, jnp.exp(lg * jnp.maximum(rel, 0.0)), 0.0)
    scores = _dot_nt(q, k) * dmat
    inner = _dot(scores.astype(bf16), v)
    i1 = lax.broadcasted_iota(jnp.int32, (chunk, 1), 0).astype(f32)
    q_dec = jnp.exp(lg * (i1 + 1.0))
    k_dec = jnp.exp(lg * (chunk - 1.0 - i1))
    state = state_ref[...]
    cross = _dot((q.astype(f32) * q_dec).astype(bf16), state.astype(bf16))
    kd = (k.astype(f32) * k_dec).astype(bf16)
    new_state = jnp.exp(lg * chunk) * state + lax.dot_general(
        kd, v, (((0,), (0,)), ((), ())), preferred_element_type=f32)
    state_ref[...] = new_state
    o = _rms(inner + cross, gn_ref[...])
    og_ref[...] = (o * sg_ref[...].astype(f32)).astype(bf16)

    @pl.when(c == pl.num_programs(2) - 1)
    def _():
        st_ref[0, 0] = new_state


def _retention(p, log_g, gn, batch, seq, base, chunk):
    nc = seq // chunk
    qb = (base + T_RQ) * (TN // RET_QK_DIM)
    kb = (base + T_RK) * (TN // RET_QK_DIM)
    vb = (base + T_RV) * (TN // RET_V_DIM)
    gb = (base + T_RG) * (TN // RET_V_DIM)
    kern = functools.partial(_ret_kernel, chunk=chunk)
    grid_spec = pltpu.PrefetchScalarGridSpec(
        num_scalar_prefetch=1,
        grid=(batch, RET_HEADS, nc),
        in_specs=[
            pl.BlockSpec((chunk, RET_QK_DIM), lambda b, h, c, lg: (b * nc + c, qb + h)),
            pl.BlockSpec((chunk, RET_QK_DIM), lambda b, h, c, lg: (b * nc + c, kb + h)),
            pl.BlockSpec((chunk, RET_V_DIM), lambda b, h, c, lg: (b * nc + c, vb + h)),
            pl.BlockSpec((chunk, RET_V_DIM), lambda b, h, c, lg: (b * nc + c, gb + h)),
            pl.BlockSpec((1, RET_V_DIM), lambda b, h, c, lg: (0, 0)),
        ],
        out_specs=[
            pl.BlockSpec((chunk, RET_V_DIM), lambda b, h, c, lg: (b * nc + c, h)),
            pl.BlockSpec((1, 1, RET_QK_DIM, RET_V_DIM), lambda b, h, c, lg: (b, h, 0, 0)),
        ],
        scratch_shapes=[pltpu.VMEM((RET_QK_DIM, RET_V_DIM), f32)],
    )
    return pl.pallas_call(
        kern,
        grid_spec=grid_spec,
        out_shape=[
            jax.ShapeDtypeStruct((batch * seq, RET_V_W), bf16),
            jax.ShapeDtypeStruct((batch, RET_HEADS, RET_QK_DIM, RET_V_DIM), f32),
        ],
        compiler_params=_params(("arbitrary", "arbitrary", "arbitrary")),
    )(log_g, p, p, p, p, gn)


def _sortable_key(score):
    bits = lax.bitcast_convert_type(score, jnp.int32)
    return jnp.where(bits >= 0, bits, bits ^ jnp.int32(0x7FFFFFFF))


def _kth_largest_key(count_ge, n_keep, shape):
    def body(it, t_u):
        bit = jnp.left_shift(jnp.int32(1), 31 - it)
        cand_u = t_u | bit
        cnt = count_ge(cand_u ^ jnp.int32(INT_MIN))
        return jnp.where(cnt >= n_keep, cand_u, t_u)
    t_u = lax.fori_loop(0, 32, body, jnp.zeros(shape, jnp.int32))
    return t_u ^ jnp.int32(INT_MIN)


def _attn_kernel(aq_ref, k_ref, v_ref, iq_ref, ikw_k_ref, ikw_q_ref, o_ref,
                 key_ref, iklo_ref, ikhi_ref, vt_ref, m_ref, l_ref, acc_ref, *, tq, ck, seq, n_keep):
    qb = pl.program_id(1)

    @pl.when(qb == 0)
    def _():
        ikw = ikw_k_ref[...]
        lane = lax.broadcasted_iota(jnp.int32, (seq, LANES), 1)
        iklo_ref[...] = jnp.where(lane < IDX_DIM, ikw, 0.0).astype(bf16)
        ikhi_ref[...] = jnp.where(lane >= IDX_DIM, pltpu.roll(ikw, IDX_DIM, 1), 0.0).astype(bf16)

        for c0 in range(0, seq, ck):
            vt_ref[:, c0:c0 + ck] = jnp.transpose(v_ref[c0:c0 + ck, :].astype(f32)).astype(bf16)

    n_chunks = (qb * tq + tq - 1) // ck + 1
    chunk = lambda kc: pl.ds(pl.multiple_of(kc * ck, ck), ck)
    stack = 4
    pairs = IDX_HEADS // 2

    iw_t = jnp.transpose(ikw_q_ref[...]) * IDX_W_SCALE
    q_stacks = [jnp.concatenate([iq_ref[:, hp * LANES:(hp + 1) * LANES] for hp in range(h0, h0 + stack)], axis=0)
                for h0 in range(0, pairs, stack)]
    k_off = lax.broadcasted_iota(jnp.int32, (ck, tq), 0)
    q_pos = qb * tq + lax.broadcasted_iota(jnp.int32, (ck, tq), 1)

    def score_body(kc, c):
        scores = jnp.zeros((ck, tq), f32)
        for si, qs in enumerate(q_stacks):
            for sub, ik_ref in enumerate((iklo_ref, ikhi_ref)):
                s = jnp.maximum(_dot_nt(ik_ref[chunk(kc), :], qs), 0.0)
                for i in range(stack):
                    row = IDX_DIM + 2 * (si * stack + i) + sub
                    scores = scores + s[:, i * tq:(i + 1) * tq] * iw_t[row:row + 1, :]
        visible = kc * ck + k_off <= q_pos
        key_ref[chunk(kc), :] = jnp.where(visible, _sortable_key(scores), jnp.int32(INT_MIN))
        return c
    lax.fori_loop(0, n_chunks, score_body, 0)

    def count_ge(t):
        t_b = jnp.broadcast_to(t, (8, tq))

        def body(kc, acc):
            keys = key_ref[chunk(kc), :].reshape(ck // 8, 8, tq)
            return acc + jnp.sum(jnp.where(keys >= t_b, 1.0, 0.0), axis=0)
        acc = lax.fori_loop(0, n_chunks, body, jnp.zeros((8, tq), f32))
        return jnp.sum(acc, axis=0, keepdims=True)

    thr = jnp.maximum(_kth_largest_key(count_ge, n_keep, (1, tq)), jnp.int32(INT_MIN + 1))

    m_ref[...] = jnp.full_like(m_ref, NEG_BIG)
    l_ref[...] = jnp.zeros_like(l_ref)
    acc_ref[...] = jnp.zeros_like(acc_ref)
    q_groups = [jnp.concatenate([aq_ref[:, hd * HEAD_DIM:(hd + 1) * HEAD_DIM]
                                 for hd in range(g * GROUP, (g + 1) * GROUP)], axis=0)
                for g in range(ATT_KV_HEADS)]

    def attn_body(kc, c):
        bias = jnp.where(key_ref[chunk(kc), :] >= thr, 0.0, -jnp.inf)
        bias = jnp.concatenate([bias] * GROUP, axis=1)
        for g in range(ATT_KV_HEADS):
            cols = slice(g * GROUP * tq, (g + 1) * GROUP * tq)
            kg = k_ref[chunk(kc), g * HEAD_DIM:(g + 1) * HEAD_DIM]
            s = _dot_nt(kg, q_groups[g]) + bias
            m_old = m_ref[:, cols]
            m_new = jnp.maximum(m_old, jnp.max(s, axis=0, keepdims=True))
            p = jnp.exp(s - m_new)
            alpha = jnp.exp(m_old - m_new)
            l_ref[:, cols] = alpha * l_ref[:, cols] + jnp.sum(p, axis=0, keepdims=True)
            vt = vt_ref[g * HEAD_DIM:(g + 1) * HEAD_DIM, chunk(kc)]
            acc_ref[:, cols] = alpha * acc_ref[:, cols] + _dot(vt, p.astype(bf16))
            m_ref[:, cols] = m_new
        return c
    lax.fori_loop(0, n_chunks, attn_body, 0)

    for hd in range(ATT_HEADS):
        cols = slice(hd * tq, (hd + 1) * tq)
        o_t = acc_ref[:, cols] / l_ref[:, cols]
        o_ref[:, hd * HEAD_DIM:(hd + 1) * HEAD_DIM] = jnp.transpose(o_t).astype(bf16)


def _prompt_attention(p, ikw, batch, seq, base, tq, n_keep):
    nq = seq // tq
    ck = tq
    kern = functools.partial(_attn_kernel, tq=tq, ck=ck, seq=seq, n_keep=n_keep)
    aq_blk = (base + T_AQ) * TN // ATT_Q_W
    iq_blk = (base + T_IQ) * TN // IDX_Q_W
    return pl.pallas_call(
        kern,
        grid=(batch, nq),
        in_specs=[
            pl.BlockSpec((tq, ATT_Q_W), lambda b, t: (b * nq + t, aq_blk)),
            pl.BlockSpec((seq, ATT_KV_W), lambda b, t: (b, base + T_AK)),
            pl.BlockSpec((seq, ATT_KV_W), lambda b, t: (b, base + T_AV)),
            pl.BlockSpec((tq, IDX_Q_W), lambda b, t: (b * nq + t, iq_blk)),
            pl.BlockSpec((seq, LANES), lambda b, t: (b, 0)),
            pl.BlockSpec((tq, LANES), lambda b, t: (b * nq + t, 0)),
        ],
        out_specs=pl.BlockSpec((tq, ATT_Q_W), lambda b, t: (b * nq + t, 0)),
        out_shape=jax.ShapeDtypeStruct((batch * seq, ATT_Q_W), bf16),
        scratch_shapes=[pltpu.VMEM((seq, tq), jnp.int32), pltpu.VMEM((seq, LANES), bf16),
                        pltpu.VMEM((seq, LANES), bf16), pltpu.VMEM((ATT_KV_W, seq), bf16),
                        pltpu.VMEM((1, ATT_HEADS * tq), f32), pltpu.VMEM((1, ATT_HEADS * tq), f32),
                        pltpu.VMEM((HEAD_DIM, ATT_HEADS * tq), f32)],
        compiler_params=_params(("arbitrary", "arbitrary")),
    )(p, p, p, p, ikw, ikw)


def _oproj_kernel(og_ref, att_ref, gr_ref, ga_ref, x_ref, wr_ref, wa_ref, wo_ref, o_ref):
    ret_out = _dot(og_ref[...], wr_ref[...])
    att_out = _dot(att_ref[...], wa_ref[...])
    merged = gr_ref[...].astype(f32) * ret_out + ga_ref[...].astype(f32) * att_out
    o_ref[...] = x_ref[...] + _dot(merged.astype(bf16), wo_ref[...])


def _oproj(og, att, p, x, wr, wa, wo, tm):
    m, d = x.shape
    const = lambda shape: pl.BlockSpec(shape, lambda i: (0, 0), pipeline_mode=pl.Buffered(1))
    return pl.pallas_call(
        _oproj_kernel,
        grid=(m // tm,),
        in_specs=[
            pl.BlockSpec((tm, RET_V_W), lambda i: (i, 0)),
            pl.BlockSpec((tm, ATT_Q_W), lambda i: (i, 0)),
            pl.BlockSpec((tm, d), lambda i: (i, 0)),
            pl.BlockSpec((tm, d), lambda i: (i, 1)),
            pl.BlockSpec((tm, d), lambda i: (i, 0)),
            const(wr.shape), const(wa.shape), const(wo.shape),
        ],
        out_specs=pl.BlockSpec((tm, d), lambda i: (i, 0)),
        out_shape=jax.ShapeDtypeStruct((m, d), f32),
        compiler_params=_params(("arbitrary",)),
    )(og, att, p, p, x, wr, wa, wo)


def _dec_ret_kernel(lg_ref, q_ref, k_ref, v_ref, sg_ref, gn_ref, st_ref, og_ref, nst_ref):
    for h in range(RET_HEADS):
        lg = lg_ref[h]
        gamma = jnp.exp(lg)
        q = q_ref[0, :, h * RET_QK_DIM:(h + 1) * RET_QK_DIM]
        k = k_ref[0, :, h * RET_QK_DIM:(h + 1) * RET_QK_DIM]
        v = v_ref[0, :, h * RET_V_DIM:(h + 1) * RET_V_DIM]
        sg = sg_ref[0, :, h * RET_V_DIM:(h + 1) * RET_V_DIM]
        state = st_ref[0, 0, h]
        qk = jnp.sum(q * k, axis=1, keepdims=True)
        q8 = jnp.broadcast_to((q * gamma).astype(bf16), (8, RET_QK_DIM))
        cross = _dot(q8, state.astype(bf16))[0:1]
        k_col = jnp.transpose(jnp.broadcast_to(k, (8, RET_QK_DIM)))[:, 0:1]
        nst_ref[0, h] = gamma * state + k_col * v
        o = _rms(qk * v + cross, gn_ref[...])
        og_ref[0, :, h * RET_V_DIM:(h + 1) * RET_V_DIM] = o * sg


def _decode_retention(q, k, v, sg, gn, state, log_g):
    db = q.shape[0]
    row = lambda w: pl.BlockSpec((1, 1, w), lambda b, lg: (b, 0, 0))
    grid_spec = pltpu.PrefetchScalarGridSpec(
        num_scalar_prefetch=1,
        grid=(db,),
        in_specs=[row(RET_QK_W), row(RET_QK_W), row(RET_V_W), row(RET_V_W),
                  pl.BlockSpec((1, RET_V_DIM), lambda b, lg: (0, 0)),
                  pl.BlockSpec((1, 1, RET_HEADS, RET_QK_DIM, RET_V_DIM), lambda b, lg: (0, b, 0, 0, 0))],
        out_specs=[row(RET_V_W),
                   pl.BlockSpec((1, RET_HEADS, RET_QK_DIM, RET_V_DIM), lambda b, lg: (b, 0, 0, 0))],
    )
    return pl.pallas_call(
        _dec_ret_kernel,
        grid_spec=grid_spec,
        out_shape=[jax.ShapeDtypeStruct((db, 1, RET_V_W), f32),
                   jax.ShapeDtypeStruct((db, RET_HEADS, RET_QK_DIM, RET_V_DIM), f32)],
        compiler_params=_params(("arbitrary",)),
    )(log_g, q, k, v, sg, gn, state)


def _dec_score_kernel(pt_ref, qi_ref, w_ref, pool_ref, s_ref, buf_ref, sem_ref, *, n_pages):
    b = pl.program_id(0)
    slot = b % 2

    def page_copy(seq, p, slot_):
        dst = buf_ref.at[slot_, :, pl.ds(pl.multiple_of(p * PAGE_SIZE, PAGE_SIZE), PAGE_SIZE)]
        return pltpu.make_async_copy(pool_ref.at[pt_ref[seq, p]], dst, sem_ref.at[slot_])

    def start_all(seq, slot_):
        def body(p, c):
            page_copy(seq, p, slot_).start()
            return c
        lax.fori_loop(0, n_pages, body, 0)

    @pl.when(b == 0)
    def _():
        start_all(0, 0)

    @pl.when(b + 1 < pl.num_programs(0))
    def _():
        start_all(b + 1, 1 - slot)

    def wait_body(p, c):
        page_copy(b, p, slot).wait()
        return c
    lax.fori_loop(0, n_pages, wait_body, 0)

    s = jnp.maximum(_dot(qi_ref[0], buf_ref[slot].astype(bf16)), 0.0)
    row = jnp.sum(s * w_ref[0], axis=0, keepdims=True)
    for p in range(n_pages):
        s_ref[0, p:p + 1, :] = row[:, p * PAGE_SIZE:(p + 1) * PAGE_SIZE]


def _decode_scores(page_table, qi, w, pool_ikt):
    db, n_pages = page_table.shape
    kern = functools.partial(_dec_score_kernel, n_pages=n_pages)
    grid_spec = pltpu.PrefetchScalarGridSpec(
        num_scalar_prefetch=1,
        grid=(db,),
        in_specs=[
            pl.BlockSpec((1, IDX_HEADS, IDX_DIM), lambda b, pt: (b, 0, 0)),
            pl.BlockSpec((1, IDX_HEADS, 1), lambda b, pt: (b, 0, 0)),
            pl.BlockSpec(memory_space=pl.ANY),
        ],
        out_specs=pl.BlockSpec((1, n_pages, PAGE_SIZE), lambda b, pt: (b, 0, 0)),
        scratch_shapes=[pltpu.VMEM((2, IDX_DIM, n_pages * PAGE_SIZE), f32), pltpu.SemaphoreType.DMA((2,))],
    )
    return pl.pallas_call(
        kern,
        grid_spec=grid_spec,
        out_shape=jax.ShapeDtypeStruct((db, n_pages, PAGE_SIZE), f32),
        compiler_params=_params(("arbitrary",)),
    )(page_table, qi, w, pool_ikt)


def _dec_select_kernel(s_ref, qi_ref, w_ref, ikn_ref, idx_ref, cnt_ref, nsel_ref, rank_ref,
                       *, n_keep, kmax):
    db, n_pages, _ = s_ref.shape
    ikn = ikn_ref[...].astype(bf16).astype(f32)
    dots = jnp.sum(qi_ref[...].astype(f32) * ikn, axis=2, keepdims=True)
    s_new = jnp.sum(jnp.maximum(dots, 0.0) * w_ref[...], axis=1, keepdims=True)
    keys = _sortable_key(s_ref[...])
    key_new = _sortable_key(s_new)

    def count_ge(t):
        c = jnp.sum(jnp.where(keys >= t, 1.0, 0.0), axis=1, keepdims=True)
        return jnp.sum(c, axis=2, keepdims=True) + jnp.where(key_new >= t, 1.0, 0.0)

    thr = _kth_largest_key(count_ge, n_keep, (db, 1, 1))
    sel = jnp.where(keys >= thr, 1.0, 0.0)
    nsel_ref[...] = jnp.broadcast_to(jnp.where(key_new >= thr, 1.0, 0.0), (db, 1, LANES))

    pp = lax.broadcasted_iota(jnp.int32, (n_pages, n_pages), 0)
    pq = lax.broadcasted_iota(jnp.int32, (n_pages, n_pages), 1)
    earlier_page = jnp.where(pq < pp, 1.0, 0.0).astype(bf16)
    oo = lax.broadcasted_iota(jnp.int32, (PAGE_SIZE, PAGE_SIZE), 0)
    oq = lax.broadcasted_iota(jnp.int32, (PAGE_SIZE, PAGE_SIZE), 1)
    earlier_off = jnp.where(oo < oq, 1.0, 0.0).astype(bf16)
    r_iota = lax.broadcasted_iota(jnp.int32, (kmax, PAGE_SIZE), 0).astype(f32)
    lane = lax.broadcasted_iota(jnp.int32, (kmax, PAGE_SIZE), 1).astype(f32)
    for b in range(db):
        sb = sel[b]
        per_page = jnp.sum(sb, axis=1, keepdims=True)
        before = _dot(earlier_page, jnp.broadcast_to(per_page, (n_pages, PAGE_SIZE)).astype(bf16))
        within = _dot(sb.astype(bf16), earlier_off)
        rank_ref[b] = jnp.where(sb > 0.0, before + within, -1.0)

        def body(p, acc, b=b):
            row = rank_ref[b, pl.ds(p, 1), :]
            return jnp.where(row == r_iota, lane + jnp.asarray(p * PAGE_SIZE, f32), acc)

        acc = lax.fori_loop(0, n_pages, body, jnp.zeros((kmax, PAGE_SIZE), f32))
        idx_ref[b] = jnp.sum(acc, axis=1, keepdims=True).astype(jnp.int32)
        total = jnp.sum(per_page, axis=0, keepdims=True)
        cnt_ref[b] = jnp.broadcast_to(total, (1, LANES)).astype(jnp.int32)


def _decode_select(scores, qi, w, ik_new, n_keep):
    db, n_pages, _ = scores.shape
    kmax = -(-n_keep // 8) * 8
    kern = functools.partial(_dec_select_kernel, n_keep=n_keep, kmax=kmax)
    full = lambda shape: pl.BlockSpec(shape, lambda i: (0,) * len(shape))
    return pl.pallas_call(
        kern,
        grid=(1,),
        in_specs=[full(scores.shape), full(qi.shape), full(w.shape), full(ik_new.shape)],
        out_specs=[full((db, kmax, 1)), full((db, 1, LANES)), full((db, 1, LANES))],
        out_shape=[jax.ShapeDtypeStruct((db, kmax, 1), jnp.int32),
                   jax.ShapeDtypeStruct((db, 1, LANES), jnp.int32),
                   jax.ShapeDtypeStruct((db, 1, LANES), f32)],
        scratch_shapes=[pltpu.VMEM((db, n_pages, PAGE_SIZE), f32)],
        compiler_params=_params(("arbitrary",)),
    )(scores, qi, w, ik_new)


def _dec_attn_kernel(idx_ref, pt_ref, cnt_ref, q_ref, kn_ref, vn_ref, nsel_ref, kpool_ref, vpool_ref, o_ref,
                     kbuf_ref, vbuf_ref, sem_ref, *, kmax):
    b = pl.program_id(0)
    slot = b % 2
    page_shift = PAGE_SIZE.bit_length() - 1

    def row_copies(seq, r, slot_):
        pos = idx_ref[seq, r]
        phys = pt_ref[seq, lax.shift_right_logical(pos, page_shift)]
        src0 = pl.multiple_of((pos & (PAGE_SIZE - 1)) * ATT_KV_HEADS, ATT_KV_HEADS)
        dst0 = pl.multiple_of(r * ATT_KV_HEADS, ATT_KV_HEADS)
        return (pltpu.make_async_copy(kpool_ref.at[phys, pl.ds(src0, ATT_KV_HEADS), :],
                                      kbuf_ref.at[slot_, pl.ds(dst0, ATT_KV_HEADS), :], sem_ref.at[0, slot_]),
                pltpu.make_async_copy(vpool_ref.at[phys, pl.ds(src0, ATT_KV_HEADS), :],
                                      vbuf_ref.at[slot_, pl.ds(dst0, ATT_KV_HEADS), :], sem_ref.at[1, slot_]))

    def start_all(seq, slot_):
        def body(r, c):
            ck, cv = row_copies(seq, r, slot_)
            ck.start()
            cv.start()
            return c
        lax.fori_loop(0, kmax, body, 0)

    @pl.when(b == 0)
    def _():
        start_all(0, 0)

    @pl.when(b + 1 < pl.num_programs(0))
    def _():
        start_all(b + 1, 1 - slot)

    def wait_body(r, c):
        ck, cv = row_copies(b, r, slot)
        ck.wait()
        cv.wait()
        return c
    lax.fori_loop(0, kmax, wait_body, 0)

    q = q_ref[0]
    s = _dot_nt(q, kbuf_ref[slot].astype(bf16))
    col = lax.broadcasted_iota(jnp.int32, s.shape, 1)
    head = lax.broadcasted_iota(jnp.int32, s.shape, 0)
    ok = (col % ATT_KV_HEADS == head // GROUP) & (col // ATT_KV_HEADS < cnt_ref[b])
    s = jnp.where(ok, s, -jnp.inf)
    s_n = jnp.sum(q.astype(f32) * kn_ref[0].astype(f32), axis=1, keepdims=True)
    s_n = jnp.where(nsel_ref[0][:, 0:1] > 0.0, s_n, -jnp.inf)
    m = jnp.maximum(jnp.max(s, axis=1, keepdims=True), s_n)
    p = jnp.exp(s - m)
    p_n = jnp.exp(s_n - m)
    l = jnp.sum(p, axis=1, keepdims=True) + p_n
    pv = _dot(p.astype(bf16), vbuf_ref[slot].astype(bf16))
    o_ref[0] = (pv + p_n.astype(bf16).astype(f32) * vn_ref[0].astype(f32)) / l


def _decode_attention(idx, page_table, cnt, q, k_new, v_new, nsel, pool_k, pool_v):
    db, kmax = idx.shape
    kern = functools.partial(_dec_attn_kernel, kmax=kmax)
    seqb = lambda shape: pl.BlockSpec(shape, lambda b, *_: (b, 0, 0))
    grid_spec = pltpu.PrefetchScalarGridSpec(
        num_scalar_prefetch=3,
        grid=(db,),
        in_specs=[
            seqb((1, ATT_HEADS, HEAD_DIM)),
            seqb((1, ATT_HEADS, HEAD_DIM)),
            seqb((1, ATT_HEADS, HEAD_DIM)),
            seqb((1, 1, LANES)),
            pl.BlockSpec(memory_space=pl.ANY),
            pl.BlockSpec(memory_space=pl.ANY),
        ],
        out_specs=seqb((1, ATT_HEADS, HEAD_DIM)),
        scratch_shapes=[pltpu.VMEM((2, kmax * ATT_KV_HEADS, HEAD_DIM), f32),
                        pltpu.VMEM((2, kmax * ATT_KV_HEADS, HEAD_DIM), f32),
                        pltpu.SemaphoreType.DMA((2, 2))],
    )
    return pl.pallas_call(
        kern,
        grid_spec=grid_spec,
        out_shape=jax.ShapeDtypeStruct((db, ATT_HEADS, HEAD_DIM), f32),
        compiler_params=_params(("arbitrary",)),
    )(idx, page_table, cnt, q, k_new, v_new, nsel, pool_k, pool_v)


def _rope_tables(pos):
    half = RET_QK_DIM // 2
    inv = ROPE_BASE ** (-jnp.arange(half, dtype=f32) / half)
    ang = pos.astype(f32)[:, None] * inv[None, :]
    cos, sin = jnp.cos(ang), jnp.sin(ang)
    return jnp.concatenate([cos, cos], axis=1), jnp.concatenate([-sin, sin], axis=1)


def _pad_w_in(w_in):
    d = w_in.shape[0]
    w_t = jnp.swapaxes(w_in, 0, 1)
    small = w_t[MAIN_W:MAIN_W + IDX_DIM + IDX_HEADS]
    pad = jnp.zeros((TN - IDX_DIM - IDX_HEADS, d), w_in.dtype)
    gates = w_t[MAIN_W + IDX_DIM + IDX_HEADS:]
    return jnp.concatenate([gates, w_t[:MAIN_W], small, pad], axis=0).astype(bf16)


def _pick(n, pref):
    for t in pref:
        if n % t == 0:
            return t
    return n


def kernel(x_prompt, x_sample, state_ret, cache_k, cache_v, cache_idx_k, page_table, ffn1_norm, ffn1_w1, ffn1_w2, mix_norm, w_in, q_norm, k_norm, ret_norm, w_ret_out, w_att_out, w_o, ffn2_norm, ffn2_w1, ffn2_w2):
    batch, seq, d = x_prompt.shape
    db, ts, _ = x_sample.shape
    depth = w_in.shape[0]
    assert depth == 1 and ts == 1 and d % (2 * TN) == 0
    n_pages = page_table.shape[1]
    past = n_pages * PAGE_SIZE
    n_phys = cache_k.shape[1]
    base = 2 * (d // TN)
    m = batch * seq
    ms = 16

    log_g = jnp.log1p(-jnp.exp2(-5.0 - jnp.arange(RET_HEADS, dtype=f32)))
    w1a, w2a = ffn1_w1[0].astype(bf16), ffn1_w2[0].astype(bf16)
    w1b, w2b = ffn2_w1[0].astype(bf16), ffn2_w2[0].astype(bf16)
    w_pad = _pad_w_in(w_in[0])
    wr, wa, wo = w_ret_out[0].astype(bf16), w_att_out[0].astype(bf16), w_o[0].astype(bf16)
    g1, g2, g3 = ffn1_norm, mix_norm, ffn2_norm
    qn, kn, gn = q_norm, k_norm, ret_norm

    tm = _pick(m, (512, 256, 128))
    tf = _pick(ffn1_w2.shape[1], (512, 256, 128))
    tmp = _pick(seq, (1024, 512, 256, 128))
    chunk = _pick(seq, (256, 128))
    tq = _pick(seq, (256, 128))
    tmo = _pick(m, (256, 128))

    xp = x_prompt.reshape(m, d)
    x1 = _ffn(xp, g1, w1a, w2a, tm, tf)
    cos_p, sin_p = _rope_tables(jnp.arange(seq))
    p, k32, v32, ikw = _proj(x1, g2, w_pad, cos_p, sin_p, qn, kn, tmp)
    og, st_p = _retention(p, log_g, gn, batch, seq, base, chunk)
    att = _prompt_attention(p, ikw, batch, seq, base, tq, min(TOPK_MAX, seq // 4))
    x2 = _oproj(og, att, p, x1, wr, wa, wo, tmo)
    yp = _ffn(x2, g3, w1b, w2b, tm, tf).reshape(batch, seq, d)

    xs = jnp.zeros((ms, d), f32).at[:db].set(x_sample.reshape(db, d))
    s1 = _ffn(xs, g1, w1a, w2a, ms, tf)
    cos_s, sin_s = _rope_tables(jnp.full((ms,), past, jnp.int32))
    ps, k32s, v32s, ikws = _proj(s1, g2, w_pad, cos_s, sin_s, qn, kn, ms)
    seg = lambda t0, t1: ps[:db, (base + t0) * TN:(base + t1) * TN]
    row3 = lambda a: a.astype(f32).reshape(db, 1, a.shape[-1])
    og_s, st_s = _decode_retention(row3(seg(T_RQ, T_RK)), row3(seg(T_RK, T_RV)), row3(seg(T_RV, T_RG)),
                                   row3(seg(T_RG, T_AQ)), gn, state_ret, log_g)
    qi = seg(T_IQ, T_IKW).reshape(db, IDX_HEADS, IDX_DIM)
    wi = (ikws[:db, IDX_DIM:IDX_DIM + IDX_HEADS] * IDX_W_SCALE).reshape(db, IDX_HEADS, 1)
    ik_new = ikws[:db, :IDX_DIM].reshape(db, 1, IDX_DIM)
    pool_ikt = jnp.swapaxes(cache_idx_k.reshape(n_phys, PAGE_SIZE, IDX_DIM), 1, 2)
    scores = _decode_scores(page_table, qi, wi, pool_ikt)
    idx, cnt, nsel = _decode_select(scores, qi, wi, ik_new, min(TOPK_MAX, (past + ts) // 4))
    aq_s = seg(T_AQ, T_AK).reshape(db, ATT_HEADS, HEAD_DIM)
    expand = lambda a: jnp.repeat(a[:db].reshape(db, ATT_KV_HEADS, HEAD_DIM), GROUP, axis=1).astype(bf16)
    pool_rows = lambda c: c.reshape(n_phys, PAGE_SIZE * ATT_KV_HEADS, HEAD_DIM)
    att_s = _decode_attention(idx.reshape(db, -1), page_table, cnt[:, 0, 0], aq_s, expand(k32s), expand(v32s),
                              nsel, pool_rows(cache_k), pool_rows(cache_v))
    pad_rows = lambda a: jnp.pad(a.reshape(db, -1).astype(bf16), ((0, ms - db), (0, 0)))
    s2 = _oproj(pad_rows(og_s), pad_rows(att_s), ps, s1, wr, wa, wo, ms)
    ys = _ffn(s2, g3, w1b, w2b, ms, tf)[:db].reshape(db, ts, d)

    return (yp, ys,
            st_p[None],
            k32.reshape(1, batch, seq, ATT_KV_HEADS, HEAD_DIM),
            v32.reshape(1, batch, seq, ATT_KV_HEADS, HEAD_DIM),
            ikw[:, :IDX_DIM].reshape(1, batch, seq, IDX_DIM),
            st_s[None],
            k32s[:db].reshape(1, db, ts, ATT_KV_HEADS, HEAD_DIM),
            v32s[:db].reshape(1, db, ts, ATT_KV_HEADS, HEAD_DIM),
            ikws[:db, :IDX_DIM].reshape(1, db, ts, IDX_DIM))
```

```python
import functools

import jax
import jax.numpy as jnp
from jax import lax
from jax.experimental import pallas as pl
from jax.experimental.pallas import tpu as pltpu

RET_HEADS = 8
RET_QK_DIM = 128
RET_V_DIM = 256
ATT_HEADS = 16
ATT_KV_HEADS = 4
HEAD_DIM = 128
GROUP = ATT_HEADS // ATT_KV_HEADS
IDX_HEADS = 16
IDX_DIM = 64
IDX_W_SCALE = (IDX_HEADS ** -0.5) * (IDX_DIM ** -0.5)
TOPK_MAX = 256
PAGE_SIZE = 128
ROPE_BASE = 10000.0
EPS = 1e-6

RET_QK_W = RET_HEADS * RET_QK_DIM
RET_V_W = RET_HEADS * RET_V_DIM
ATT_Q_W = ATT_HEADS * HEAD_DIM
ATT_KV_W = ATT_KV_HEADS * HEAD_DIM
IDX_Q_W = IDX_HEADS * IDX_DIM
MAIN_W = 2 * RET_QK_W + 2 * RET_V_W + ATT_Q_W + 2 * ATT_KV_W + IDX_Q_W

LANES = 128
TN = 512
T_RQ, T_RK, T_RV, T_RG, T_AQ, T_AK, T_AV, T_IQ, T_IKW, T_END = 0, 2, 4, 8, 12, 16, 17, 18, 20, 21
VMEM_LIMIT = 56 * 1024 * 1024
INT_MIN = -2 ** 31
NEG_BIG = -1e30

bf16 = jnp.bfloat16
f32 = jnp.float32


def _params(sem, vmem=VMEM_LIMIT):
    return pltpu.CompilerParams(dimension_semantics=sem, vmem_limit_bytes=vmem)


def _sigmoid(x):
    return 1.0 / (1.0 + jnp.exp(-x))


def _dot(a, b):
    return jnp.dot(a, b, preferred_element_type=f32)


def _dot_nt(a, b):
    return lax.dot_general(a, b, (((1,), (1,)), ((), ())), preferred_element_type=f32)


def _rms(x, gain):
    ms = jnp.mean(x * x, axis=-1, keepdims=True)
    return x * lax.rsqrt(ms + EPS) * gain


def _ffn_kernel(x_ref, g_ref, w1g_ref, w1u_ref, w2_ref, o_ref, h_ref):
    j = pl.program_id(1)

    @pl.when(j == 0)
    def _():
        h_ref[...] = _rms(x_ref[...], g_ref[...]).astype(bf16)
        o_ref[...] = jnp.zeros_like(o_ref)

    h = h_ref[...]
    gate = _dot(h, w1g_ref[...].astype(bf16))
    up = _dot(h, w1u_ref[...].astype(bf16))
    act = (gate * _sigmoid(gate) * up).astype(bf16)
    o_ref[...] += _dot(act, w2_ref[...].astype(bf16))

    @pl.when(j == pl.num_programs(1) - 1)
    def _():
        o_ref[...] = x_ref[...] + 0.5 * o_ref[...]


def _ffn(x, gain, w1, w2, tm, tf):
    m, d = x.shape
    dff = w2.shape[0]
    nf = dff // tf
    return pl.pallas_call(
        _ffn_kernel,
        grid=(m // tm, nf),
        in_specs=[
            pl.BlockSpec((tm, d), lambda i, j: (i, 0), pipeline_mode=pl.Buffered(1)),
            pl.BlockSpec((1, d), lambda i, j: (0, 0)),
            pl.BlockSpec((d, tf), lambda i, j: (0, j)),
            pl.BlockSpec((d, tf), lambda i, j: (0, j + nf)),
            pl.BlockSpec((tf, d), lambda i, j: (j, 0)),
        ],
        out_specs=pl.BlockSpec((tm, d), lambda i, j: (i, 0)),
        out_shape=jax.ShapeDtypeStruct((m, d), f32),
        scratch_shapes=[pltpu.VMEM((tm, d), bf16)],
        compiler_params=_params(("arbitrary", "arbitrary")),
    )(x, gain, w1, w1, w2)


def _proj_kernel(x_ref, g_ref, w_ref, cos_ref, sin_ref, qn_ref, kn_ref,
                 p_ref, k32_ref, v32_ref, ikw_ref, h_ref, *, base):
    j = pl.program_id(1)

    @pl.when(j == 0)
    def _():
        h_ref[...] = _rms(x_ref[...], g_ref[...]).astype(bf16)

    w = w_ref[...].astype(bf16)
    heads = [slice(c * LANES, (c + 1) * LANES) for c in range(TN // LANES)]
    tm = h_ref.shape[0]
    rc = min(tm, 256)

    def segment(cond, epilogue):
        @pl.when(cond)
        def _():
            for r0 in range(0, tm, rc):
                rows = slice(r0, r0 + rc)
                epilogue(_dot_nt(h_ref[rows, :], w), rows)

    def gates(res, rows):
        p_ref[rows, :] = _sigmoid(res).astype(bf16)

    def rotary(res, rows):
        cos, sin = cos_ref[rows, :], sin_ref[rows, :]
        scale = jnp.where(j >= base + T_RK, RET_QK_DIM ** -0.5, 1.0).astype(f32)
        for sl in heads:
            x = res[:, sl]
            p_ref[rows, sl] = ((x * cos + pltpu.roll(x, LANES // 2, 1) * sin) * scale).astype(bf16)

    def plain(res, rows):
        p_ref[rows, :] = res.astype(bf16)

    def silu(res, rows):
        p_ref[rows, :] = (res * _sigmoid(res)).astype(bf16)

    def q_norm(res, rows):
        for sl in heads:
            p_ref[rows, sl] = (_rms(res[:, sl], qn_ref[...]) * HEAD_DIM ** -0.5).astype(bf16)

    def k_norm(res, rows):
        for sl in heads:
            y = _rms(res[:, sl], kn_ref[...])
            k32_ref[rows, sl] = y
            p_ref[rows, sl] = y.astype(bf16)

    def value(res, rows):
        v32_ref[rows, :] = res
        p_ref[rows, :] = res.astype(bf16)

    def idx_kw(res, rows):
        ikw_ref[rows, :] = res[:, :LANES]
        p_ref[rows, :] = res.astype(bf16)

    segment(j < base, gates)
    segment((j >= base + T_RQ) & (j < base + T_RV), rotary)
    segment(((j >= base + T_RV) & (j < base + T_RG)) | ((j >= base + T_IQ) & (j < base + T_IKW)), plain)
    segment((j >= base + T_RG) & (j < base + T_AQ), silu)
    segment((j >= base + T_AQ) & (j < base + T_AK), q_norm)
    segment(j == base + T_AK, k_norm)
    segment(j == base + T_AV, value)
    segment(j == base + T_IKW, idx_kw)


def _proj(x, gain, w_t, cos2, sin2, qn, kn, tm):
    m, d = x.shape
    gate_row0 = MAIN_W + IDX_DIM + IDX_HEADS
    base = (w_t.shape[0] - gate_row0) // TN
    nt = base + T_END
    tab_blocks = cos2.shape[0] // tm
    kern = functools.partial(_proj_kernel, base=base)
    assert gate_row0 % 8 == 0
    w_row = lambda j: 8 * jnp.where(j < base, gate_row0 // 8 + j * (TN // 8), (j - base) * (TN // 8))
    return pl.pallas_call(
        kern,
        grid=(m // tm, nt),
        in_specs=[
            pl.BlockSpec((tm, d), lambda i, j: (i, 0)),
            pl.BlockSpec((1, d), lambda i, j: (0, 0)),
            pl.BlockSpec((pl.Element(TN), pl.Element(d)), lambda i, j: (w_row(j), 0)),
            pl.BlockSpec((tm, LANES), lambda i, j: (i % tab_blocks, 0)),
            pl.BlockSpec((tm, LANES), lambda i, j: (i % tab_blocks, 0)),
            pl.BlockSpec((1, LANES), lambda i, j: (0, 0)),
            pl.BlockSpec((1, LANES), lambda i, j: (0, 0)),
        ],
        out_specs=[
            pl.BlockSpec((tm, TN), lambda i, j: (i, j)),
            pl.BlockSpec((tm, TN), lambda i, j: (i, 0)),
            pl.BlockSpec((tm, TN), lambda i, j: (i, 0)),
            pl.BlockSpec((tm, LANES), lambda i, j: (i, 0)),
        ],
        out_shape=[
            jax.ShapeDtypeStruct((m, nt * TN), bf16),
            jax.ShapeDtypeStruct((m, ATT_KV_W), f32),
            jax.ShapeDtypeStruct((m, ATT_KV_W), f32),
            jax.ShapeDtypeStruct((m, LANES), f32),
        ],
        scratch_shapes=[pltpu.VMEM((tm, d), bf16)],
        compiler_params=_params(("arbitrary", "arbitrary")),
    )(x, gain, w_t, cos2, sin2, qn, kn)


def _ret_kernel(lg_ref, q_ref, k_ref, v_ref, sg_ref, gn_ref, og_ref, st_ref, state_ref, *, chunk):
    h = pl.program_id(1)
    c = pl.program_id(2)
    lg = lg_ref[h]

    @pl.when(c == 0)
    def _():
        state_ref[...] = jnp.zeros_like(state_ref)

    q, k, v = q_ref[...], k_ref[...], v_ref[...]
    ii = lax.broadcasted_iota(jnp.int32, (chunk, chunk), 0)
    jj = lax.broadcasted_iota(jnp.int32, (chunk, chunk), 1)
    rel = (ii - jj).astype(f32)
    dmat = jnp.where(rel >= 0, jnp.exp(lg * jnp.maximum(rel, 0.0)), 0.0)
    scores = _dot_nt(q, k) * dmat
    inner = _dot(scores.astype(bf16), v)
    i1 = lax.broadcasted_iota(jnp.int32, (chunk, 1), 0).astype(f32)
    q_dec = jnp.exp(lg * (i1 + 1.0))
    k_dec = jnp.exp(lg * (chunk - 1.0 - i1))
    state = state_ref[...]
    cross = _dot((q.astype(f32) * q_dec).astype(bf16), state.astype(bf16))
    kd = (k.astype(f32) * k_dec).astype(bf16)
    new_state = jnp.exp(lg * chunk) * state + lax.dot_general(
        kd, v, (((0,), (0,)), ((), ())), preferred_element_type=f32)
    state_ref[...] = new_state
    o = _rms(inner + cross, gn_ref[...])
    og_ref[...] = (o * sg_ref[...].astype(f32)).astype(bf16)

    @pl.when(c == pl.num_programs(2) - 1)
    def _():
        st_ref[0, 0] = new_state


def _retention(p, log_g, gn, batch, seq, base, chunk):
    nc = seq // chunk
    qb = (base + T_RQ) * (TN // RET_QK_DIM)
    kb = (base + T_RK) * (TN // RET_QK_DIM)
    vb = (base + T_RV) * (TN // RET_V_DIM)
    gb = (base + T_RG) * (TN // RET_V_DIM)
    kern = functools.partial(_ret_kernel, chunk=chunk)
    grid_spec = pltpu.PrefetchScalarGridSpec(
        num_scalar_prefetch=1,
        grid=(batch, RET_HEADS, nc),
        in_specs=[
            pl.BlockSpec((chunk, RET_QK_DIM), lambda b, h, c, lg: (b * nc + c, qb + h)),
            pl.BlockSpec((chunk, RET_QK_DIM), lambda b, h, c, lg: (b * nc + c, kb + h)),
            pl.BlockSpec((chunk, RET_V_DIM), lambda b, h, c, lg: (b * nc + c, vb + h)),
            pl.BlockSpec((chunk, RET_V_DIM), lambda b, h, c, lg: (b * nc + c, gb + h)),
            pl.BlockSpec((1, RET_V_DIM), lambda b, h, c, lg: (0, 0)),
        ],
        out_specs=[
            pl.BlockSpec((chunk, RET_V_DIM), lambda b, h, c, lg: (b * nc + c, h)),
            pl.BlockSpec((1, 1, RET_QK_DIM, RET_V_DIM), lambda b, h, c, lg: (b, h, 0, 0)),
        ],
        scratch_shapes=[pltpu.VMEM((RET_QK_DIM, RET_V_DIM), f32)],
    )
    return pl.pallas_call(
        kern,
        grid_spec=grid_spec,
        out_shape=[
            jax.ShapeDtypeStruct((batch * seq, RET_V_W), bf16),
            jax.ShapeDtypeStruct((batch, RET_HEADS, RET_QK_DIM, RET_V_DIM), f32),
        ],
        compiler_params=_params(("arbitrary", "arbitrary", "arbitrary")),
    )(log_g, p, p, p, p, gn)


def _sortable_key(score):
    bits = lax.bitcast_convert_type(score, jnp.int32)
    return jnp.where(bits >= 0, bits, bits ^ jnp.int32(0x7FFFFFFF))


def _kth_largest_key(count_ge, n_keep, shape):
    def body(it, t_u):
        bit = jnp.left_shift(jnp.int32(1), 31 - it)
        cand_u = t_u | bit
        cnt = count_ge(cand_u ^ jnp.int32(INT_MIN))
        return jnp.where(cnt >= n_keep, cand_u, t_u)
    t_u = lax.fori_loop(0, 32, body, jnp.zeros(shape, jnp.int32))
    return t_u ^ jnp.int32(INT_MIN)


def _attn_kernel(aq_ref, k_ref, v_ref, iq_ref, ikw_k_ref, ikw_q_ref, o_ref,
                 key_ref, iklo_ref, ikhi_ref, vt_ref, m_ref, l_ref, acc_ref, *, tq, ck, seq, n_keep):
    qb = pl.program_id(1)

    @pl.when(qb == 0)
    def _():
        ikw = ikw_k_ref[...]
        lane = lax.broadcasted_iota(jnp.int32, (seq, LANES), 1)
        iklo_ref[...] = jnp.where(lane < IDX_DIM, ikw, 0.0).astype(bf16)
        ikhi_ref[...] = jnp.where(lane >= IDX_DIM, pltpu.roll(ikw, IDX_DIM, 1), 0.0).astype(bf16)

        for c0 in range(0, seq, ck):
            vt_ref[:, c0:c0 + ck] = jnp.transpose(v_ref[c0:c0 + ck, :].astype(f32)).astype(bf16)

    n_chunks = (qb * tq + tq - 1) // ck + 1
    chunk = lambda kc: pl.ds(pl.multiple_of(kc * ck, ck), ck)
    stack = 4
    pairs = IDX_HEADS // 2

    iw_t = jnp.transpose(ikw_q_ref[...]) * IDX_W_SCALE
    q_stacks = [jnp.concatenate([iq_ref[:, hp * LANES:(hp + 1) * LANES] for hp in range(h0, h0 + stack)], axis=0)
                for h0 in range(0, pairs, stack)]
    k_off = lax.broadcasted_iota(jnp.int32, (ck, tq), 0)
    q_pos = qb * tq + lax.broadcasted_iota(jnp.int32, (ck, tq), 1)

    def score_body(kc, c):
        scores = jnp.zeros((ck, tq), f32)
        for si, qs in enumerate(q_stacks):
            for sub, ik_ref in enumerate((iklo_ref, ikhi_ref)):
                s = jnp.maximum(_dot_nt(ik_ref[chunk(kc), :], qs), 0.0)
                for i in range(stack):
                    row = IDX_DIM + 2 * (si * stack + i) + sub
                    scores = scores + s[:, i * tq:(i + 1) * tq] * iw_t[row:row + 1, :]
        visible = kc * ck + k_off <= q_pos
        key_ref[chunk(kc), :] = jnp.where(visible, _sortable_key(scores), jnp.int32(INT_MIN))
        return c
    lax.fori_loop(0, n_chunks, score_body, 0)

    def count_ge(t):
        t_b = jnp.broadcast_to(t, (8, tq))

        def body(kc, acc):
            keys = key_ref[chunk(kc), :].reshape(ck // 8, 8, tq)
            return acc + jnp.sum(jnp.where(keys >= t_b, 1.0, 0.0), axis=0)
        acc = lax.fori_loop(0, n_chunks, body, jnp.zeros((8, tq), f32))
        return jnp.sum(acc, axis=0, keepdims=True)

    thr = jnp.maximum(_kth_largest_key(count_ge, n_keep, (1, tq)), jnp.int32(INT_MIN + 1))

    m_ref[...] = jnp.full_like(m_ref, NEG_BIG)
    l_ref[...] = jnp.zeros_like(l_ref)
    acc_ref[...] = jnp.zeros_like(acc_ref)
    q_groups = [jnp.concatenate([aq_ref[:, hd * HEAD_DIM:(hd + 1) * HEAD_DIM]
                                 for hd in range(g * GROUP, (g + 1) * GROUP)], axis=0)
                for g in range(ATT_KV_HEADS)]

    def attn_body(kc, c):
        bias = jnp.where(key_ref[chunk(kc), :] >= thr, 0.0, -jnp.inf)
        bias = jnp.concatenate([bias] * GROUP, axis=1)
        for g in range(ATT_KV_HEADS):
            cols = slice(g * GROUP * tq, (g + 1) * GROUP * tq)
            kg = k_ref[chunk(kc), g * HEAD_DIM:(g + 1) * HEAD_DIM]
            s = _dot_nt(kg, q_groups[g]) + bias
            m_old = m_ref[:, cols]
            m_new = jnp.maximum(m_old, jnp.max(s, axis=0, keepdims=True))
            p = jnp.exp(s - m_new)
            alpha = jnp.exp(m_old - m_new)
            l_ref[:, cols] = alpha * l_ref[:, cols] + jnp.sum(p, axis=0, keepdims=True)
            vt = vt_ref[g * HEAD_DIM:(g + 1) * HEAD_DIM, chunk(kc)]
            acc_ref[:, cols] = alpha * acc_ref[:, cols] + _dot(vt, p.astype(bf16))
            m_ref[:, cols] = m_new
        return c
    lax.fori_loop(0, n_chunks, attn_body, 0)

    for hd in range(ATT_HEADS):
        cols = slice(hd * tq, (hd + 1) * tq)
        o_t = acc_ref[:, cols] / l_ref[:, cols]
        o_ref[:, hd * HEAD_DIM:(hd + 1) * HEAD_DIM] = jnp.transpose(o_t).astype(bf16)


def _prompt_attention(p, ikw, batch, seq, base, tq, n_keep):
    nq = seq // tq
    ck = tq
    kern = functools.partial(_attn_kernel, tq=tq, ck=ck, seq=seq, n_keep=n_keep)
    aq_blk = (base + T_AQ) * TN // ATT_Q_W
    iq_blk = (base + T_IQ) * TN // IDX_Q_W
    return pl.pallas_call(
        kern,
        grid=(batch, nq),
        in_specs=[
            pl.BlockSpec((tq, ATT_Q_W), lambda b, t: (b * nq + t, aq_blk)),
            pl.BlockSpec((seq, ATT_KV_W), lambda b, t: (b, base + T_AK)),
            pl.BlockSpec((seq, ATT_KV_W), lambda b, t: (b, base + T_AV)),
            pl.BlockSpec((tq, IDX_Q_W), lambda b, t: (b * nq + t, iq_blk)),
            pl.BlockSpec((seq, LANES), lambda b, t: (b, 0)),
            pl.BlockSpec((tq, LANES), lambda b, t: (b * nq + t, 0)),
        ],
        out_specs=pl.BlockSpec((tq, ATT_Q_W), lambda b, t: (b * nq + t, 0)),
        out_shape=jax.ShapeDtypeStruct((batch * seq, ATT_Q_W), bf16),
        scratch_shapes=[pltpu.VMEM((seq, tq), jnp.int32), pltpu.VMEM((seq, LANES), bf16),
                        pltpu.VMEM((seq, LANES), bf16), pltpu.VMEM((ATT_KV_W, seq), bf16),
                        pltpu.VMEM((1, ATT_HEADS * tq), f32), pltpu.VMEM((1, ATT_HEADS * tq), f32),
                        pltpu.VMEM((HEAD_DIM, ATT_HEADS * tq), f32)],
        compiler_params=_params(("arbitrary", "arbitrary")),
    )(p, p, p, p, ikw, ikw)


def _oproj_kernel(og_ref, att_ref, gr_ref, ga_ref, x_ref, wr_ref, wa_ref, wo_ref, o_ref):
    ret_out = _dot(og_ref[...], wr_ref[...])
    att_out = _dot(att_ref[...], wa_ref[...])
    merged = gr_ref[...].astype(f32) * ret_out + ga_ref[...].astype(f32) * att_out
    o_ref[...] = x_ref[...] + _dot(merged.astype(bf16), wo_ref[...])


def _oproj(og, att, p, x, wr, wa, wo, tm):
    m, d = x.shape
    const = lambda shape: pl.BlockSpec(shape, lambda i: (0, 0), pipeline_mode=pl.Buffered(1))
    return pl.pallas_call(
        _oproj_kernel,
        grid=(m // tm,),
        in_specs=[
            pl.BlockSpec((tm, RET_V_W), lambda i: (i, 0)),
            pl.BlockSpec((tm, ATT_Q_W), lambda i: (i, 0)),
            pl.BlockSpec((tm, d), lambda i: (i, 0)),
            pl.BlockSpec((tm, d), lambda i: (i, 1)),
            pl.BlockSpec((tm, d), lambda i: (i, 0)),
            const(wr.shape), const(wa.shape), const(wo.shape),
        ],
        out_specs=pl.BlockSpec((tm, d), lambda i: (i, 0)),
        out_shape=jax.ShapeDtypeStruct((m, d), f32),
        compiler_params=_params(("arbitrary",)),
    )(og, att, p, p, x, wr, wa, wo)


def _dec_ret_kernel(lg_ref, q_ref, k_ref, v_ref, sg_ref, gn_ref, st_ref, og_ref, nst_ref):
    for h in range(RET_HEADS):
        lg = lg_ref[h]
        gamma = jnp.exp(lg)
        q = q_ref[0, :, h * RET_QK_DIM:(h + 1) * RET_QK_DIM]
        k = k_ref[0, :, h * RET_QK_DIM:(h + 1) * RET_QK_DIM]
        v = v_ref[0, :, h * RET_V_DIM:(h + 1) * RET_V_DIM]
        sg = sg_ref[0, :, h * RET_V_DIM:(h + 1) * RET_V_DIM]
        state = st_ref[0, 0, h]
        qk = jnp.sum(q * k, axis=1, keepdims=True)
        q8 = jnp.broadcast_to((q * gamma).astype(bf16), (8, RET_QK_DIM))
        cross = _dot(q8, state.astype(bf16))[0:1]
        k_col = jnp.transpose(jnp.broadcast_to(k, (8, RET_QK_DIM)))[:, 0:1]
        nst_ref[0, h] = gamma * state + k_col * v
        o = _rms(qk * v + cross, gn_ref[...])
        og_ref[0, :, h * RET_V_DIM:(h + 1) * RET_V_DIM] = o * sg


def _decode_retention(q, k, v, sg, gn, state, log_g):
    db = q.shape[0]
    row = lambda w: pl.BlockSpec((1, 1, w), lambda b, lg: (b, 0, 0))
    grid_spec = pltpu.PrefetchScalarGridSpec(
        num_scalar_prefetch=1,
        grid=(db,),
        in_specs=[row(RET_QK_W), row(RET_QK_W), row(RET_V_W), row(RET_V_W),
                  pl.BlockSpec((1, RET_V_DIM), lambda b, lg: (0, 0)),
                  pl.BlockSpec((1, 1, RET_HEADS, RET_QK_DIM, RET_V_DIM), lambda b, lg: (0, b, 0, 0, 0))],
        out_specs=[row(RET_V_W),
                   pl.BlockSpec((1, RET_HEADS, RET_QK_DIM, RET_V_DIM), lambda b, lg: (b, 0, 0, 0))],
    )
    return pl.pallas_call(
        _dec_ret_kernel,
        grid_spec=grid_spec,
        out_shape=[jax.ShapeDtypeStruct((db, 1, RET_V_W), f32),
                   jax.ShapeDtypeStruct((db, RET_HEADS, RET_QK_DIM, RET_V_DIM), f32)],
        compiler_params=_params(("arbitrary",)),
    )(log_g, q, k, v, sg, gn, state)


def _dec_score_kernel(pt_ref, qi_ref, w_ref, pool_ref, s_ref, buf_ref, sem_ref, *, n_pages):
    b = pl.program_id(0)
    slot = b % 2

    def page_copy(seq, p, slot_):
        dst = buf_ref.at[slot_, :, pl.ds(pl.multiple_of(p * PAGE_SIZE, PAGE_SIZE), PAGE_SIZE)]
        return pltpu.make_async_copy(pool_ref.at[pt_ref[seq, p]], dst, sem_ref.at[slot_])

    def start_all(seq, slot_):
        def body(p, c):
            page_copy(seq, p, slot_).start()
            return c
        lax.fori_loop(0, n_pages, body, 0)

    @pl.when(b == 0)
    def _():
        start_all(0, 0)

    @pl.when(b + 1 < pl.num_programs(0))
    def _():
        start_all(b + 1, 1 - slot)

    def wait_body(p, c):
        page_copy(b, p, slot).wait()
        return c
    lax.fori_loop(0, n_pages, wait_body, 0)

    s = jnp.maximum(_dot(qi_ref[0], buf_ref[slot].astype(bf16)), 0.0)
    row = jnp.sum(s * w_ref[0], axis=0, keepdims=True)
    for p in range(n_pages):
        s_ref[0, p:p + 1, :] = row[:, p * PAGE_SIZE:(p + 1) * PAGE_SIZE]


def _decode_scores(page_table, qi, w, pool_ikt):
    db, n_pages = page_table.shape
    kern = functools.partial(_dec_score_kernel, n_pages=n_pages)
    grid_spec = pltpu.PrefetchScalarGridSpec(
        num_scalar_prefetch=1,
        grid=(db,),
        in_specs=[
            pl.BlockSpec((1, IDX_HEADS, IDX_DIM), lambda b, pt: (b, 0, 0)),
            pl.BlockSpec((1, IDX_HEADS, 1), lambda b, pt: (b, 0, 0)),
            pl.BlockSpec(memory_space=pl.ANY),
        ],
        out_specs=pl.BlockSpec((1, n_pages, PAGE_SIZE), lambda b, pt: (b, 0, 0)),
        scratch_shapes=[pltpu.VMEM((2, IDX_DIM, n_pages * PAGE_SIZE), f32), pltpu.SemaphoreType.DMA((2,))],
    )
    return pl.pallas_call(
        kern,
        grid_spec=grid_spec,
        out_shape=jax.ShapeDtypeStruct((db, n_pages, PAGE_SIZE), f32),
        compiler_params=_params(("arbitrary",)),
    )(page_table, qi, w, pool_ikt)


def _dec_select_kernel(s_ref, qi_ref, w_ref, ikn_ref, idx_ref, cnt_ref, nsel_ref, rank_ref,
                       *, n_keep, kmax):
    db, n_pages, _ = s_ref.shape
    ikn = ikn_ref[...].astype(bf16).astype(f32)
    dots = jnp.sum(qi_ref[...].astype(f32) * ikn, axis=2, keepdims=True)
    s_new = jnp.sum(jnp.maximum(dots, 0.0) * w_ref[...], axis=1, keepdims=True)
    keys = _sortable_key(s_ref[...])
    key_new = _sortable_key(s_new)

    def count_ge(t):
        c = jnp.sum(jnp.where(keys >= t, 1.0, 0.0), axis=1, keepdims=True)
        return jnp.sum(c, axis=2, keepdims=True) + jnp.where(key_new >= t, 1.0, 0.0)

    thr = _kth_largest_key(count_ge, n_keep, (db, 1, 1))
    sel = jnp.where(keys >= thr, 1.0, 0.0)
    nsel_ref[...] = jnp.broadcast_to(jnp.where(key_new >= thr, 1.0, 0.0), (db, 1, LANES))

    pp = lax.broadcasted_iota(jnp.int32, (n_pages, n_pages), 0)
    pq = lax.broadcasted_iota(jnp.int32, (n_pages, n_pages), 1)
    earlier_page = jnp.where(pq < pp, 1.0, 0.0).astype(bf16)
    oo = lax.broadcasted_iota(jnp.int32, (PAGE_SIZE, PAGE_SIZE), 0)
    oq = lax.broadcasted_iota(jnp.int32, (PAGE_SIZE, PAGE_SIZE), 1)
    earlier_off = jnp.where(oo < oq, 1.0, 0.0).astype(bf16)
    r_iota = lax.broadcasted_iota(jnp.int32, (kmax, PAGE_SIZE), 0).astype(f32)
    lane = lax.broadcasted_iota(jnp.int32, (kmax, PAGE_SIZE), 1).astype(f32)
    for b in range(db):
        sb = sel[b]
        per_page = jnp.sum(sb, axis=1, keepdims=True)
        before = _dot(earlier_page, jnp.broadcast_to(per_page, (n_pages, PAGE_SIZE)).astype(bf16))
        within = _dot(sb.astype(bf16), earlier_off)
        rank_ref[b] = jnp.where(sb > 0.0, before + within, -1.0)

        def body(p, acc, b=b):
            row = rank_ref[b, pl.ds(p, 1), :]
            return jnp.where(row == r_iota, lane + jnp.asarray(p * PAGE_SIZE, f32), acc)

        acc = lax.fori_loop(0, n_pages, body, jnp.zeros((kmax, PAGE_SIZE), f32))
        idx_ref[b] = jnp.sum(acc, axis=1, keepdims=True).astype(jnp.int32)
        total = jnp.sum(per_page, axis=0, keepdims=True)
        cnt_ref[b] = jnp.broadcast_to(total, (1, LANES)).astype(jnp.int32)


def _decode_select(scores, qi, w, ik_new, n_keep):
    db, n_pages, _ = scores.shape
    kmax = -(-n_keep // 8) * 8
    kern = functools.partial(_dec_select_kernel, n_keep=n_keep, kmax=kmax)
    full = lambda shape: pl.BlockSpec(shape, lambda i: (0,) * len(shape))
    return pl.pallas_call(
        kern,
        grid=(1,),
        in_specs=[full(scores.shape), full(qi.shape), full(w.shape), full(ik_new.shape)],
        out_specs=[full((db, kmax, 1)), full((db, 1, LANES)), full((db, 1, LANES))],
        out_shape=[jax.ShapeDtypeStruct((db, kmax, 1), jnp.int32),
                   jax.ShapeDtypeStruct((db, 1, LANES), jnp.int32),
                   jax.ShapeDtypeStruct((db, 1, LANES), f32)],
        scratch_shapes=[pltpu.VMEM((db, n_pages, PAGE_SIZE), f32)],
        compiler_params=_params(("arbitrary",)),
    )(scores, qi, w, ik_new)


def _dec_attn_kernel(idx_ref, pt_ref, cnt_ref, q_ref, kn_ref, vn_ref, nsel_ref, kpool_ref, vpool_ref, o_ref,
                     kbuf_ref, vbuf_ref, sem_ref, *, kmax):
    b = pl.program_id(0)
    slot = b % 2
    page_shift = PAGE_SIZE.bit_length() - 1

    def row_copies(seq, r, slot_):
        pos = idx_ref[seq, r]
        phys = pt_ref[seq, lax.shift_right_logical(pos, page_shift)]
        src0 = pl.multiple_of((pos & (PAGE_SIZE - 1)) * ATT_KV_HEADS, ATT_KV_HEADS)
        dst0 = pl.multiple_of(r * ATT_KV_HEADS, ATT_KV_HEADS)
        return (pltpu.make_async_copy(kpool_ref.at[phys, pl.ds(src0, ATT_KV_HEADS), :],
                                      kbuf_ref.at[slot_, pl.ds(dst0, ATT_KV_HEADS), :], sem_ref.at[0, slot_]),
                pltpu.make_async_copy(vpool_ref.at[phys, pl.ds(src0, ATT_KV_HEADS), :],
                                      vbuf_ref.at[slot_, pl.ds(dst0, ATT_KV_HEADS), :], sem_ref.at[1, slot_]))

    def start_all(seq, slot_):
        def body(r, c):
            ck, cv = row_copies(seq, r, slot_)
            ck.start()
            cv.start()
            return c
        lax.fori_loop(0, kmax, body, 0)

    @pl.when(b == 0)
    def _():
        start_all(0, 0)

    @pl.when(b + 1 < pl.num_programs(0))
    def _():
        start_all(b + 1, 1 - slot)

    def wait_body(r, c):
        ck, cv = row_copies(b, r, slot)
        ck.wait()
        cv.wait()
        return c
    lax.fori_loop(0, kmax, wait_body, 0)

    q = q_ref[0]
    s = _dot_nt(q, kbuf_ref[slot].astype(bf16))
    col = lax.broadcasted_iota(jnp.int32, s.shape, 1)
    head = lax.broadcasted_iota(jnp.int32, s.shape, 0)
    ok = (col % ATT_KV_HEADS == head // GROUP) & (col // ATT_KV_HEADS < cnt_ref[b])
    s = jnp.where(ok, s, -jnp.inf)
    s_n = jnp.sum(q.astype(f32) * kn_ref[0].astype(f32), axis=1, keepdims=True)
    s_n = jnp.where(nsel_ref[0][:, 0:1] > 0.0, s_n, -jnp.inf)
    m = jnp.maximum(jnp.max(s, axis=1, keepdims=True), s_n)
    p = jnp.exp(s - m)
    p_n = jnp.exp(s_n - m)
    l = jnp.sum(p, axis=1, keepdims=True) + p_n
    pv = _dot(p.astype(bf16), vbuf_ref[slot].astype(bf16))
    o_ref[0] = (pv + p_n.astype(bf16).astype(f32) * vn_ref[0].astype(f32)) / l


def _decode_attention(idx, page_table, cnt, q, k_new, v_new, nsel, pool_k, pool_v):
    db, kmax = idx.shape
    kern = functools.partial(_dec_attn_kernel, kmax=kmax)
    seqb = lambda shape: pl.BlockSpec(shape, lambda b, *_: (b, 0, 0))
    grid_spec = pltpu.PrefetchScalarGridSpec(
        num_scalar_prefetch=3,
        grid=(db,),
        in_specs=[
            seqb((1, ATT_HEADS, HEAD_DIM)),
            seqb((1, ATT_HEADS, HEAD_DIM)),
            seqb((1, ATT_HEADS, HEAD_DIM)),
            seqb((1, 1, LANES)),
            pl.BlockSpec(memory_space=pl.ANY),
            pl.BlockSpec(memory_space=pl.ANY),
        ],
        out_specs=seqb((1, ATT_HEADS, HEAD_DIM)),
        scratch_shapes=[pltpu.VMEM((2, kmax * ATT_KV_HEADS, HEAD_DIM), f32),
                        pltpu.VMEM((2, kmax * ATT_KV_HEADS, HEAD_DIM), f32),
                        pltpu.SemaphoreType.DMA((2, 2))],
    )
    return pl.pallas_call(
        kern,
        grid_spec=grid_spec,
        out_shape=jax.ShapeDtypeStruct((db, ATT_HEADS, HEAD_DIM), f32),
        compiler_params=_params(("arbitrary",)),
    )(idx, page_table, cnt, q, k_new, v_new, nsel, pool_k, pool_v)


def _rope_tables(pos):
    half = RET_QK_DIM // 2
    inv = ROPE_BASE ** (-jnp.arange(half, dtype=f32) / half)
    ang = pos.astype(f32)[:, None] * inv[None, :]
    cos, sin = jnp.cos(ang), jnp.sin(ang)
    return jnp.concatenate([cos, cos], axis=1), jnp.concatenate([-sin, sin], axis=1)


def _pick(n, pref):
    for t in pref:
        if n % t == 0:
            return t
    return n


def kernel(x_prompt, x_sample, state_ret, cache_k, cache_v, cache_idx_k, page_table, ffn1_norm, ffn1_w1, ffn1_w2, mix_norm, w_in, q_norm, k_norm, ret_norm, w_ret_out, w_att_out, w_o, ffn2_norm, ffn2_w1, ffn2_w2):
    batch, seq, d = x_prompt.shape
    db, ts, _ = x_sample.shape
    depth = w_in.shape[0]
    assert depth == 1 and ts == 1 and d % (2 * TN) == 0
    n_pages = page_table.shape[1]
    past = n_pages * PAGE_SIZE
    n_phys = cache_k.shape[1]
    base = 2 * (d // TN)
    m = batch * seq
    ms = 16

    log_g = jnp.log1p(-jnp.exp2(-5.0 - jnp.arange(RET_HEADS, dtype=f32)))
    w1a, w2a = ffn1_w1[0], ffn1_w2[0]
    w1b, w2b = ffn2_w1[0], ffn2_w2[0]
    w_pad = jnp.swapaxes(w_in[0], 0, 1)
    wr, wa, wo = w_ret_out[0].astype(bf16), w_att_out[0].astype(bf16), w_o[0].astype(bf16)
    g1, g2, g3 = ffn1_norm, mix_norm, ffn2_norm
    qn, kn, gn = q_norm, k_norm, ret_norm

    tm = _pick(m, (1024, 512, 256, 128))
    tf = _pick(ffn1_w2.shape[1], (256, 128))
    tmp = _pick(seq, (1024, 512, 256, 128))
    chunk = _pick(seq, (256, 128))
    tq = _pick(seq, (256, 128))
    tmo = _pick(m, (256, 128))

    xp = x_prompt.reshape(m, d)
    x1 = _ffn(xp, g1, w1a, w2a, tm, tf)
    cos_p, sin_p = _rope_tables(jnp.arange(seq))
    p, k32, v32, ikw = _proj(x1, g2, w_pad, cos_p, sin_p, qn, kn, tmp)
    og, st_p = _retention(p, log_g, gn, batch, seq, base, chunk)
    att = _prompt_attention(p, ikw, batch, seq, base, tq, min(TOPK_MAX, seq // 4))
    x2 = _oproj(og, att, p, x1, wr, wa, wo, tmo)
    yp = _ffn(x2, g3, w1b, w2b, tm, tf).reshape(batch, seq, d)

    xs = jnp.zeros((ms, d), f32).at[:db].set(x_sample.reshape(db, d))
    s1 = _ffn(xs, g1, w1a, w2a, ms, tf)
    cos_s, sin_s = _rope_tables(jnp.full((ms,), past, jnp.int32))
    ps, k32s, v32s, ikws = _proj(s1, g2, w_pad, cos_s, sin_s, qn, kn, ms)
    seg = lambda t0, t1: ps[:db, (base + t0) * TN:(base + t1) * TN]
    row3 = lambda a: a.astype(f32).reshape(db, 1, a.shape[-1])
    og_s, st_s = _decode_retention(row3(seg(T_RQ, T_RK)), row3(seg(T_RK, T_RV)), row3(seg(T_RV, T_RG)),
                                   row3(seg(T_RG, T_AQ)), gn, state_ret, log_g)
    qi = seg(T_IQ, T_IKW).reshape(db, IDX_HEADS, IDX_DIM)
    wi = (ikws[:db, IDX_DIM:IDX_DIM + IDX_HEADS] * IDX_W_SCALE).reshape(db, IDX_HEADS, 1)
    ik_new = ikws[:db, :IDX_DIM].reshape(db, 1, IDX_DIM)
    pool_ikt = jnp.swapaxes(cache_idx_k.reshape(n_phys, PAGE_SIZE, IDX_DIM), 1, 2)
    scores = _decode_scores(page_table, qi, wi, pool_ikt)
    idx, cnt, nsel = _decode_select(scores, qi, wi, ik_new, min(TOPK_MAX, (past + ts) // 4))
    aq_s = seg(T_AQ, T_AK).reshape(db, ATT_HEADS, HEAD_DIM)
    expand = lambda a: jnp.repeat(a[:db].reshape(db, ATT_KV_HEADS, HEAD_DIM), GROUP, axis=1).astype(bf16)
    pool_rows = lambda c: c.reshape(n_phys, PAGE_SIZE * ATT_KV_HEADS, HEAD_DIM)
    att_s = _decode_attention(idx.reshape(db, -1), page_table, cnt[:, 0, 0], aq_s, expand(k32s), expand(v32s),
                              nsel, pool_rows(cache_k), pool_rows(cache_v))
    pad_rows = lambda a: jnp.pad(a.reshape(db, -1).astype(bf16), ((0, ms - db), (0, 0)))
    s2 = _oproj(pad_rows(og_s), pad_rows(att_s), ps, s1, wr, wa, wo, ms)
    ys = _ffn(s2, g3, w1b, w2b, ms, tf)[:db].reshape(db, ts, d)

    return (yp, ys,
            st_p[None],
            k32.reshape(1, batch, seq, ATT_KV_HEADS, HEAD_DIM),
            v32.reshape(1, batch, seq, ATT_KV_HEADS, HEAD_DIM),
            ikw[:, :IDX_DIM].reshape(1, batch, seq, IDX_DIM),
            st_s[None],
            k32s[:db].reshape(1, db, ts, ATT_KV_HEADS, HEAD_DIM),
            v32s[:db].reshape(1, db, ts, ATT_KV_HEADS, HEAD_DIM),
            ikws[:db, :IDX_DIM].reshape(1, db, ts, IDX_DIM))
```

```python
import functools

import jax
import jax.numpy as jnp
from jax import lax
from jax.experimental import pallas as pl
from jax.experimental.pallas import tpu as pltpu

RET_HEADS = 8
RET_QK_DIM = 128
RET_V_DIM = 256
ATT_HEADS = 16
ATT_KV_HEADS = 4
HEAD_DIM = 128
GROUP = ATT_HEADS // ATT_KV_HEADS
IDX_HEADS = 16
IDX_DIM = 64
IDX_W_SCALE = (IDX_HEADS ** -0.5) * (IDX_DIM ** -0.5)
TOPK_MAX = 256
PAGE_SIZE = 128
ROPE_BASE = 10000.0
EPS = 1e-6

RET_QK_W = RET_HEADS * RET_QK_DIM
RET_V_W = RET_HEADS * RET_V_DIM
ATT_Q_W = ATT_HEADS * HEAD_DIM
ATT_KV_W = ATT_KV_HEADS * HEAD_DIM
IDX_Q_W = IDX_HEADS * IDX_DIM
MAIN_W = 2 * RET_QK_W + 2 * RET_V_W + ATT_Q_W + 2 * ATT_KV_W + IDX_Q_W

LANES = 128
TN = 512
T_RQ, T_RK, T_RV, T_RG, T_AQ, T_AK, T_AV, T_IQ, T_IKW, T_END = 0, 2, 4, 8, 12, 16, 17, 18, 20, 21
VMEM_LIMIT = 56 * 1024 * 1024
INT_MIN = -2 ** 31
NEG_BIG = -1e30
LOG2E = 1.4426950408889634
VT_PAD = 16

bf16 = jnp.bfloat16
f32 = jnp.float32


def _params(sem, vmem=VMEM_LIMIT):
    return pltpu.CompilerParams(dimension_semantics=sem, vmem_limit_bytes=vmem)


def _sigmoid(x):
    return 1.0 / (1.0 + jnp.exp(-x))


def _dot(a, b):
    return jnp.dot(a, b, preferred_element_type=f32)


def _dot_nt(a, b):
    return lax.dot_general(a, b, (((1,), (1,)), ((), ())), preferred_element_type=f32)


def _rms(x, gain):
    ms = jnp.mean(x * x, axis=-1, keepdims=True)
    return x * lax.rsqrt(ms + EPS) * gain


def _ffn_kernel(x_ref, g_ref, w1g_ref, w1u_ref, w2_ref, o_ref, h_ref):
    j = pl.program_id(1)

    @pl.when(j == 0)
    def _():
        h_ref[...] = _rms(x_ref[...], g_ref[...]).astype(bf16)
        o_ref[...] = jnp.zeros_like(o_ref)

    h = h_ref[...]
    gate = _dot(h, w1g_ref[...].astype(bf16))
    up = _dot(h, w1u_ref[...].astype(bf16))
    act = (gate * _sigmoid(gate) * up).astype(bf16)
    o_ref[...] += _dot(act, w2_ref[...].astype(bf16))

    @pl.when(j == pl.num_programs(1) - 1)
    def _():
        o_ref[...] = x_ref[...] + 0.5 * o_ref[...]


def _ffn(x, gain, w1, w2, tm, tf):
    m, d = x.shape
    dff = w2.shape[0]
    nf = dff // tf
    return pl.pallas_call(
        _ffn_kernel,
        grid=(m // tm, nf),
        in_specs=[
            pl.BlockSpec((tm, d), lambda i, j: (i, 0), pipeline_mode=pl.Buffered(1)),
            pl.BlockSpec((1, d), lambda i, j: (0, 0)),
            pl.BlockSpec((d, tf), lambda i, j: (0, j)),
            pl.BlockSpec((d, tf), lambda i, j: (0, j + nf)),
            pl.BlockSpec((tf, d), lambda i, j: (j, 0)),
        ],
        out_specs=pl.BlockSpec((tm, d), lambda i, j: (i, 0)),
        out_shape=jax.ShapeDtypeStruct((m, d), f32),
        scratch_shapes=[pltpu.VMEM((tm, d), bf16)],
        compiler_params=_params(("arbitrary", "arbitrary")),
    )(x, gain, w1, w1, w2)


def _proj_kernel(x_ref, g_ref, w_ref, cos_ref, sin_ref, qn_ref, kn_ref,
                 p_ref, k32_ref, v32_ref, ikw_ref, h_ref, *, base):
    j = pl.program_id(1)

    @pl.when(j == 0)
    def _():
        h_ref[...] = _rms(x_ref[...], g_ref[...]).astype(bf16)

    w = w_ref[...].astype(bf16)
    heads = [slice(c * LANES, (c + 1) * LANES) for c in range(TN // LANES)]
    tm = h_ref.shape[0]
    rc = min(tm, 256)

    def segment(cond, epilogue):
        @pl.when(cond)
        def _():
            for r0 in range(0, tm, rc):
                rows = slice(r0, r0 + rc)
                epilogue(_dot_nt(h_ref[rows, :], w), rows)

    def gates(res, rows):
        p_ref[rows, :] = _sigmoid(res).astype(bf16)

    def rotary(res, rows):
        cos, sin = cos_ref[rows, :], sin_ref[rows, :]
        scale = jnp.where(j >= base + T_RK, RET_QK_DIM ** -0.5, 1.0).astype(f32)
        for sl in heads:
            x = res[:, sl]
            p_ref[rows, sl] = ((x * cos + pltpu.roll(x, LANES // 2, 1) * sin) * scale).astype(bf16)

    def plain(res, rows):
        p_ref[rows, :] = res.astype(bf16)

    def silu(res, rows):
        p_ref[rows, :] = (res * _sigmoid(res)).astype(bf16)

    def q_norm(res, rows):
        for sl in heads:
            p_ref[rows, sl] = (_rms(res[:, sl], qn_ref[...]) * (HEAD_DIM ** -0.5 * LOG2E)).astype(bf16)

    def k_norm(res, rows):
        for sl in heads:
            y = _rms(res[:, sl], kn_ref[...])
            k32_ref[rows, sl] = y
            p_ref[rows, sl] = y.astype(bf16)

    def value(res, rows):
        v32_ref[rows, :] = res
        p_ref[rows, :] = res.astype(bf16)

    def idx_kw(res, rows):
        ikw_ref[rows, :] = res[:, :LANES]
        p_ref[rows, :] = res.astype(bf16)

    segment(j < base, gates)
    segment((j >= base + T_RQ) & (j < base + T_RV), rotary)
    segment(((j >= base + T_RV) & (j < base + T_RG)) | ((j >= base + T_IQ) & (j < base + T_IKW)), plain)
    segment((j >= base + T_RG) & (j < base + T_AQ), silu)
    segment((j >= base + T_AQ) & (j < base + T_AK), q_norm)
    segment(j == base + T_AK, k_norm)
    segment(j == base + T_AV, value)
    segment(j == base + T_IKW, idx_kw)


def _proj(x, gain, w_t, cos2, sin2, qn, kn, tm):
    m, d = x.shape
    gate_row0 = MAIN_W + IDX_DIM + IDX_HEADS
    base = (w_t.shape[0] - gate_row0) // TN
    nt = base + T_END
    tab_blocks = cos2.shape[0] // tm
    kern = functools.partial(_proj_kernel, base=base)
    al = 16
    assert gate_row0 % al == 0
    w_row = lambda j: al * jnp.where(j < base, gate_row0 // al + j * (TN // al), (j - base) * (TN // al))
    return pl.pallas_call(
        kern,
        grid=(m // tm, nt),
        in_specs=[
            pl.BlockSpec((tm, d), lambda i, j: (i, 0)),
            pl.BlockSpec((1, d), lambda i, j: (0, 0)),
            pl.BlockSpec((pl.Element(TN), pl.Element(d)), lambda i, j: (w_row(j), 0)),
            pl.BlockSpec((tm, LANES), lambda i, j: (i % tab_blocks, 0)),
            pl.BlockSpec((tm, LANES), lambda i, j: (i % tab_blocks, 0)),
            pl.BlockSpec((1, LANES), lambda i, j: (0, 0)),
            pl.BlockSpec((1, LANES), lambda i, j: (0, 0)),
        ],
        out_specs=[
            pl.BlockSpec((tm, TN), lambda i, j: (i, j)),
            pl.BlockSpec((tm, TN), lambda i, j: (i, 0)),
            pl.BlockSpec((tm, TN), lambda i, j: (i, 0)),
            pl.BlockSpec((tm, LANES), lambda i, j: (i, 0)),
        ],
        out_shape=[
            jax.ShapeDtypeStruct((m, nt * TN), bf16),
            jax.ShapeDtypeStruct((m, ATT_KV_W), f32),
            jax.ShapeDtypeStruct((m, ATT_KV_W), f32),
            jax.ShapeDtypeStruct((m, LANES), f32),
        ],
        scratch_shapes=[pltpu.VMEM((tm, d), bf16)],
        compiler_params=_params(("arbitrary", "arbitrary")),
    )(x, gain, w_t, cos2, sin2, qn, kn)


def _ret_kernel(lg_ref, q_ref, k_ref, v_ref, sg_ref, gn_ref, og_ref, st_ref, state_ref, *, chunk):
    c = pl.program_id(1)

    @pl.when(c == 0)
    def _():
        state_ref[...] = jnp.zeros_like(state_ref)

    ii = lax.broadcasted_iota(jnp.int32, (chunk, chunk), 0)
    jj = lax.broadcasted_iota(jnp.int32, (chunk, chunk), 1)
    rel = (ii - jj).astype(f32)
    i1 = lax.broadcasted_iota(jnp.int32, (chunk, 1), 0).astype(f32)
    for h in range(RET_HEADS):
        lg = lg_ref[h]
        qk_cols = slice(h * RET_QK_DIM, (h + 1) * RET_QK_DIM)
        v_cols = slice(h * RET_V_DIM, (h + 1) * RET_V_DIM)
        q, k, v = q_ref[:, qk_cols], k_ref[:, qk_cols], v_ref[:, v_cols]
        dmat = jnp.where(rel >= 0, jnp.exp(lg * jnp.maximum(rel, 0.0)), 0.0)
        scores = _dot_nt(q, k) * dmat
        inner = _dot(scores.astype(bf16), v)
        q_dec = jnp.exp(lg * (i1 + 1.0))
        k_dec = jnp.exp(lg * (chunk - 1.0 - i1))
        state = state_ref[h]
        cross = _dot((q.astype(f32) * q_dec).astype(bf16), state.astype(bf16))
        kd = (k.astype(f32) * k_dec).astype(bf16)
        new_state = jnp.exp(lg * chunk) * state + lax.dot_general(
            kd, v, (((0,), (0,)), ((), ())), preferred_element_type=f32)
        state_ref[h] = new_state
        o = _rms(inner + cross, gn_ref[...])
        og_ref[:, v_cols] = (o * sg_ref[:, v_cols].astype(f32)).astype(bf16)

    @pl.when(c == pl.num_programs(1) - 1)
    def _():
        st_ref[0] = state_ref[...]


def _retention(p, log_g, gn, batch, seq, base, chunk):
    nc = seq // chunk
    qb = (base + T_RQ) * TN // RET_QK_W
    kb = (base + T_RK) * TN // RET_QK_W
    vb = (base + T_RV) * TN // RET_V_W
    gb = (base + T_RG) * TN // RET_V_W
    kern = functools.partial(_ret_kernel, chunk=chunk)
    grid_spec = pltpu.PrefetchScalarGridSpec(
        num_scalar_prefetch=1,
        grid=(batch, nc),
        in_specs=[
            pl.BlockSpec((chunk, RET_QK_W), lambda b, c, lg: (b * nc + c, qb)),
            pl.BlockSpec((chunk, RET_QK_W), lambda b, c, lg: (b * nc + c, kb)),
            pl.BlockSpec((chunk, RET_V_W), lambda b, c, lg: (b * nc + c, vb)),
            pl.BlockSpec((chunk, RET_V_W), lambda b, c, lg: (b * nc + c, gb)),
            pl.BlockSpec((1, RET_V_DIM), lambda b, c, lg: (0, 0)),
        ],
        out_specs=[
            pl.BlockSpec((chunk, RET_V_W), lambda b, c, lg: (b * nc + c, 0)),
            pl.BlockSpec((1, RET_HEADS, RET_QK_DIM, RET_V_DIM), lambda b, c, lg: (b, 0, 0, 0)),
        ],
        scratch_shapes=[pltpu.VMEM((RET_HEADS, RET_QK_DIM, RET_V_DIM), f32)],
    )
    return pl.pallas_call(
        kern,
        grid_spec=grid_spec,
        out_shape=[
            jax.ShapeDtypeStruct((batch * seq, RET_V_W), bf16),
            jax.ShapeDtypeStruct((batch, RET_HEADS, RET_QK_DIM, RET_V_DIM), f32),
        ],
        compiler_params=_params(("arbitrary", "arbitrary")),
    )(log_g, p, p, p, p, gn)


def _sortable_key(score):
    bits = lax.bitcast_convert_type(score, jnp.int32)
    return jnp.where(bits >= 0, bits, bits ^ jnp.int32(0x7FFFFFFF))


def _kth_largest_key(count_ge, n_keep, shape):
    def body(it, t_u):
        bit = jnp.left_shift(jnp.int32(1), 31 - it)
        cand_u = t_u | bit
        cnt = count_ge(cand_u ^ jnp.int32(INT_MIN))
        return jnp.where(cnt >= n_keep, cand_u, t_u)
    t_u = lax.fori_loop(0, 32, body, jnp.zeros(shape, jnp.int32))
    return t_u ^ jnp.int32(INT_MIN)


def _attn_kernel(aq_ref, k_ref, v_ref, iq_ref, ikw_k_ref, ikw_q_ref, o_ref,
                 key_ref, iklo_ref, ikhi_ref, vt_ref, m_ref, acc_ref, *, tq, ck, seq, n_keep):
    qb = pl.program_id(1)

    @pl.when(qb == 0)
    def _():
        ikw = ikw_k_ref[...]
        lane = lax.broadcasted_iota(jnp.int32, (seq, LANES), 1)
        iklo_ref[...] = jnp.where(lane < IDX_DIM, ikw, 0.0).astype(bf16)
        ikhi_ref[...] = jnp.where(lane >= IDX_DIM, pltpu.roll(ikw, IDX_DIM, 1), 0.0).astype(bf16)

        for c0 in range(0, seq, ck):
            v_t = jnp.transpose(v_ref[c0:c0 + ck, :].astype(f32)).astype(bf16)
            for g in range(ATT_KV_HEADS):
                r0 = g * (HEAD_DIM + VT_PAD)
                vt_ref[r0:r0 + HEAD_DIM, c0:c0 + ck] = v_t[g * HEAD_DIM:(g + 1) * HEAD_DIM]
                vt_ref[r0 + HEAD_DIM:r0 + HEAD_DIM + VT_PAD, c0:c0 + ck] = jnp.ones((VT_PAD, ck), bf16)

    n_chunks = (qb * tq + tq - 1) // ck + 1
    chunk = lambda kc: pl.ds(pl.multiple_of(kc * ck, ck), ck)
    stack = 4
    pairs = IDX_HEADS // 2

    iw_t = jnp.transpose(ikw_q_ref[...]) * IDX_W_SCALE
    q_stacks = [jnp.concatenate([iq_ref[:, hp * LANES:(hp + 1) * LANES] for hp in range(h0, h0 + stack)], axis=0)
                for h0 in range(0, pairs, stack)]
    k_off = lax.broadcasted_iota(jnp.int32, (ck, tq), 0)
    q_pos = qb * tq + lax.broadcasted_iota(jnp.int32, (ck, tq), 1)

    def score_body(kc, c):
        scores = jnp.zeros((ck, tq), f32)
        for si, qs in enumerate(q_stacks):
            for sub, ik_ref in enumerate((iklo_ref, ikhi_ref)):
                s = jnp.maximum(_dot_nt(ik_ref[chunk(kc), :], qs), 0.0)
                for i in range(stack):
                    row = IDX_DIM + 2 * (si * stack + i) + sub
                    scores = scores + s[:, i * tq:(i + 1) * tq] * iw_t[row:row + 1, :]
        visible = kc * ck + k_off <= q_pos
        key_ref[chunk(kc), :] = jnp.where(visible, _sortable_key(scores), jnp.int32(INT_MIN))
        return c
    lax.fori_loop(0, n_chunks, score_body, 0)

    def count_ge(t):
        t_b = jnp.broadcast_to(t, (8, tq))

        def body(kc, acc):
            keys = key_ref[chunk(kc), :].reshape(ck // 8, 8, tq)
            return acc + jnp.sum(jnp.where(keys >= t_b, 1.0, 0.0), axis=0)
        acc = lax.fori_loop(0, n_chunks, body, jnp.zeros((8, tq), f32))
        return jnp.sum(acc, axis=0, keepdims=True)

    thr = jnp.maximum(_kth_largest_key(count_ge, n_keep, (1, tq)), jnp.int32(INT_MIN + 1))

    m_ref[...] = jnp.full_like(m_ref, NEG_BIG)
    acc_ref[...] = jnp.zeros_like(acc_ref)
    hs = GROUP
    q_sets = [jnp.concatenate([aq_ref[:, hd * HEAD_DIM:(hd + 1) * HEAD_DIM] for hd in range(h0, h0 + hs)], axis=0)
              for h0 in range(0, ATT_HEADS, hs)]

    def attn_body(kc, c):
        bias = jnp.where(key_ref[chunk(kc), :] >= thr, 0.0, -jnp.inf)
        bias = jnp.concatenate([bias] * hs, axis=1)
        for si, qs in enumerate(q_sets):
            g = si * hs // GROUP
            cols = slice(si * hs * tq, (si + 1) * hs * tq)
            s = _dot_nt(k_ref[chunk(kc), g * HEAD_DIM:(g + 1) * HEAD_DIM], qs) + bias
            m_old = m_ref[:, cols]
            m_new = jnp.maximum(m_old, jnp.max(s, axis=0, keepdims=True))
            p = jnp.exp2(s - m_new)
            alpha = jnp.exp2(m_old - m_new)
            r0 = g * (HEAD_DIM + VT_PAD)
            vt = vt_ref[r0:r0 + HEAD_DIM + VT_PAD, chunk(kc)]
            acc_ref[:, cols] = alpha * acc_ref[:, cols] + _dot(vt, p.astype(bf16))
            m_ref[:, cols] = m_new
        return c
    lax.fori_loop(0, n_chunks, attn_body, 0)

    for hd in range(ATT_HEADS):
        cols = slice(hd * tq, (hd + 1) * tq)
        o_t = acc_ref[:HEAD_DIM, cols] / acc_ref[HEAD_DIM:HEAD_DIM + 1, cols]
        o_ref[:, hd * HEAD_DIM:(hd + 1) * HEAD_DIM] = jnp.transpose(o_t).astype(bf16)


def _prompt_attention(p, ikw, batch, seq, base, tq, n_keep):
    nq = seq // tq
    ck = tq
    kern = functools.partial(_attn_kernel, tq=tq, ck=ck, seq=seq, n_keep=n_keep)
    aq_blk = (base + T_AQ) * TN // ATT_Q_W
    iq_blk = (base + T_IQ) * TN // IDX_Q_W
    return pl.pallas_call(
        kern,
        grid=(batch, nq),
        in_specs=[
            pl.BlockSpec((tq, ATT_Q_W), lambda b, t: (b * nq + t, aq_blk)),
            pl.BlockSpec((seq, ATT_KV_W), lambda b, t: (b, base + T_AK)),
            pl.BlockSpec((seq, ATT_KV_W), lambda b, t: (b, base + T_AV)),
            pl.BlockSpec((tq, IDX_Q_W), lambda b, t: (b * nq + t, iq_blk)),
            pl.BlockSpec((seq, LANES), lambda b, t: (b, 0)),
            pl.BlockSpec((tq, LANES), lambda b, t: (b * nq + t, 0)),
        ],
        out_specs=pl.BlockSpec((tq, ATT_Q_W), lambda b, t: (b * nq + t, 0)),
        out_shape=jax.ShapeDtypeStruct((batch * seq, ATT_Q_W), bf16),
        scratch_shapes=[pltpu.VMEM((seq, tq), jnp.int32), pltpu.VMEM((seq, LANES), bf16),
                        pltpu.VMEM((seq, LANES), bf16),
                        pltpu.VMEM((ATT_KV_HEADS * (HEAD_DIM + VT_PAD), seq), bf16),
                        pltpu.VMEM((1, ATT_HEADS * tq), f32),
                        pltpu.VMEM((HEAD_DIM + VT_PAD, ATT_HEADS * tq), f32)],
        compiler_params=_params(("arbitrary", "arbitrary")),
    )(p, p, p, p, ikw, ikw)


def _oproj_kernel(og_ref, att_ref, gr_ref, ga_ref, x_ref, wr_ref, wa_ref, wo_ref, o_ref):
    ret_out = _dot(og_ref[...], wr_ref[...])
    att_out = _dot(att_ref[...], wa_ref[...])
    merged = gr_ref[...].astype(f32) * ret_out + ga_ref[...].astype(f32) * att_out
    o_ref[...] = x_ref[...] + _dot(merged.astype(bf16), wo_ref[...])


def _oproj(og, att, p, x, wr, wa, wo, tm):
    m, d = x.shape
    const = lambda shape: pl.BlockSpec(shape, lambda i: (0, 0), pipeline_mode=pl.Buffered(1))
    return pl.pallas_call(
        _oproj_kernel,
        grid=(m // tm,),
        in_specs=[
            pl.BlockSpec((tm, RET_V_W), lambda i: (i, 0)),
            pl.BlockSpec((tm, ATT_Q_W), lambda i: (i, 0)),
            pl.BlockSpec((tm, d), lambda i: (i, 0)),
            pl.BlockSpec((tm, d), lambda i: (i, 1)),
            pl.BlockSpec((tm, d), lambda i: (i, 0)),
            const(wr.shape), const(wa.shape), const(wo.shape),
        ],
        out_specs=pl.BlockSpec((tm, d), lambda i: (i, 0)),
        out_shape=jax.ShapeDtypeStruct((m, d), f32),
        compiler_params=_params(("arbitrary",)),
    )(og, att, p, p, x, wr, wa, wo)


def _dec_ret_kernel(lg_ref, q_ref, k_ref, v_ref, sg_ref, gn_ref, st_ref, og_ref, nst_ref):
    for h in range(RET_HEADS):
        lg = lg_ref[h]
        gamma = jnp.exp(lg)
        q = q_ref[0, :, h * RET_QK_DIM:(h + 1) * RET_QK_DIM]
        k = k_ref[0, :, h * RET_QK_DIM:(h + 1) * RET_QK_DIM]
        v = v_ref[0, :, h * RET_V_DIM:(h + 1) * RET_V_DIM]
        sg = sg_ref[0, :, h * RET_V_DIM:(h + 1) * RET_V_DIM]
        state = st_ref[0, 0, h]
        qk = jnp.sum(q * k, axis=1, keepdims=True)
        q8 = jnp.broadcast_to((q * gamma).astype(bf16), (8, RET_QK_DIM))
        cross = _dot(q8, state.astype(bf16))[0:1]
        k_col = jnp.transpose(jnp.broadcast_to(k, (8, RET_QK_DIM)))[:, 0:1]
        nst_ref[0, h] = gamma * state + k_col * v
        o = _rms(qk * v + cross, gn_ref[...])
        og_ref[0, :, h * RET_V_DIM:(h + 1) * RET_V_DIM] = o * sg


def _decode_retention(q, k, v, sg, gn, state, log_g):
    db = q.shape[0]
    row = lambda w: pl.BlockSpec((1, 1, w), lambda b, lg: (b, 0, 0))
    grid_spec = pltpu.PrefetchScalarGridSpec(
        num_scalar_prefetch=1,
        grid=(db,),
        in_specs=[row(RET_QK_W), row(RET_QK_W), row(RET_V_W), row(RET_V_W),
                  pl.BlockSpec((1, RET_V_DIM), lambda b, lg: (0, 0)),
                  pl.BlockSpec((1, 1, RET_HEADS, RET_QK_DIM, RET_V_DIM), lambda b, lg: (0, b, 0, 0, 0))],
        out_specs=[row(RET_V_W),
                   pl.BlockSpec((1, RET_HEADS, RET_QK_DIM, RET_V_DIM), lambda b, lg: (b, 0, 0, 0))],
    )
    return pl.pallas_call(
        _dec_ret_kernel,
        grid_spec=grid_spec,
        out_shape=[jax.ShapeDtypeStruct((db, 1, RET_V_W), f32),
                   jax.ShapeDtypeStruct((db, RET_HEADS, RET_QK_DIM, RET_V_DIM), f32)],
        compiler_params=_params(("arbitrary",)),
    )(log_g, q, k, v, sg, gn, state)


def _dec_score_kernel(pt_ref, qi_ref, w_ref, pool_ref, s_ref, buf_ref, sem_ref, *, n_pages):
    b = pl.program_id(0)
    slot = b % 2

    def page_copy(seq, p, slot_):
        dst = buf_ref.at[slot_, :, pl.ds(pl.multiple_of(p * PAGE_SIZE, PAGE_SIZE), PAGE_SIZE)]
        return pltpu.make_async_copy(pool_ref.at[pt_ref[seq, p]], dst, sem_ref.at[slot_])

    def start_all(seq, slot_):
        def body(p, c):
            page_copy(seq, p, slot_).start()
            return c
        lax.fori_loop(0, n_pages, body, 0)

    @pl.when(b == 0)
    def _():
        start_all(0, 0)

    @pl.when(b + 1 < pl.num_programs(0))
    def _():
        start_all(b + 1, 1 - slot)

    def wait_body(p, c):
        page_copy(b, p, slot).wait()
        return c
    lax.fori_loop(0, n_pages, wait_body, 0)

    s = jnp.maximum(_dot(qi_ref[0], buf_ref[slot].astype(bf16)), 0.0)
    row = jnp.sum(s * w_ref[0], axis=0, keepdims=True)
    for p in range(n_pages):
        s_ref[0, p:p + 1, :] = row[:, p * PAGE_SIZE:(p + 1) * PAGE_SIZE]


def _decode_scores(page_table, qi, w, pool_ikt):
    db, n_pages = page_table.shape
    kern = functools.partial(_dec_score_kernel, n_pages=n_pages)
    grid_spec = pltpu.PrefetchScalarGridSpec(
        num_scalar_prefetch=1,
        grid=(db,),
        in_specs=[
            pl.BlockSpec((1, IDX_HEADS, IDX_DIM), lambda b, pt: (b, 0, 0)),
            pl.BlockSpec((1, IDX_HEADS, 1), lambda b, pt: (b, 0, 0)),
            pl.BlockSpec(memory_space=pl.ANY),
        ],
        out_specs=pl.BlockSpec((1, n_pages, PAGE_SIZE), lambda b, pt: (b, 0, 0)),
        scratch_shapes=[pltpu.VMEM((2, IDX_DIM, n_pages * PAGE_SIZE), f32), pltpu.SemaphoreType.DMA((2,))],
    )
    return pl.pallas_call(
        kern,
        grid_spec=grid_spec,
        out_shape=jax.ShapeDtypeStruct((db, n_pages, PAGE_SIZE), f32),
        compiler_params=_params(("arbitrary",)),
    )(page_table, qi, w, pool_ikt)


def _dec_select_kernel(s_ref, qi_ref, w_ref, ikn_ref, idx_ref, cnt_ref, nsel_ref, rank_ref,
                       *, n_keep, kmax):
    db, n_pages, _ = s_ref.shape
    ikn = ikn_ref[...].astype(bf16).astype(f32)
    dots = jnp.sum(qi_ref[...].astype(f32) * ikn, axis=2, keepdims=True)
    s_new = jnp.sum(jnp.maximum(dots, 0.0) * w_ref[...], axis=1, keepdims=True)
    keys = _sortable_key(s_ref[...])
    key_new = _sortable_key(s_new)

    def count_ge(t):
        c = jnp.sum(jnp.where(keys >= t, 1.0, 0.0), axis=1, keepdims=True)
        return jnp.sum(c, axis=2, keepdims=True) + jnp.where(key_new >= t, 1.0, 0.0)

    thr = _kth_largest_key(count_ge, n_keep, (db, 1, 1))
    sel = jnp.where(keys >= thr, 1.0, 0.0)
    nsel_ref[...] = jnp.broadcast_to(jnp.where(key_new >= thr, 1.0, 0.0), (db, 1, LANES))

    pp = lax.broadcasted_iota(jnp.int32, (n_pages, n_pages), 0)
    pq = lax.broadcasted_iota(jnp.int32, (n_pages, n_pages), 1)
    earlier_page = jnp.where(pq < pp, 1.0, 0.0).astype(bf16)
    oo = lax.broadcasted_iota(jnp.int32, (PAGE_SIZE, PAGE_SIZE), 0)
    oq = lax.broadcasted_iota(jnp.int32, (PAGE_SIZE, PAGE_SIZE), 1)
    earlier_off = jnp.where(oo < oq, 1.0, 0.0).astype(bf16)
    r_iota = lax.broadcasted_iota(jnp.int32, (kmax, PAGE_SIZE), 0).astype(f32)
    lane = lax.broadcasted_iota(jnp.int32, (kmax, PAGE_SIZE), 1).astype(f32)
    for b in range(db):
        sb = sel[b]
        per_page = jnp.sum(sb, axis=1, keepdims=True)
        before = _dot(earlier_page, jnp.broadcast_to(per_page, (n_pages, PAGE_SIZE)).astype(bf16))
        within = _dot(sb.astype(bf16), earlier_off)
        rank_ref[b] = jnp.where(sb > 0.0, before + within, -1.0)

        def body(p, acc, b=b):
            row = rank_ref[b, pl.ds(p, 1), :]
            return jnp.where(row == r_iota, lane + jnp.asarray(p * PAGE_SIZE, f32), acc)

        acc = lax.fori_loop(0, n_pages, body, jnp.zeros((kmax, PAGE_SIZE), f32))
        idx_ref[b] = jnp.sum(acc, axis=1, keepdims=True).astype(jnp.int32)
        total = jnp.sum(per_page, axis=0, keepdims=True)
        cnt_ref[b] = jnp.broadcast_to(total, (1, LANES)).astype(jnp.int32)


def _decode_select(scores, qi, w, ik_new, n_keep):
    db, n_pages, _ = scores.shape
    kmax = -(-n_keep // 8) * 8
    kern = functools.partial(_dec_select_kernel, n_keep=n_keep, kmax=kmax)
    full = lambda shape: pl.BlockSpec(shape, lambda i: (0,) * len(shape))
    return pl.pallas_call(
        kern,
        grid=(1,),
        in_specs=[full(scores.shape), full(qi.shape), full(w.shape), full(ik_new.shape)],
        out_specs=[full((db, kmax, 1)), full((db, 1, LANES)), full((db, 1, LANES))],
        out_shape=[jax.ShapeDtypeStruct((db, kmax, 1), jnp.int32),
                   jax.ShapeDtypeStruct((db, 1, LANES), jnp.int32),
                   jax.ShapeDtypeStruct((db, 1, LANES), f32)],
        scratch_shapes=[pltpu.VMEM((db, n_pages, PAGE_SIZE), f32)],
        compiler_params=_params(("arbitrary",)),
    )(scores, qi, w, ik_new)


def _dec_attn_kernel(idx_ref, pt_ref, cnt_ref, q_ref, kn_ref, vn_ref, nsel_ref, kpool_ref, vpool_ref, o_ref,
                     kbuf_ref, vbuf_ref, sem_ref, *, kmax):
    b = pl.program_id(0)
    slot = b % 2
    page_shift = PAGE_SIZE.bit_length() - 1

    def row_copies(seq, r, slot_):
        pos = idx_ref[seq, r]
        phys = pt_ref[seq, lax.shift_right_logical(pos, page_shift)]
        src0 = pl.multiple_of((pos & (PAGE_SIZE - 1)) * ATT_KV_HEADS, ATT_KV_HEADS)
        dst0 = pl.multiple_of(r * ATT_KV_HEADS, ATT_KV_HEADS)
        return (pltpu.make_async_copy(kpool_ref.at[phys, pl.ds(src0, ATT_KV_HEADS), :],
                                      kbuf_ref.at[slot_, pl.ds(dst0, ATT_KV_HEADS), :], sem_ref.at[0, slot_]),
                pltpu.make_async_copy(vpool_ref.at[phys, pl.ds(src0, ATT_KV_HEADS), :],
                                      vbuf_ref.at[slot_, pl.ds(dst0, ATT_KV_HEADS), :], sem_ref.at[1, slot_]))

    def start_all(seq, slot_):
        def body(r, c):
            ck, cv = row_copies(seq, r, slot_)
            ck.start()
            cv.start()
            return c
        lax.fori_loop(0, kmax, body, 0)

    @pl.when(b == 0)
    def _():
        start_all(0, 0)

    @pl.when(b + 1 < pl.num_programs(0))
    def _():
        start_all(b + 1, 1 - slot)

    def wait_body(r, c):
        ck, cv = row_copies(b, r, slot)
        ck.wait()
        cv.wait()
        return c
    lax.fori_loop(0, kmax, wait_body, 0)

    q = q_ref[0]
    s = _dot_nt(q, kbuf_ref[slot].astype(bf16))
    col = lax.broadcasted_iota(jnp.int32, s.shape, 1)
    head = lax.broadcasted_iota(jnp.int32, s.shape, 0)
    ok = (col % ATT_KV_HEADS == head // GROUP) & (col // ATT_KV_HEADS < cnt_ref[b])
    s = jnp.where(ok, s, -jnp.inf)
    s_n = jnp.sum(q.astype(f32) * kn_ref[0].astype(f32), axis=1, keepdims=True)
    s_n = jnp.where(nsel_ref[0][:, 0:1] > 0.0, s_n, -jnp.inf)
    m = jnp.maximum(jnp.max(s, axis=1, keepdims=True), s_n)
    p = jnp.exp2(s - m)
    p_n = jnp.exp2(s_n - m)
    l = jnp.sum(p, axis=1, keepdims=True) + p_n
    pv = _dot(p.astype(bf16), vbuf_ref[slot].astype(bf16))
    o_ref[0] = (pv + p_n.astype(bf16).astype(f32) * vn_ref[0].astype(f32)) / l


def _decode_attention(idx, page_table, cnt, q, k_new, v_new, nsel, pool_k, pool_v):
    db, kmax = idx.shape
    kern = functools.partial(_dec_attn_kernel, kmax=kmax)
    seqb = lambda shape: pl.BlockSpec(shape, lambda b, *_: (b, 0, 0))
    grid_spec = pltpu.PrefetchScalarGridSpec(
        num_scalar_prefetch=3,
        grid=(db,),
        in_specs=[
            seqb((1, ATT_HEADS, HEAD_DIM)),
            seqb((1, ATT_HEADS, HEAD_DIM)),
            seqb((1, ATT_HEADS, HEAD_DIM)),
            seqb((1, 1, LANES)),
            pl.BlockSpec(memory_space=pl.ANY),
            pl.BlockSpec(memory_space=pl.ANY),
        ],
        out_specs=seqb((1, ATT_HEADS, HEAD_DIM)),
        scratch_shapes=[pltpu.VMEM((2, kmax * ATT_KV_HEADS, HEAD_DIM), f32),
                        pltpu.VMEM((2, kmax * ATT_KV_HEADS, HEAD_DIM), f32),
                        pltpu.SemaphoreType.DMA((2, 2))],
    )
    return pl.pallas_call(
        kern,
        grid_spec=grid_spec,
        out_shape=jax.ShapeDtypeStruct((db, ATT_HEADS, HEAD_DIM), f32),
        compiler_params=_params(("arbitrary",)),
    )(idx, page_table, cnt, q, k_new, v_new, nsel, pool_k, pool_v)


def _rope_tables(pos):
    half = RET_QK_DIM // 2
    inv = ROPE_BASE ** (-jnp.arange(half, dtype=f32) / half)
    ang = pos.astype(f32)[:, None] * inv[None, :]
    cos, sin = jnp.cos(ang), jnp.sin(ang)
    return jnp.concatenate([cos, cos], axis=1), jnp.concatenate([-sin, sin], axis=1)


def _pick(n, pref):
    for t in pref:
        if n % t == 0:
            return t
    return n


def kernel(x_prompt, x_sample, state_ret, cache_k, cache_v, cache_idx_k, page_table, ffn1_norm, ffn1_w1, ffn1_w2, mix_norm, w_in, q_norm, k_norm, ret_norm, w_ret_out, w_att_out, w_o, ffn2_norm, ffn2_w1, ffn2_w2):
    batch, seq, d = x_prompt.shape
    db, ts, _ = x_sample.shape
    depth = w_in.shape[0]
    assert depth == 1 and ts == 1 and d % (2 * TN) == 0
    n_pages = page_table.shape[1]
    past = n_pages * PAGE_SIZE
    n_phys = cache_k.shape[1]
    base = 2 * (d // TN)
    m = batch * seq
    ms = 16

    log_g = jnp.log1p(-jnp.exp2(-5.0 - jnp.arange(RET_HEADS, dtype=f32)))
    w1a, w2a = ffn1_w1[0].astype(bf16), ffn1_w2[0].astype(bf16)
    w1b, w2b = ffn2_w1[0].astype(bf16), ffn2_w2[0].astype(bf16)
    w_pad = jnp.swapaxes(w_in[0], 0, 1).astype(bf16)
    wr, wa, wo = w_ret_out[0].astype(bf16), w_att_out[0].astype(bf16), w_o[0].astype(bf16)
    g1, g2, g3 = ffn1_norm, mix_norm, ffn2_norm
    qn, kn, gn = q_norm, k_norm, ret_norm

    tm = _pick(m, (1024, 512, 256, 128))
    tf = _pick(ffn1_w2.shape[1], (512, 256, 128))
    tmp = _pick(seq, (1024, 512, 256, 128))
    chunk = _pick(seq, (256, 128))
    tq = _pick(seq, (256, 128))
    tmo = _pick(m, (256, 128))

    xp = x_prompt.reshape(m, d)
    x1 = _ffn(xp, g1, w1a, w2a, tm, tf)
    cos_p, sin_p = _rope_tables(jnp.arange(seq))
    p, k32, v32, ikw = _proj(x1, g2, w_pad, cos_p, sin_p, qn, kn, tmp)
    og, st_p = _retention(p, log_g, gn, batch, seq, base, chunk)
    att = _prompt_attention(p, ikw, batch, seq, base, tq, min(TOPK_MAX, seq // 4))
    x2 = _oproj(og, att, p, x1, wr, wa, wo, tmo)
    yp = _ffn(x2, g3, w1b, w2b, tm, tf).reshape(batch, seq, d)

    xs = jnp.zeros((ms, d), f32).at[:db].set(x_sample.reshape(db, d))
    s1 = _ffn(xs, g1, w1a, w2a, ms, tf)
    cos_s, sin_s = _rope_tables(jnp.full((ms,), past, jnp.int32))
    ps, k32s, v32s, ikws = _proj(s1, g2, w_pad, cos_s, sin_s, qn, kn, ms)
    seg = lambda t0, t1: ps[:db, (base + t0) * TN:(base + t1) * TN]
    row3 = lambda a: a.astype(f32).reshape(db, 1, a.shape[-1])
    og_s, st_s = _decode_retention(row3(seg(T_RQ, T_RK)), row3(seg(T_RK, T_RV)), row3(seg(T_RV, T_RG)),
                                   row3(seg(T_RG, T_AQ)), gn, state_ret, log_g)
    qi = seg(T_IQ, T_IKW).reshape(db, IDX_HEADS, IDX_DIM)
    wi = (ikws[:db, IDX_DIM:IDX_DIM + IDX_HEADS] * IDX_W_SCALE).reshape(db, IDX_HEADS, 1)
    ik_new = ikws[:db, :IDX_DIM].reshape(db, 1, IDX_DIM)
    pool_ikt = jnp.swapaxes(cache_idx_k.reshape(n_phys, PAGE_SIZE, IDX_DIM), 1, 2)
    scores = _decode_scores(page_table, qi, wi, pool_ikt)
    idx, cnt, nsel = _decode_select(scores, qi, wi, ik_new, min(TOPK_MAX, (past + ts) // 4))
    aq_s = seg(T_AQ, T_AK).reshape(db, ATT_HEADS, HEAD_DIM)
    expand = lambda a: jnp.repeat(a[:db].reshape(db, ATT_KV_HEADS, HEAD_DIM), GROUP, axis=1).astype(bf16)
    pool_rows = lambda c: c.reshape(n_phys, PAGE_SIZE * ATT_KV_HEADS, HEAD_DIM)
    att_s = _decode_attention(idx.reshape(db, -1), page_table, cnt[:, 0, 0], aq_s, expand(k32s), expand(v32s),
                              nsel, pool_rows(cache_k), pool_rows(cache_v))
    pad_rows = lambda a: jnp.pad(a.reshape(db, -1).astype(bf16), ((0, ms - db), (0, 0)))
    s2 = _oproj(pad_rows(og_s), pad_rows(att_s), ps, s1, wr, wa, wo, ms)
    ys = _ffn(s2, g3, w1b, w2b, ms, tf)[:db].reshape(db, ts, d)

    return (yp, ys,
            st_p[None],
            k32.reshape(1, batch, seq, ATT_KV_HEADS, HEAD_DIM),
            v32.reshape(1, batch, seq, ATT_KV_HEADS, HEAD_DIM),
            ikw[:, :IDX_DIM].reshape(1, batch, seq, IDX_DIM),
            st_s[None],
            k32s[:db].reshape(1, db, ts, ATT_KV_HEADS, HEAD_DIM),
            v32s[:db].reshape(1, db, ts, ATT_KV_HEADS, HEAD_DIM),
            ikws[:db, :IDX_DIM].reshape(1, db, ts, IDX_DIM))
```

```python
import functools

import jax
import jax.numpy as jnp
from jax import lax
from jax.experimental import pallas as pl
from jax.experimental.pallas import tpu as pltpu

RET_HEADS = 8
RET_QK_DIM = 128
RET_V_DIM = 256
ATT_HEADS = 16
ATT_KV_HEADS = 4
HEAD_DIM = 128
GROUP = ATT_HEADS // ATT_KV_HEADS
IDX_HEADS = 16
IDX_DIM = 64
IDX_W_SCALE = (IDX_HEADS ** -0.5) * (IDX_DIM ** -0.5)
TOPK_MAX = 256
PAGE_SIZE = 128
ROPE_BASE = 10000.0
EPS = 1e-6

RET_QK_W = RET_HEADS * RET_QK_DIM
RET_V_W = RET_HEADS * RET_V_DIM
ATT_Q_W = ATT_HEADS * HEAD_DIM
ATT_KV_W = ATT_KV_HEADS * HEAD_DIM
IDX_Q_W = IDX_HEADS * IDX_DIM
MAIN_W = 2 * RET_QK_W + 2 * RET_V_W + ATT_Q_W + 2 * ATT_KV_W + IDX_Q_W

LANES = 128
TN = 512
T_RQ, T_RK, T_RV, T_RG, T_AQ, T_AK, T_AV, T_IQ, T_IKW, T_END = 0, 2, 4, 8, 12, 16, 17, 18, 20, 21
VMEM_LIMIT = 56 * 1024 * 1024
INT_MIN = -2 ** 31
I16_MIN, I16_MAX = -2 ** 15, 2 ** 15 - 1
NEG_BIG = -1e30
LOG2E = 1.4426950408889634
VT_PAD = 16

bf16 = jnp.bfloat16
f32 = jnp.float32


def _params(sem, vmem=VMEM_LIMIT):
    return pltpu.CompilerParams(dimension_semantics=sem, vmem_limit_bytes=vmem)


def _sigmoid(x):
    return 1.0 / (1.0 + jnp.exp(-x))


def _dot(a, b):
    return jnp.dot(a, b, preferred_element_type=f32)


def _dot_nt(a, b):
    return lax.dot_general(a, b, (((1,), (1,)), ((), ())), preferred_element_type=f32)


def _rms(x, gain):
    ms = jnp.mean(x * x, axis=-1, keepdims=True)
    return x * lax.rsqrt(ms + EPS) * gain


def _ffn_kernel(x_ref, g_ref, w1g_ref, w1u_ref, w2_ref, o_ref, h_ref):
    j = pl.program_id(1)

    @pl.when(j == 0)
    def _():
        h_ref[...] = _rms(x_ref[...], g_ref[...]).astype(bf16)
        o_ref[...] = jnp.zeros_like(o_ref)

    h = h_ref[...]
    gate = _dot(h, w1g_ref[...].astype(bf16))
    up = _dot(h, w1u_ref[...].astype(bf16))
    act = (gate * _sigmoid(gate) * up).astype(bf16)
    o_ref[...] += _dot(act, w2_ref[...].astype(bf16))

    @pl.when(j == pl.num_programs(1) - 1)
    def _():
        o_ref[...] = x_ref[...] + 0.5 * o_ref[...]


def _ffn(x, gain, w1, w2, tm, tf):
    m, d = x.shape
    dff = w2.shape[0]
    nf = dff // tf
    return pl.pallas_call(
        _ffn_kernel,
        grid=(m // tm, nf),
        in_specs=[
            pl.BlockSpec((tm, d), lambda i, j: (i, 0), pipeline_mode=pl.Buffered(1)),
            pl.BlockSpec((1, d), lambda i, j: (0, 0)),
            pl.BlockSpec((d, tf), lambda i, j: (0, j)),
            pl.BlockSpec((d, tf), lambda i, j: (0, j + nf)),
            pl.BlockSpec((tf, d), lambda i, j: (j, 0)),
        ],
        out_specs=pl.BlockSpec((tm, d), lambda i, j: (i, 0)),
        out_shape=jax.ShapeDtypeStruct((m, d), f32),
        scratch_shapes=[pltpu.VMEM((tm, d), bf16)],
        compiler_params=_params(("arbitrary", "arbitrary")),
    )(x, gain, w1, w1, w2)


def _proj_kernel(x_ref, g_ref, w_ref, cos_ref, sin_ref, qn_ref, kn_ref,
                 p_ref, k32_ref, v32_ref, ikw_ref, h_ref, *, base):
    j = pl.program_id(1)

    @pl.when(j == 0)
    def _():
        h_ref[...] = _rms(x_ref[...], g_ref[...]).astype(bf16)

    w = w_ref[...].astype(bf16)
    heads = [slice(c * LANES, (c + 1) * LANES) for c in range(TN // LANES)]
    tm = h_ref.shape[0]
    rc = min(tm, 256)

    def segment(cond, epilogue):
        @pl.when(cond)
        def _():
            for r0 in range(0, tm, rc):
                rows = slice(r0, r0 + rc)
                epilogue(_dot_nt(h_ref[rows, :], w), rows)

    def gates(res, rows):
        p_ref[rows, :] = _sigmoid(res).astype(bf16)

    def rotary(res, rows):
        cos, sin = cos_ref[rows, :], sin_ref[rows, :]
        scale = jnp.where(j >= base + T_RK, RET_QK_DIM ** -0.5, 1.0).astype(f32)
        for sl in heads:
            x = res[:, sl]
            p_ref[rows, sl] = ((x * cos + pltpu.roll(x, LANES // 2, 1) * sin) * scale).astype(bf16)

    def plain(res, rows):
        p_ref[rows, :] = res.astype(bf16)

    def silu(res, rows):
        p_ref[rows, :] = (res * _sigmoid(res)).astype(bf16)

    def q_norm(res, rows):
        for sl in heads:
            p_ref[rows, sl] = (_rms(res[:, sl], qn_ref[...]) * (HEAD_DIM ** -0.5 * LOG2E)).astype(bf16)

    def k_norm(res, rows):
        for sl in heads:
            y = _rms(res[:, sl], kn_ref[...])
            k32_ref[rows, sl] = y
            p_ref[rows, sl] = y.astype(bf16)

    def value(res, rows):
        v32_ref[rows, :] = res
        p_ref[rows, :] = res.astype(bf16)

    def idx_kw(res, rows):
        ikw_ref[rows, :] = res[:, :LANES]
        p_ref[rows, :] = res.astype(bf16)

    segment(j < base, gates)
    segment((j >= base + T_RQ) & (j < base + T_RV), rotary)
    segment(((j >= base + T_RV) & (j < base + T_RG)) | ((j >= base + T_IQ) & (j < base + T_IKW)), plain)
    segment((j >= base + T_RG) & (j < base + T_AQ), silu)
    segment((j >= base + T_AQ) & (j < base + T_AK), q_norm)
    segment(j == base + T_AK, k_norm)
    segment(j == base + T_AV, value)
    segment(j == base + T_IKW, idx_kw)


def _proj(x, gain, w_t, cos2, sin2, qn, kn, tm):
    m, d = x.shape
    gate_row0 = MAIN_W + IDX_DIM + IDX_HEADS
    base = (w_t.shape[0] - gate_row0) // TN
    nt = base + T_END
    tab_blocks = cos2.shape[0] // tm
    kern = functools.partial(_proj_kernel, base=base)
    al = 16
    assert gate_row0 % al == 0
    w_row = lambda j: al * jnp.where(j < base, gate_row0 // al + j * (TN // al), (j - base) * (TN // al))
    return pl.pallas_call(
        kern,
        grid=(m // tm, nt),
        in_specs=[
            pl.BlockSpec((tm, d), lambda i, j: (i, 0)),
            pl.BlockSpec((1, d), lambda i, j: (0, 0)),
            pl.BlockSpec((pl.Element(TN), pl.Element(d)), lambda i, j: (w_row(j), 0)),
            pl.BlockSpec((tm, LANES), lambda i, j: (i % tab_blocks, 0)),
            pl.BlockSpec((tm, LANES), lambda i, j: (i % tab_blocks, 0)),
            pl.BlockSpec((1, LANES), lambda i, j: (0, 0)),
            pl.BlockSpec((1, LANES), lambda i, j: (0, 0)),
        ],
        out_specs=[
            pl.BlockSpec((tm, TN), lambda i, j: (i, j)),
            pl.BlockSpec((tm, TN), lambda i, j: (i, 0)),
            pl.BlockSpec((tm, TN), lambda i, j: (i, 0)),
            pl.BlockSpec((tm, LANES), lambda i, j: (i, 0)),
        ],
        out_shape=[
            jax.ShapeDtypeStruct((m, nt * TN), bf16),
            jax.ShapeDtypeStruct((m, ATT_KV_W), f32),
            jax.ShapeDtypeStruct((m, ATT_KV_W), f32),
            jax.ShapeDtypeStruct((m, LANES), f32),
        ],
        scratch_shapes=[pltpu.VMEM((tm, d), bf16)],
        compiler_params=_params(("arbitrary", "arbitrary")),
    )(x, gain, w_t, cos2, sin2, qn, kn)


def _ret_kernel(lg_ref, q_ref, k_ref, v_ref, sg_ref, gn_ref, og_ref, st_ref, state_ref, *, chunk):
    c = pl.program_id(1)

    @pl.when(c == 0)
    def _():
        state_ref[...] = jnp.zeros_like(state_ref)

    ii = lax.broadcasted_iota(jnp.int32, (chunk, chunk), 0)
    jj = lax.broadcasted_iota(jnp.int32, (chunk, chunk), 1)
    rel = (ii - jj).astype(f32)
    i1 = lax.broadcasted_iota(jnp.int32, (chunk, 1), 0).astype(f32)
    for h in range(RET_HEADS):
        lg = lg_ref[h]
        qk_cols = slice(h * RET_QK_DIM, (h + 1) * RET_QK_DIM)
        v_cols = slice(h * RET_V_DIM, (h + 1) * RET_V_DIM)
        q, k, v = q_ref[:, qk_cols], k_ref[:, qk_cols], v_ref[:, v_cols]
        dmat = jnp.where(rel >= 0, jnp.exp(lg * jnp.maximum(rel, 0.0)), 0.0)
        scores = _dot_nt(q, k) * dmat
        inner = _dot(scores.astype(bf16), v)
        q_dec = jnp.exp(lg * (i1 + 1.0))
        k_dec = jnp.exp(lg * (chunk - 1.0 - i1))
        state = state_ref[h]
        cross = _dot((q.astype(f32) * q_dec).astype(bf16), state.astype(bf16))
        kd = (k.astype(f32) * k_dec).astype(bf16)
        new_state = jnp.exp(lg * chunk) * state + lax.dot_general(
            kd, v, (((0,), (0,)), ((), ())), preferred_element_type=f32)
        state_ref[h] = new_state
        o = _rms(inner + cross, gn_ref[...])
        og_ref[:, v_cols] = (o * sg_ref[:, v_cols].astype(f32)).astype(bf16)

    @pl.when(c == pl.num_programs(1) - 1)
    def _():
        st_ref[0] = state_ref[...]


def _retention(p, log_g, gn, batch, seq, base, chunk):
    nc = seq // chunk
    qb = (base + T_RQ) * TN // RET_QK_W
    kb = (base + T_RK) * TN // RET_QK_W
    vb = (base + T_RV) * TN // RET_V_W
    gb = (base + T_RG) * TN // RET_V_W
    kern = functools.partial(_ret_kernel, chunk=chunk)
    grid_spec = pltpu.PrefetchScalarGridSpec(
        num_scalar_prefetch=1,
        grid=(batch, nc),
        in_specs=[
            pl.BlockSpec((chunk, RET_QK_W), lambda b, c, lg: (b * nc + c, qb)),
            pl.BlockSpec((chunk, RET_QK_W), lambda b, c, lg: (b * nc + c, kb)),
            pl.BlockSpec((chunk, RET_V_W), lambda b, c, lg: (b * nc + c, vb)),
            pl.BlockSpec((chunk, RET_V_W), lambda b, c, lg: (b * nc + c, gb)),
            pl.BlockSpec((1, RET_V_DIM), lambda b, c, lg: (0, 0)),
        ],
        out_specs=[
            pl.BlockSpec((chunk, RET_V_W), lambda b, c, lg: (b * nc + c, 0)),
            pl.BlockSpec((1, RET_HEADS, RET_QK_DIM, RET_V_DIM), lambda b, c, lg: (b, 0, 0, 0)),
        ],
        scratch_shapes=[pltpu.VMEM((RET_HEADS, RET_QK_DIM, RET_V_DIM), f32)],
    )
    return pl.pallas_call(
        kern,
        grid_spec=grid_spec,
        out_shape=[
            jax.ShapeDtypeStruct((batch * seq, RET_V_W), bf16),
            jax.ShapeDtypeStruct((batch, RET_HEADS, RET_QK_DIM, RET_V_DIM), f32),
        ],
        compiler_params=_params(("arbitrary", "arbitrary")),
    )(log_g, p, p, p, p, gn)


def _sortable_key(score):
    bits = lax.bitcast_convert_type(score, jnp.int32)
    return jnp.where(bits >= 0, bits, bits ^ jnp.int32(0x7FFFFFFF))


def _kth_largest_key(count_ge, n_keep, shape):
    def body(it, t_u):
        bit = jnp.left_shift(jnp.int32(1), 31 - it)
        cand_u = t_u | bit
        cnt = count_ge(cand_u ^ jnp.int32(INT_MIN))
        return jnp.where(cnt >= n_keep, cand_u, t_u)
    t_u = lax.fori_loop(0, 32, body, jnp.zeros(shape, jnp.int32))
    return t_u ^ jnp.int32(INT_MIN)


def _kth_largest_16(count_ge, n_keep, shape):
    def body(it, t_u):
        cand_u = t_u | jnp.left_shift(jnp.int32(1), 15 - it)
        cnt = count_ge(cand_u + I16_MIN)
        return jnp.where(cnt >= n_keep, cand_u, t_u)
    t_u = lax.fori_loop(0, 16, body, jnp.zeros(shape, jnp.int32))
    return t_u + I16_MIN


def _attn_kernel(aq_ref, k_ref, v_ref, iq_ref, ikw_k_ref, ikw_q_ref, o_ref,
                 key_ref, hi_ref, lo_ref, iklo_ref, ikhi_ref, vt_ref, m_ref, acc_ref, *, tq, ck, seq, n_keep):
    qb = pl.program_id(1)

    @pl.when(qb == 0)
    def _():
        ikw = ikw_k_ref[...]
        lane = lax.broadcasted_iota(jnp.int32, (seq, LANES), 1)
        iklo_ref[...] = jnp.where(lane < IDX_DIM, ikw, 0.0).astype(bf16)
        ikhi_ref[...] = jnp.where(lane >= IDX_DIM, pltpu.roll(ikw, IDX_DIM, 1), 0.0).astype(bf16)

        for c0 in range(0, seq, ck):
            v_t = jnp.transpose(v_ref[c0:c0 + ck, :].astype(f32)).astype(bf16)
            for g in range(ATT_KV_HEADS):
                r0 = g * (HEAD_DIM + VT_PAD)
                vt_ref[r0:r0 + HEAD_DIM, c0:c0 + ck] = v_t[g * HEAD_DIM:(g + 1) * HEAD_DIM]
                vt_ref[r0 + HEAD_DIM:r0 + HEAD_DIM + VT_PAD, c0:c0 + ck] = jnp.ones((VT_PAD, ck), bf16)

    n_chunks = (qb * tq + tq - 1) // ck + 1
    chunk = lambda kc: pl.ds(pl.multiple_of(kc * ck, ck), ck)
    stack = 4
    pairs = IDX_HEADS // 2

    iw_t = jnp.transpose(ikw_q_ref[...]) * IDX_W_SCALE
    q_stacks = [jnp.concatenate([iq_ref[:, hp * LANES:(hp + 1) * LANES] for hp in range(h0, h0 + stack)], axis=0)
                for h0 in range(0, pairs, stack)]
    k_off = lax.broadcasted_iota(jnp.int32, (ck, tq), 0)
    q_pos = qb * tq + lax.broadcasted_iota(jnp.int32, (ck, tq), 1)

    def score_body(kc, c):
        scores = jnp.zeros((ck, tq), f32)
        for si, qs in enumerate(q_stacks):
            for sub, ik_ref in enumerate((iklo_ref, ikhi_ref)):
                s = jnp.maximum(_dot_nt(ik_ref[chunk(kc), :], qs), 0.0)
                for i in range(stack):
                    row = IDX_DIM + 2 * (si * stack + i) + sub
                    scores = scores + s[:, i * tq:(i + 1) * tq] * iw_t[row:row + 1, :]
        visible = kc * ck + k_off <= q_pos
        key_ref[chunk(kc), :] = jnp.where(visible, _sortable_key(scores), jnp.int32(INT_MIN))
        return c
    lax.fori_loop(0, n_chunks, score_body, 0)

    i16 = jnp.int16
    sub = 16

    def count16_ge(half_ref):
        def count(t):
            t_b = jnp.broadcast_to(t, (sub, tq)).astype(i16)

            def body(kc, acc):
                x = half_ref[chunk(kc), :]
                for r0 in range(0, ck, sub):
                    acc = acc + jnp.where(x[r0:r0 + sub] >= t_b, i16(1), i16(0))
                return acc
            acc = lax.fori_loop(0, n_chunks, body, jnp.zeros((sub, tq), i16))
            return jnp.sum(acc.astype(f32), axis=0, keepdims=True)
        return count

    def hi_body(kc, c):
        hi_ref[chunk(kc), :] = lax.shift_right_arithmetic(key_ref[chunk(kc), :], 16).astype(i16)
        return c
    lax.fori_loop(0, n_chunks, hi_body, 0)
    count_hi = count16_ge(hi_ref)
    hi_thr = _kth_largest_16(count_hi, float(n_keep), (1, tq))
    above = jnp.where(hi_thr >= I16_MAX, 0.0, count_hi(jnp.minimum(hi_thr + 1, I16_MAX)))

    def lo_body(kc, c):
        keys = key_ref[chunk(kc), :]
        lo = ((keys & 0xFFFF) - (I16_MAX + 1)).astype(i16)
        same_hi = lax.shift_right_arithmetic(keys, 16) == hi_thr
        lo_ref[chunk(kc), :] = jnp.where(same_hi, lo, i16(I16_MIN))
        return c
    lax.fori_loop(0, n_chunks, lo_body, 0)
    lo_thr = _kth_largest_16(count16_ge(lo_ref), n_keep - above, (1, tq))
    thr = hi_thr * 65536 + (lo_thr + (I16_MAX + 1))
    thr = jnp.maximum(thr, jnp.int32(INT_MIN + 1))

    m_ref[...] = jnp.full_like(m_ref, NEG_BIG)
    acc_ref[...] = jnp.zeros_like(acc_ref)
    hs = GROUP
    q_sets = [jnp.concatenate([aq_ref[:, hd * HEAD_DIM:(hd + 1) * HEAD_DIM] for hd in range(h0, h0 + hs)], axis=0)
              for h0 in range(0, ATT_HEADS, hs)]

    def attn_body(kc, c):
        bias = jnp.where(key_ref[chunk(kc), :] >= thr, 0.0, -jnp.inf)
        bias = jnp.concatenate([bias] * hs, axis=1)
        for si, qs in enumerate(q_sets):
            g = si * hs // GROUP
            cols = slice(si * hs * tq, (si + 1) * hs * tq)
            s = _dot_nt(k_ref[chunk(kc), g * HEAD_DIM:(g + 1) * HEAD_DIM], qs) + bias
            m_old = m_ref[:, cols]
            m_new = jnp.maximum(m_old, jnp.max(s, axis=0, keepdims=True))
            p = jnp.exp2(s - m_new)
            alpha = jnp.exp2(m_old - m_new)
            r0 = g * (HEAD_DIM + VT_PAD)
            vt = vt_ref[r0:r0 + HEAD_DIM + VT_PAD, chunk(kc)]
            acc_ref[:, cols] = alpha * acc_ref[:, cols] + _dot(vt, p.astype(bf16))
            m_ref[:, cols] = m_new
        return c
    lax.fori_loop(0, n_chunks, attn_body, 0)

    for hd in range(ATT_HEADS):
        cols = slice(hd * tq, (hd + 1) * tq)
        o_t = acc_ref[:HEAD_DIM, cols] / acc_ref[HEAD_DIM:HEAD_DIM + 1, cols]
        o_ref[:, hd * HEAD_DIM:(hd + 1) * HEAD_DIM] = jnp.transpose(o_t).astype(bf16)


def _prompt_attention(p, ikw, batch, seq, base, tq, n_keep):
    nq = seq // tq
    ck = tq
    kern = functools.partial(_attn_kernel, tq=tq, ck=ck, seq=seq, n_keep=n_keep)
    aq_blk = (base + T_AQ) * TN // ATT_Q_W
    iq_blk = (base + T_IQ) * TN // IDX_Q_W
    return pl.pallas_call(
        kern,
        grid=(batch, nq),
        in_specs=[
            pl.BlockSpec((tq, ATT_Q_W), lambda b, t: (b * nq + t, aq_blk)),
            pl.BlockSpec((seq, ATT_KV_W), lambda b, t: (b, base + T_AK)),
            pl.BlockSpec((seq, ATT_KV_W), lambda b, t: (b, base + T_AV)),
            pl.BlockSpec((tq, IDX_Q_W), lambda b, t: (b * nq + t, iq_blk)),
            pl.BlockSpec((seq, LANES), lambda b, t: (b, 0)),
            pl.BlockSpec((tq, LANES), lambda b, t: (b * nq + t, 0)),
        ],
        out_specs=pl.BlockSpec((tq, ATT_Q_W), lambda b, t: (b * nq + t, 0)),
        out_shape=jax.ShapeDtypeStruct((batch * seq, ATT_Q_W), bf16),
        scratch_shapes=[pltpu.VMEM((seq, tq), jnp.int32), pltpu.VMEM((seq, tq), jnp.int16),
                        pltpu.VMEM((seq, tq), jnp.int16), pltpu.VMEM((seq, LANES), bf16),
                        pltpu.VMEM((seq, LANES), bf16),
                        pltpu.VMEM((ATT_KV_HEADS * (HEAD_DIM + VT_PAD), seq), bf16),
                        pltpu.VMEM((1, ATT_HEADS * tq), f32),
                        pltpu.VMEM((HEAD_DIM + VT_PAD, ATT_HEADS * tq), f32)],
        compiler_params=_params(("arbitrary", "arbitrary")),
    )(p, p, p, p, ikw, ikw)


def _oproj_kernel(og_ref, att_ref, gr_ref, ga_ref, x_ref, wr_ref, wa_ref, wo_ref, o_ref):
    ret_out = _dot(og_ref[...], wr_ref[...])
    att_out = _dot(att_ref[...], wa_ref[...])
    merged = gr_ref[...].astype(f32) * ret_out + ga_ref[...].astype(f32) * att_out
    o_ref[...] = x_ref[...] + _dot(merged.astype(bf16), wo_ref[...])


def _oproj(og, att, p, x, wr, wa, wo, tm):
    m, d = x.shape
    const = lambda shape: pl.BlockSpec(shape, lambda i: (0, 0), pipeline_mode=pl.Buffered(1))
    return pl.pallas_call(
        _oproj_kernel,
        grid=(m // tm,),
        in_specs=[
            pl.BlockSpec((tm, RET_V_W), lambda i: (i, 0)),
            pl.BlockSpec((tm, ATT_Q_W), lambda i: (i, 0)),
            pl.BlockSpec((tm, d), lambda i: (i, 0)),
            pl.BlockSpec((tm, d), lambda i: (i, 1)),
            pl.BlockSpec((tm, d), lambda i: (i, 0)),
            const(wr.shape), const(wa.shape), const(wo.shape),
        ],
        out_specs=pl.BlockSpec((tm, d), lambda i: (i, 0)),
        out_shape=jax.ShapeDtypeStruct((m, d), f32),
        compiler_params=_params(("arbitrary",)),
    )(og, att, p, p, x, wr, wa, wo)


def _dec_ret_kernel(lg_ref, q_ref, k_ref, v_ref, sg_ref, gn_ref, st_ref, og_ref, nst_ref):
    for h in range(RET_HEADS):
        lg = lg_ref[h]
        gamma = jnp.exp(lg)
        q = q_ref[0, :, h * RET_QK_DIM:(h + 1) * RET_QK_DIM]
        k = k_ref[0, :, h * RET_QK_DIM:(h + 1) * RET_QK_DIM]
        v = v_ref[0, :, h * RET_V_DIM:(h + 1) * RET_V_DIM]
        sg = sg_ref[0, :, h * RET_V_DIM:(h + 1) * RET_V_DIM]
        state = st_ref[0, 0, h]
        qk = jnp.sum(q * k, axis=1, keepdims=True)
        q8 = jnp.broadcast_to((q * gamma).astype(bf16), (8, RET_QK_DIM))
        cross = _dot(q8, state.astype(bf16))[0:1]
        k_col = jnp.transpose(jnp.broadcast_to(k, (8, RET_QK_DIM)))[:, 0:1]
        nst_ref[0, h] = gamma * state + k_col * v
        o = _rms(qk * v + cross, gn_ref[...])
        og_ref[0, :, h * RET_V_DIM:(h + 1) * RET_V_DIM] = o * sg


def _decode_retention(q, k, v, sg, gn, state, log_g):
    db = q.shape[0]
    row = lambda w: pl.BlockSpec((1, 1, w), lambda b, lg: (b, 0, 0))
    grid_spec = pltpu.PrefetchScalarGridSpec(
        num_scalar_prefetch=1,
        grid=(db,),
        in_specs=[row(RET_QK_W), row(RET_QK_W), row(RET_V_W), row(RET_V_W),
                  pl.BlockSpec((1, RET_V_DIM), lambda b, lg: (0, 0)),
                  pl.BlockSpec((1, 1, RET_HEADS, RET_QK_DIM, RET_V_DIM), lambda b, lg: (0, b, 0, 0, 0))],
        out_specs=[row(RET_V_W),
                   pl.BlockSpec((1, RET_HEADS, RET_QK_DIM, RET_V_DIM), lambda b, lg: (b, 0, 0, 0))],
    )
    return pl.pallas_call(
        _dec_ret_kernel,
        grid_spec=grid_spec,
        out_shape=[jax.ShapeDtypeStruct((db, 1, RET_V_W), f32),
                   jax.ShapeDtypeStruct((db, RET_HEADS, RET_QK_DIM, RET_V_DIM), f32)],
        compiler_params=_params(("arbitrary",)),
    )(log_g, q, k, v, sg, gn, state)


def _dec_score_kernel(pt_ref, qi_ref, w_ref, pool_ref, s_ref, buf_ref, sem_ref, *, n_pages):
    b = pl.program_id(0)
    slot = b % 2

    def page_copy(seq, p, slot_):
        dst = buf_ref.at[slot_, :, pl.ds(pl.multiple_of(p * PAGE_SIZE, PAGE_SIZE), PAGE_SIZE)]
        return pltpu.make_async_copy(pool_ref.at[pt_ref[seq, p]], dst, sem_ref.at[slot_])

    def start_all(seq, slot_):
        def body(p, c):
            page_copy(seq, p, slot_).start()
            return c
        lax.fori_loop(0, n_pages, body, 0)

    @pl.when(b == 0)
    def _():
        start_all(0, 0)

    @pl.when(b + 1 < pl.num_programs(0))
    def _():
        start_all(b + 1, 1 - slot)

    def wait_body(p, c):
        page_copy(b, p, slot).wait()
        return c
    lax.fori_loop(0, n_pages, wait_body, 0)

    s = jnp.maximum(_dot(qi_ref[0], buf_ref[slot].astype(bf16)), 0.0)
    row = jnp.sum(s * w_ref[0], axis=0, keepdims=True)
    for p in range(n_pages):
        s_ref[0, p:p + 1, :] = row[:, p * PAGE_SIZE:(p + 1) * PAGE_SIZE]


def _decode_scores(page_table, qi, w, pool_ikt):
    db, n_pages = page_table.shape
    kern = functools.partial(_dec_score_kernel, n_pages=n_pages)
    grid_spec = pltpu.PrefetchScalarGridSpec(
        num_scalar_prefetch=1,
        grid=(db,),
        in_specs=[
            pl.BlockSpec((1, IDX_HEADS, IDX_DIM), lambda b, pt: (b, 0, 0)),
            pl.BlockSpec((1, IDX_HEADS, 1), lambda b, pt: (b, 0, 0)),
            pl.BlockSpec(memory_space=pl.ANY),
        ],
        out_specs=pl.BlockSpec((1, n_pages, PAGE_SIZE), lambda b, pt: (b, 0, 0)),
        scratch_shapes=[pltpu.VMEM((2, IDX_DIM, n_pages * PAGE_SIZE), f32), pltpu.SemaphoreType.DMA((2,))],
    )
    return pl.pallas_call(
        kern,
        grid_spec=grid_spec,
        out_shape=jax.ShapeDtypeStruct((db, n_pages, PAGE_SIZE), f32),
        compiler_params=_params(("arbitrary",)),
    )(page_table, qi, w, pool_ikt)


def _dec_select_kernel(s_ref, qi_ref, w_ref, ikn_ref, idx_ref, cnt_ref, nsel_ref, rank_ref,
                       *, n_keep, kmax):
    db, n_pages, _ = s_ref.shape
    ikn = ikn_ref[...].astype(bf16).astype(f32)
    dots = jnp.sum(qi_ref[...].astype(f32) * ikn, axis=2, keepdims=True)
    s_new = jnp.sum(jnp.maximum(dots, 0.0) * w_ref[...], axis=1, keepdims=True)
    keys = _sortable_key(s_ref[...])
    key_new = _sortable_key(s_new)

    def count_ge(t):
        c = jnp.sum(jnp.where(keys >= t, 1.0, 0.0), axis=1, keepdims=True)
        return jnp.sum(c, axis=2, keepdims=True) + jnp.where(key_new >= t, 1.0, 0.0)

    thr = _kth_largest_key(count_ge, n_keep, (db, 1, 1))
    sel = jnp.where(keys >= thr, 1.0, 0.0)
    nsel_ref[...] = jnp.broadcast_to(jnp.where(key_new >= thr, 1.0, 0.0), (db, 1, LANES))

    pp = lax.broadcasted_iota(jnp.int32, (n_pages, n_pages), 0)
    pq = lax.broadcasted_iota(jnp.int32, (n_pages, n_pages), 1)
    earlier_page = jnp.where(pq < pp, 1.0, 0.0).astype(bf16)
    oo = lax.broadcasted_iota(jnp.int32, (PAGE_SIZE, PAGE_SIZE), 0)
    oq = lax.broadcasted_iota(jnp.int32, (PAGE_SIZE, PAGE_SIZE), 1)
    earlier_off = jnp.where(oo < oq, 1.0, 0.0).astype(bf16)
    r_iota = lax.broadcasted_iota(jnp.int32, (kmax, PAGE_SIZE), 0).astype(f32)
    lane = lax.broadcasted_iota(jnp.int32, (kmax, PAGE_SIZE), 1).astype(f32)
    for b in range(db):
        sb = sel[b]
        per_page = jnp.sum(sb, axis=1, keepdims=True)
        before = _dot(earlier_page, jnp.broadcast_to(per_page, (n_pages, PAGE_SIZE)).astype(bf16))
        within = _dot(sb.astype(bf16), earlier_off)
        rank_ref[b] = jnp.where(sb > 0.0, before + within, -1.0)

        def body(p, acc, b=b):
            row = rank_ref[b, pl.ds(p, 1), :]
            return jnp.where(row == r_iota, lane + jnp.asarray(p * PAGE_SIZE, f32), acc)

        acc = lax.fori_loop(0, n_pages, body, jnp.zeros((kmax, PAGE_SIZE), f32))
        idx_ref[b] = jnp.sum(acc, axis=1, keepdims=True).astype(jnp.int32)
        total = jnp.sum(per_page, axis=0, keepdims=True)
        cnt_ref[b] = jnp.broadcast_to(total, (1, LANES)).astype(jnp.int32)


def _decode_select(scores, qi, w, ik_new, n_keep):
    db, n_pages, _ = scores.shape
    kmax = -(-n_keep // 8) * 8
    kern = functools.partial(_dec_select_kernel, n_keep=n_keep, kmax=kmax)
    full = lambda shape: pl.BlockSpec(shape, lambda i: (0,) * len(shape))
    return pl.pallas_call(
        kern,
        grid=(1,),
        in_specs=[full(scores.shape), full(qi.shape), full(w.shape), full(ik_new.shape)],
        out_specs=[full((db, kmax, 1)), full((db, 1, LANES)), full((db, 1, LANES))],
        out_shape=[jax.ShapeDtypeStruct((db, kmax, 1), jnp.int32),
                   jax.ShapeDtypeStruct((db, 1, LANES), jnp.int32),
                   jax.ShapeDtypeStruct((db, 1, LANES), f32)],
        scratch_shapes=[pltpu.VMEM((db, n_pages, PAGE_SIZE), f32)],
        compiler_params=_params(("arbitrary",)),
    )(scores, qi, w, ik_new)


def _dec_attn_kernel(idx_ref, pt_ref, cnt_ref, q_ref, kn_ref, vn_ref, nsel_ref, kpool_ref, vpool_ref, o_ref,
                     kbuf_ref, vbuf_ref, sem_ref, *, kmax):
    b = pl.program_id(0)
    slot = b % 2
    page_shift = PAGE_SIZE.bit_length() - 1

    def row_copies(seq, r, slot_):
        pos = idx_ref[seq, r]
        phys = pt_ref[seq, lax.shift_right_logical(pos, page_shift)]
        src0 = pl.multiple_of((pos & (PAGE_SIZE - 1)) * ATT_KV_HEADS, ATT_KV_HEADS)
        dst0 = pl.multiple_of(r * ATT_KV_HEADS, ATT_KV_HEADS)
        return (pltpu.make_async_copy(kpool_ref.at[phys, pl.ds(src0, ATT_KV_HEADS), :],
                                      kbuf_ref.at[slot_, pl.ds(dst0, ATT_KV_HEADS), :], sem_ref.at[0, slot_]),
                pltpu.make_async_copy(vpool_ref.at[phys, pl.ds(src0, ATT_KV_HEADS), :],
                                      vbuf_ref.at[slot_, pl.ds(dst0, ATT_KV_HEADS), :], sem_ref.at[1, slot_]))

    def start_all(seq, slot_):
        def body(r, c):
            ck, cv = row_copies(seq, r, slot_)
            ck.start()
            cv.start()
            return c
        lax.fori_loop(0, kmax, body, 0)

    @pl.when(b == 0)
    def _():
        start_all(0, 0)

    @pl.when(b + 1 < pl.num_programs(0))
    def _():
        start_all(b + 1, 1 - slot)

    def wait_body(r, c):
        ck, cv = row_copies(b, r, slot)
        ck.wait()
        cv.wait()
        return c
    lax.fori_loop(0, kmax, wait_body, 0)

    q = q_ref[0]
    s = _dot_nt(q, kbuf_ref[slot].astype(bf16))
    col = lax.broadcasted_iota(jnp.int32, s.shape, 1)
    head = lax.broadcasted_iota(jnp.int32, s.shape, 0)
    ok = (col % ATT_KV_HEADS == head // GROUP) & (col // ATT_KV_HEADS < cnt_ref[b])
    s = jnp.where(ok, s, -jnp.inf)
    s_n = jnp.sum(q.astype(f32) * kn_ref[0].astype(f32), axis=1, keepdims=True)
    s_n = jnp.where(nsel_ref[0][:, 0:1] > 0.0, s_n, -jnp.inf)
    m = jnp.maximum(jnp.max(s, axis=1, keepdims=True), s_n)
    p = jnp.exp2(s - m)
    p_n = jnp.exp2(s_n - m)
    l = jnp.sum(p, axis=1, keepdims=True) + p_n
    pv = _dot(p.astype(bf16), vbuf_ref[slot].astype(bf16))
    o_ref[0] = (pv + p_n.astype(bf16).astype(f32) * vn_ref[0].astype(f32)) / l


def _decode_attention(idx, page_table, cnt, q, k_new, v_new, nsel, pool_k, pool_v):
    db, kmax = idx.shape
    kern = functools.partial(_dec_attn_kernel, kmax=kmax)
    seqb = lambda shape: pl.BlockSpec(shape, lambda b, *_: (b, 0, 0))
    grid_spec = pltpu.PrefetchScalarGridSpec(
        num_scalar_prefetch=3,
        grid=(db,),
        in_specs=[
            seqb((1, ATT_HEADS, HEAD_DIM)),
            seqb((1, ATT_HEADS, HEAD_DIM)),
            seqb((1, ATT_HEADS, HEAD_DIM)),
            seqb((1, 1, LANES)),
            pl.BlockSpec(memory_space=pl.ANY),
            pl.BlockSpec(memory_space=pl.ANY),
        ],
        out_specs=seqb((1, ATT_HEADS, HEAD_DIM)),
        scratch_shapes=[pltpu.VMEM((2, kmax * ATT_KV_HEADS, HEAD_DIM), f32),
                        pltpu.VMEM((2, kmax * ATT_KV_HEADS, HEAD_DIM), f32),
                        pltpu.SemaphoreType.DMA((2, 2))],
    )
    return pl.pallas_call(
        kern,
        grid_spec=grid_spec,
        out_shape=jax.ShapeDtypeStruct((db, ATT_HEADS, HEAD_DIM), f32),
        compiler_params=_params(("arbitrary",)),
    )(idx, page_table, cnt, q, k_new, v_new, nsel, pool_k, pool_v)


def _rope_tables(pos):
    half = RET_QK_DIM // 2
    inv = ROPE_BASE ** (-jnp.arange(half, dtype=f32) / half)
    ang = pos.astype(f32)[:, None] * inv[None, :]
    cos, sin = jnp.cos(ang), jnp.sin(ang)
    return jnp.concatenate([cos, cos], axis=1), jnp.concatenate([-sin, sin], axis=1)


def _pick(n, pref):
    for t in pref:
        if n % t == 0:
            return t
    return n


def kernel(x_prompt, x_sample, state_ret, cache_k, cache_v, cache_idx_k, page_table, ffn1_norm, ffn1_w1, ffn1_w2, mix_norm, w_in, q_norm, k_norm, ret_norm, w_ret_out, w_att_out, w_o, ffn2_norm, ffn2_w1, ffn2_w2):
    batch, seq, d = x_prompt.shape
    db, ts, _ = x_sample.shape
    depth = w_in.shape[0]
    assert depth == 1 and ts == 1 and d % (2 * TN) == 0
    n_pages = page_table.shape[1]
    past = n_pages * PAGE_SIZE
    n_phys = cache_k.shape[1]
    base = 2 * (d // TN)
    m = batch * seq
    ms = 16

    log_g = jnp.log1p(-jnp.exp2(-5.0 - jnp.arange(RET_HEADS, dtype=f32)))
    w1a, w2a = ffn1_w1[0].astype(bf16), ffn1_w2[0].astype(bf16)
    w1b, w2b = ffn2_w1[0].astype(bf16), ffn2_w2[0].astype(bf16)
    w_pad = jnp.swapaxes(w_in[0], 0, 1).astype(bf16)
    wr, wa, wo = w_ret_out[0].astype(bf16), w_att_out[0].astype(bf16), w_o[0].astype(bf16)
    g1, g2, g3 = ffn1_norm, mix_norm, ffn2_norm
    qn, kn, gn = q_norm, k_norm, ret_norm

    tm = _pick(m, (1024, 512, 256, 128))
    tf = _pick(ffn1_w2.shape[1], (512, 256, 128))
    tmp = _pick(seq, (1024, 512, 256, 128))
    chunk = _pick(seq, (256, 128))
    tq = _pick(seq, (256, 128))
    tmo = _pick(m, (256, 128))

    xp = x_prompt.reshape(m, d)
    x1 = _ffn(xp, g1, w1a, w2a, tm, tf)
    cos_p, sin_p = _rope_tables(jnp.arange(seq))
    p, k32, v32, ikw = _proj(x1, g2, w_pad, cos_p, sin_p, qn, kn, tmp)
    og, st_p = _retention(p, log_g, gn, batch, seq, base, chunk)
    att = _prompt_attention(p, ikw, batch, seq, base, tq, min(TOPK_MAX, seq // 4))
    x2 = _oproj(og, att, p, x1, wr, wa, wo, tmo)
    yp = _ffn(x2, g3, w1b, w2b, tm, tf).reshape(batch, seq, d)

    xs = jnp.zeros((ms, d), f32).at[:db].set(x_sample.reshape(db, d))
    s1 = _ffn(xs, g1, w1a, w2a, ms, tf)
    cos_s, sin_s = _rope_tables(jnp.full((ms,), past, jnp.int32))
    ps, k32s, v32s, ikws = _proj(s1, g2, w_pad, cos_s, sin_s, qn, kn, ms)
    seg = lambda t0, t1: ps[:db, (base + t0) * TN:(base + t1) * TN]
    row3 = lambda a: a.astype(f32).reshape(db, 1, a.shape[-1])
    og_s, st_s = _decode_retention(row3(seg(T_RQ, T_RK)), row3(seg(T_RK, T_RV)), row3(seg(T_RV, T_RG)),
                                   row3(seg(T_RG, T_AQ)), gn, state_ret, log_g)
    qi = seg(T_IQ, T_IKW).reshape(db, IDX_HEADS, IDX_DIM)
    wi = (ikws[:db, IDX_DIM:IDX_DIM + IDX_HEADS] * IDX_W_SCALE).reshape(db, IDX_HEADS, 1)
    ik_new = ikws[:db, :IDX_DIM].reshape(db, 1, IDX_DIM)
    pool_ikt = jnp.swapaxes(cache_idx_k.reshape(n_phys, PAGE_SIZE, IDX_DIM), 1, 2)
    scores = _decode_scores(page_table, qi, wi, pool_ikt)
    idx, cnt, nsel = _decode_select(scores, qi, wi, ik_new, min(TOPK_MAX, (past + ts) // 4))
    aq_s = seg(T_AQ, T_AK).reshape(db, ATT_HEADS, HEAD_DIM)
    expand = lambda a: jnp.repeat(a[:db].reshape(db, ATT_KV_HEADS, HEAD_DIM), GROUP, axis=1).astype(bf16)
    pool_rows = lambda c: c.reshape(n_phys, PAGE_SIZE * ATT_KV_HEADS, HEAD_DIM)
    att_s = _decode_attention(idx.reshape(db, -1), page_table, cnt[:, 0, 0], aq_s, expand(k32s), expand(v32s),
                              nsel, pool_rows(cache_k), pool_rows(cache_v))
    pad_rows = lambda a: jnp.pad(a.reshape(db, -1).astype(bf16), ((0, ms - db), (0, 0)))
    s2 = _oproj(pad_rows(og_s), pad_rows(att_s), ps, s1, wr, wa, wo, ms)
    ys = _ffn(s2, g3, w1b, w2b, ms, tf)[:db].reshape(db, ts, d)

    return (yp, ys,
            st_p[None],
            k32.reshape(1, batch, seq, ATT_KV_HEADS, HEAD_DIM),
            v32.reshape(1, batch, seq, ATT_KV_HEADS, HEAD_DIM),
            ikw[:, :IDX_DIM].reshape(1, batch, seq, IDX_DIM),
            st_s[None],
            k32s[:db].reshape(1, db, ts, ATT_KV_HEADS, HEAD_DIM),
            v32s[:db].reshape(1, db, ts, ATT_KV_HEADS, HEAD_DIM),
            ikws[:db, :IDX_DIM].reshape(1, db, ts, IDX_DIM))
```

```python
import functools

import jax
import jax.numpy as jnp
from jax import lax
from jax.experimental import pallas as pl
from jax.experimental.pallas import tpu as pltpu

RET_HEADS = 8
RET_QK_DIM = 128
RET_V_DIM = 256
ATT_HEADS = 16
ATT_KV_HEADS = 4
HEAD_DIM = 128
GROUP = ATT_HEADS // ATT_KV_HEADS
IDX_HEADS = 16
IDX_DIM = 64
IDX_W_SCALE = (IDX_HEADS ** -0.5) * (IDX_DIM ** -0.5)
TOPK_MAX = 256
PAGE_SIZE = 128
ROPE_BASE = 10000.0
EPS = 1e-6

RET_QK_W = RET_HEADS * RET_QK_DIM
RET_V_W = RET_HEADS * RET_V_DIM
ATT_Q_W = ATT_HEADS * HEAD_DIM
ATT_KV_W = ATT_KV_HEADS * HEAD_DIM
IDX_Q_W = IDX_HEADS * IDX_DIM
MAIN_W = 2 * RET_QK_W + 2 * RET_V_W + ATT_Q_W + 2 * ATT_KV_W + IDX_Q_W

LANES = 128
TN = 512
T_RQ, T_RK, T_RV, T_RG, T_AQ, T_AK, T_AV, T_IQ, T_IKW, T_END = 0, 2, 4, 8, 12, 16, 17, 18, 20, 21
VMEM_LIMIT = 56 * 1024 * 1024
INT_MIN = -2 ** 31
I16_MIN, I16_MAX = -2 ** 15, 2 ** 15 - 1
NEG_BIG = -1e30
LOG2E = 1.4426950408889634
VT_PAD = 16

bf16 = jnp.bfloat16
f32 = jnp.float32


def _params(sem, vmem=VMEM_LIMIT):
    return pltpu.CompilerParams(dimension_semantics=sem, vmem_limit_bytes=vmem)


def _sigmoid(x):
    return 1.0 / (1.0 + jnp.exp(-x))


def _dot(a, b):
    return jnp.dot(a, b, preferred_element_type=f32)


def _dot_nt(a, b):
    return lax.dot_general(a, b, (((1,), (1,)), ((), ())), preferred_element_type=f32)


def _rms(x, gain):
    ms = jnp.mean(x * x, axis=-1, keepdims=True)
    return x * lax.rsqrt(ms + EPS) * gain


def _ffn_kernel(x_ref, g_ref, w1g_ref, w1u_ref, w2_ref, o_ref, h_ref):
    j = pl.program_id(1)

    @pl.when(j == 0)
    def _():
        h_ref[...] = _rms(x_ref[...], g_ref[...]).astype(bf16)
        o_ref[...] = jnp.zeros_like(o_ref)

    h = h_ref[...]
    gate = _dot(h, w1g_ref[...].astype(bf16))
    up = _dot(h, w1u_ref[...].astype(bf16))
    act = (gate * _sigmoid(gate) * up).astype(bf16)
    o_ref[...] += _dot(act, w2_ref[...].astype(bf16))

    @pl.when(j == pl.num_programs(1) - 1)
    def _():
        o_ref[...] = x_ref[...] + 0.5 * o_ref[...]


def _ffn(x, gain, w1, w2, tm, tf):
    m, d = x.shape
    dff = w2.shape[0]
    nf = dff // tf
    return pl.pallas_call(
        _ffn_kernel,
        grid=(m // tm, nf),
        in_specs=[
            pl.BlockSpec((tm, d), lambda i, j: (i, 0)),
            pl.BlockSpec((1, d), lambda i, j: (0, 0)),
            pl.BlockSpec((d, tf), lambda i, j: (0, j)),
            pl.BlockSpec((d, tf), lambda i, j: (0, j + nf)),
            pl.BlockSpec((tf, d), lambda i, j: (j, 0)),
        ],
        out_specs=pl.BlockSpec((tm, d), lambda i, j: (i, 0)),
        out_shape=jax.ShapeDtypeStruct((m, d), f32),
        scratch_shapes=[pltpu.VMEM((tm, d), bf16)],
        compiler_params=_params(("arbitrary", "arbitrary")),
    )(x, gain, w1, w1, w2)


def _proj_kernel(x_ref, g_ref, w_ref, cos_ref, sin_ref, qn_ref, kn_ref,
                 p_ref, k32_ref, v32_ref, ikw_ref, h_ref, *, base):
    j = pl.program_id(1)

    @pl.when(j == 0)
    def _():
        h_ref[...] = _rms(x_ref[...], g_ref[...]).astype(bf16)

    w = w_ref[...].astype(bf16)
    heads = [slice(c * LANES, (c + 1) * LANES) for c in range(TN // LANES)]
    tm = h_ref.shape[0]
    rc = min(tm, 256)

    def segment(cond, epilogue):
        @pl.when(cond)
        def _():
            for r0 in range(0, tm, rc):
                rows = slice(r0, r0 + rc)
                epilogue(_dot_nt(h_ref[rows, :], w), rows)

    def gates(res, rows):
        p_ref[rows, :] = _sigmoid(res).astype(bf16)

    def rotary(res, rows):
        cos, sin = cos_ref[rows, :], sin_ref[rows, :]
        scale = jnp.where(j >= base + T_RK, RET_QK_DIM ** -0.5, 1.0).astype(f32)
        for sl in heads:
            x = res[:, sl]
            p_ref[rows, sl] = ((x * cos + pltpu.roll(x, LANES // 2, 1) * sin) * scale).astype(bf16)

    def plain(res, rows):
        p_ref[rows, :] = res.astype(bf16)

    def silu(res, rows):
        p_ref[rows, :] = (res * _sigmoid(res)).astype(bf16)

    def q_norm(res, rows):
        for sl in heads:
            p_ref[rows, sl] = (_rms(res[:, sl], qn_ref[...]) * (HEAD_DIM ** -0.5 * LOG2E)).astype(bf16)

    def k_norm(res, rows):
        for sl in heads:
            y = _rms(res[:, sl], kn_ref[...])
            k32_ref[rows, sl] = y
            p_ref[rows, sl] = y.astype(bf16)

    def value(res, rows):
        v32_ref[rows, :] = res
        p_ref[rows, :] = res.astype(bf16)

    def idx_kw(res, rows):
        ikw_ref[rows, :] = res[:, :LANES]
        p_ref[rows, :] = res.astype(bf16)

    segment(j < base, gates)
    segment((j >= base + T_RQ) & (j < base + T_RV), rotary)
    segment(((j >= base + T_RV) & (j < base + T_RG)) | ((j >= base + T_IQ) & (j < base + T_IKW)), plain)
    segment((j >= base + T_RG) & (j < base + T_AQ), silu)
    segment((j >= base + T_AQ) & (j < base + T_AK), q_norm)
    segment(j == base + T_AK, k_norm)
    segment(j == base + T_AV, value)
    segment(j == base + T_IKW, idx_kw)


def _proj(x, gain, w_t, cos2, sin2, qn, kn, tm):
    m, d = x.shape
    gate_row0 = MAIN_W + IDX_DIM + IDX_HEADS
    base = (w_t.shape[0] - gate_row0) // TN
    nt = base + T_END
    tab_blocks = cos2.shape[0] // tm
    kern = functools.partial(_proj_kernel, base=base)
    al = 16
    assert gate_row0 % al == 0
    w_row = lambda j: al * jnp.where(j < base, gate_row0 // al + j * (TN // al), (j - base) * (TN // al))
    return pl.pallas_call(
        kern,
        grid=(m // tm, nt),
        in_specs=[
            pl.BlockSpec((tm, d), lambda i, j: (i, 0)),
            pl.BlockSpec((1, d), lambda i, j: (0, 0)),
            pl.BlockSpec((pl.Element(TN), pl.Element(d)), lambda i, j: (w_row(j), 0)),
            pl.BlockSpec((tm, LANES), lambda i, j: (i % tab_blocks, 0)),
            pl.BlockSpec((tm, LANES), lambda i, j: (i % tab_blocks, 0)),
            pl.BlockSpec((1, LANES), lambda i, j: (0, 0)),
            pl.BlockSpec((1, LANES), lambda i, j: (0, 0)),
        ],
        out_specs=[
            pl.BlockSpec((tm, TN), lambda i, j: (i, j)),
            pl.BlockSpec((tm, TN), lambda i, j: (i, 0)),
            pl.BlockSpec((tm, TN), lambda i, j: (i, 0)),
            pl.BlockSpec((tm, LANES), lambda i, j: (i, 0)),
        ],
        out_shape=[
            jax.ShapeDtypeStruct((m, nt * TN), bf16),
            jax.ShapeDtypeStruct((m, ATT_KV_W), f32),
            jax.ShapeDtypeStruct((m, ATT_KV_W), f32),
            jax.ShapeDtypeStruct((m, LANES), f32),
        ],
        scratch_shapes=[pltpu.VMEM((tm, d), bf16)],
        compiler_params=_params(("arbitrary", "arbitrary")),
    )(x, gain, w_t, cos2, sin2, qn, kn)


def _ret_kernel(lg_ref, q_ref, k_ref, v_ref, sg_ref, gn_ref, og_ref, st_ref, state_ref, *, chunk):
    c = pl.program_id(1)

    @pl.when(c == 0)
    def _():
        state_ref[...] = jnp.zeros_like(state_ref)

    ii = lax.broadcasted_iota(jnp.int32, (chunk, chunk), 0)
    jj = lax.broadcasted_iota(jnp.int32, (chunk, chunk), 1)
    rel = (ii - jj).astype(f32)
    i1 = lax.broadcasted_iota(jnp.int32, (chunk, 1), 0).astype(f32)
    for h in range(RET_HEADS):
        lg = lg_ref[h]
        qk_cols = slice(h * RET_QK_DIM, (h + 1) * RET_QK_DIM)
        v_cols = slice(h * RET_V_DIM, (h + 1) * RET_V_DIM)
        q, k, v = q_ref[:, qk_cols], k_ref[:, qk_cols], v_ref[:, v_cols]
        dmat = jnp.where(rel >= 0, jnp.exp(lg * jnp.maximum(rel, 0.0)), 0.0)
        scores = _dot_nt(q, k) * dmat
        inner = _dot(scores.astype(bf16), v)
        q_dec = jnp.exp(lg * (i1 + 1.0))
        k_dec = jnp.exp(lg * (chunk - 1.0 - i1))
        state = state_ref[h]
        cross = _dot((q.astype(f32) * q_dec).astype(bf16), state.astype(bf16))
        kd = (k.astype(f32) * k_dec).astype(bf16)
        new_state = jnp.exp(lg * chunk) * state + lax.dot_general(
            kd, v, (((0,), (0,)), ((), ())), preferred_element_type=f32)
        state_ref[h] = new_state
        o = _rms(inner + cross, gn_ref[...])
        og_ref[:, v_cols] = (o * sg_ref[:, v_cols].astype(f32)).astype(bf16)

    @pl.when(c == pl.num_programs(1) - 1)
    def _():
        st_ref[0] = state_ref[...]


def _retention(p, log_g, gn, batch, seq, base, chunk):
    nc = seq // chunk
    qb = (base + T_RQ) * TN // RET_QK_W
    kb = (base + T_RK) * TN // RET_QK_W
    vb = (base + T_RV) * TN // RET_V_W
    gb = (base + T_RG) * TN // RET_V_W
    kern = functools.partial(_ret_kernel, chunk=chunk)
    grid_spec = pltpu.PrefetchScalarGridSpec(
        num_scalar_prefetch=1,
        grid=(batch, nc),
        in_specs=[
            pl.BlockSpec((chunk, RET_QK_W), lambda b, c, lg: (b * nc + c, qb)),
            pl.BlockSpec((chunk, RET_QK_W), lambda b, c, lg: (b * nc + c, kb)),
            pl.BlockSpec((chunk, RET_V_W), lambda b, c, lg: (b * nc + c, vb)),
            pl.BlockSpec((chunk, RET_V_W), lambda b, c, lg: (b * nc + c, gb)),
            pl.BlockSpec((1, RET_V_DIM), lambda b, c, lg: (0, 0)),
        ],
        out_specs=[
            pl.BlockSpec((chunk, RET_V_W), lambda b, c, lg: (b * nc + c, 0)),
            pl.BlockSpec((1, RET_HEADS, RET_QK_DIM, RET_V_DIM), lambda b, c, lg: (b, 0, 0, 0)),
        ],
        scratch_shapes=[pltpu.VMEM((RET_HEADS, RET_QK_DIM, RET_V_DIM), f32)],
    )
    return pl.pallas_call(
        kern,
        grid_spec=grid_spec,
        out_shape=[
            jax.ShapeDtypeStruct((batch * seq, RET_V_W), bf16),
            jax.ShapeDtypeStruct((batch, RET_HEADS, RET_QK_DIM, RET_V_DIM), f32),
        ],
        compiler_params=_params(("arbitrary", "arbitrary")),
    )(log_g, p, p, p, p, gn)


def _sortable_key(score):
    bits = lax.bitcast_convert_type(score, jnp.int32)
    return jnp.where(bits >= 0, bits, bits ^ jnp.int32(0x7FFFFFFF))


def _kth_largest_key(count_ge, n_keep, shape):
    def body(it, t_u):
        bit = jnp.left_shift(jnp.int32(1), 31 - it)
        cand_u = t_u | bit
        cnt = count_ge(cand_u ^ jnp.int32(INT_MIN))
        return jnp.where(cnt >= n_keep, cand_u, t_u)
    t_u = lax.fori_loop(0, 32, body, jnp.zeros(shape, jnp.int32))
    return t_u ^ jnp.int32(INT_MIN)


def _kth_largest_16(count_ge, n_keep, shape):
    def body(it, t_u):
        cand_u = t_u | jnp.left_shift(jnp.int32(1), 15 - it)
        cnt = count_ge(cand_u + I16_MIN)
        return jnp.where(cnt >= n_keep, cand_u, t_u)
    t_u = lax.fori_loop(0, 16, body, jnp.zeros(shape, jnp.int32))
    return t_u + I16_MIN


def _attn_kernel(aq_ref, k_ref, v_ref, iq_ref, ikw_k_ref, ikw_q_ref, o_ref,
                 key_ref, hi_ref, lo_ref, iklo_ref, ikhi_ref, vt_ref, m_ref, acc_ref, p_ref, alpha_ref,
                 *, tq, ck, seq, n_keep):
    qb = pl.program_id(1)

    @pl.when(qb == 0)
    def _():
        ikw = ikw_k_ref[...]
        lane = lax.broadcasted_iota(jnp.int32, (seq, LANES), 1)
        iklo_ref[...] = jnp.where(lane < IDX_DIM, ikw, 0.0).astype(bf16)
        ikhi_ref[...] = jnp.where(lane >= IDX_DIM, pltpu.roll(ikw, IDX_DIM, 1), 0.0).astype(bf16)

        for c0 in range(0, seq, ck):
            v_t = jnp.transpose(v_ref[c0:c0 + ck, :].astype(f32)).astype(bf16)
            for g in range(ATT_KV_HEADS):
                r0 = g * (HEAD_DIM + VT_PAD)
                vt_ref[r0:r0 + HEAD_DIM, c0:c0 + ck] = v_t[g * HEAD_DIM:(g + 1) * HEAD_DIM]
                vt_ref[r0 + HEAD_DIM:r0 + HEAD_DIM + VT_PAD, c0:c0 + ck] = jnp.ones((VT_PAD, ck), bf16)

    n_chunks = (qb * tq + tq - 1) // ck + 1
    chunk = lambda kc: pl.ds(pl.multiple_of(kc * ck, ck), ck)
    stack = 4
    pairs = IDX_HEADS // 2

    iw_t = jnp.transpose(ikw_q_ref[...]) * IDX_W_SCALE
    q_stacks = [jnp.concatenate([iq_ref[:, hp * LANES:(hp + 1) * LANES] for hp in range(h0, h0 + stack)], axis=0)
                for h0 in range(0, pairs, stack)]
    k_off = lax.broadcasted_iota(jnp.int32, (ck, tq), 0)
    q_pos = qb * tq + lax.broadcasted_iota(jnp.int32, (ck, tq), 1)

    def score_body(kc, c):
        scores = jnp.zeros((ck, tq), f32)
        for si, qs in enumerate(q_stacks):
            for sub, ik_ref in enumerate((iklo_ref, ikhi_ref)):
                s = jnp.maximum(_dot_nt(ik_ref[chunk(kc), :], qs), 0.0)
                for i in range(stack):
                    row = IDX_DIM + 2 * (si * stack + i) + sub
                    scores = scores + s[:, i * tq:(i + 1) * tq] * iw_t[row:row + 1, :]
        visible = kc * ck + k_off <= q_pos
        key_ref[chunk(kc), :] = jnp.where(visible, _sortable_key(scores), jnp.int32(INT_MIN))
        return c
    lax.fori_loop(0, n_chunks, score_body, 0)

    i16 = jnp.int16
    sub = 16

    def count16_ge(half_ref):
        def count(t):
            t_b = jnp.broadcast_to(t, (sub, tq)).astype(i16)

            def body(kc, acc):
                x = half_ref[chunk(kc), :]
                for r0 in range(0, ck, sub):
                    acc = acc + jnp.where(x[r0:r0 + sub] >= t_b, i16(1), i16(0))
                return acc
            acc = lax.fori_loop(0, n_chunks, body, jnp.zeros((sub, tq), i16))
            return jnp.sum(acc.astype(f32), axis=0, keepdims=True)
        return count

    def hi_body(kc, c):
        hi_ref[chunk(kc), :] = lax.shift_right_arithmetic(key_ref[chunk(kc), :], 16).astype(i16)
        return c
    lax.fori_loop(0, n_chunks, hi_body, 0)
    count_hi = count16_ge(hi_ref)
    hi_thr = _kth_largest_16(count_hi, float(n_keep), (1, tq))
    above = jnp.where(hi_thr >= I16_MAX, 0.0, count_hi(jnp.minimum(hi_thr + 1, I16_MAX)))

    def lo_body(kc, c):
        keys = key_ref[chunk(kc), :]
        lo = ((keys & 0xFFFF) - (I16_MAX + 1)).astype(i16)
        same_hi = lax.shift_right_arithmetic(keys, 16) == hi_thr
        lo_ref[chunk(kc), :] = jnp.where(same_hi, lo, i16(I16_MIN))
        return c
    lax.fori_loop(0, n_chunks, lo_body, 0)
    lo_thr = _kth_largest_16(count16_ge(lo_ref), n_keep - above, (1, tq))
    thr = hi_thr * 65536 + (lo_thr + (I16_MAX + 1))
    thr = jnp.maximum(thr, jnp.int32(INT_MIN + 1))

    m_ref[...] = jnp.full_like(m_ref, NEG_BIG)
    acc_ref[...] = jnp.zeros_like(acc_ref)
    hs = GROUP
    q_sets = [jnp.concatenate([aq_ref[:, hd * HEAD_DIM:(hd + 1) * HEAD_DIM] for hd in range(h0, h0 + hs)], axis=0)
              for h0 in range(0, ATT_HEADS, hs)]

    p_ref[...] = jnp.zeros_like(p_ref)
    alpha_ref[...] = jnp.ones_like(alpha_ref)

    def apply_pv(kc, si):
        g = si * hs // GROUP
        cols = slice(si * hs * tq, (si + 1) * hs * tq)
        r0 = g * (HEAD_DIM + VT_PAD)
        vt = vt_ref[r0:r0 + HEAD_DIM + VT_PAD, chunk(kc)]
        acc_ref[:, cols] = alpha_ref[:, cols] * acc_ref[:, cols] + _dot(vt, p_ref[:, cols])

    def attn_body(kc, c):
        bias = jnp.where(key_ref[chunk(kc), :] >= thr, 0.0, -jnp.inf)
        bias = jnp.concatenate([bias] * hs, axis=1)
        prev = jnp.maximum(kc - 1, 0)
        for si, qs in enumerate(q_sets):
            g = si * hs // GROUP
            cols = slice(si * hs * tq, (si + 1) * hs * tq)
            apply_pv(prev, si)
            s = _dot_nt(k_ref[chunk(kc), g * HEAD_DIM:(g + 1) * HEAD_DIM], qs) + bias
            m_old = m_ref[:, cols]
            m_new = jnp.maximum(m_old, jnp.max(s, axis=0, keepdims=True))
            p_ref[:, cols] = jnp.exp2(s - m_new).astype(bf16)
            alpha_ref[:, cols] = jnp.exp2(m_old - m_new)
            m_ref[:, cols] = m_new
        return c
    lax.fori_loop(0, n_chunks, attn_body, 0)
    for si in range(len(q_sets)):
        apply_pv(n_chunks - 1, si)

    for hd in range(ATT_HEADS):
        cols = slice(hd * tq, (hd + 1) * tq)
        o_t = acc_ref[:HEAD_DIM, cols] / acc_ref[HEAD_DIM:HEAD_DIM + 1, cols]
        o_ref[:, hd * HEAD_DIM:(hd + 1) * HEAD_DIM] = jnp.transpose(o_t).astype(bf16)


def _prompt_attention(p, ikw, batch, seq, base, tq, n_keep):
    nq = seq // tq
    ck = tq
    kern = functools.partial(_attn_kernel, tq=tq, ck=ck, seq=seq, n_keep=n_keep)
    aq_blk = (base + T_AQ) * TN // ATT_Q_W
    iq_blk = (base + T_IQ) * TN // IDX_Q_W
    return pl.pallas_call(
        kern,
        grid=(batch, nq),
        in_specs=[
            pl.BlockSpec((tq, ATT_Q_W), lambda b, t: (b * nq + t, aq_blk)),
            pl.BlockSpec((seq, ATT_KV_W), lambda b, t: (b, base + T_AK)),
            pl.BlockSpec((seq, ATT_KV_W), lambda b, t: (b, base + T_AV)),
            pl.BlockSpec((tq, IDX_Q_W), lambda b, t: (b * nq + t, iq_blk)),
            pl.BlockSpec((seq, LANES), lambda b, t: (b, 0)),
            pl.BlockSpec((tq, LANES), lambda b, t: (b * nq + t, 0)),
        ],
        out_specs=pl.BlockSpec((tq, ATT_Q_W), lambda b, t: (b * nq + t, 0)),
        out_shape=jax.ShapeDtypeStruct((batch * seq, ATT_Q_W), bf16),
        scratch_shapes=[pltpu.VMEM((seq, tq), jnp.int32), pltpu.VMEM((seq, tq), jnp.int16),
                        pltpu.VMEM((seq, tq), jnp.int16), pltpu.VMEM((seq, LANES), bf16),
                        pltpu.VMEM((seq, LANES), bf16),
                        pltpu.VMEM((ATT_KV_HEADS * (HEAD_DIM + VT_PAD), seq), bf16),
                        pltpu.VMEM((1, ATT_HEADS * tq), f32),
                        pltpu.VMEM((HEAD_DIM + VT_PAD, ATT_HEADS * tq), f32),
                        pltpu.VMEM((ck, ATT_HEADS * tq), bf16), pltpu.VMEM((1, ATT_HEADS * tq), f32)],
        compiler_params=_params(("arbitrary", "arbitrary")),
    )(p, p, p, p, ikw, ikw)


def _oproj_kernel(og_ref, att_ref, gr_ref, ga_ref, x_ref, wr_ref, wa_ref, wo_ref, o_ref):
    ret_out = _dot(og_ref[...], wr_ref[...])
    att_out = _dot(att_ref[...], wa_ref[...])
    merged = gr_ref[...].astype(f32) * ret_out + ga_ref[...].astype(f32) * att_out
    o_ref[...] = x_ref[...] + _dot(merged.astype(bf16), wo_ref[...])


def _oproj(og, att, p, x, wr, wa, wo, tm):
    m, d = x.shape
    const = lambda shape: pl.BlockSpec(shape, lambda i: (0, 0), pipeline_mode=pl.Buffered(1))
    return pl.pallas_call(
        _oproj_kernel,
        grid=(m // tm,),
        in_specs=[
            pl.BlockSpec((tm, RET_V_W), lambda i: (i, 0)),
            pl.BlockSpec((tm, ATT_Q_W), lambda i: (i, 0)),
            pl.BlockSpec((tm, d), lambda i: (i, 0)),
            pl.BlockSpec((tm, d), lambda i: (i, 1)),
            pl.BlockSpec((tm, d), lambda i: (i, 0)),
            const(wr.shape), const(wa.shape), const(wo.shape),
        ],
        out_specs=pl.BlockSpec((tm, d), lambda i: (i, 0)),
        out_shape=jax.ShapeDtypeStruct((m, d), f32),
        compiler_params=_params(("arbitrary",)),
    )(og, att, p, p, x, wr, wa, wo)


def _dec_ret_kernel(lg_ref, q_ref, k_ref, v_ref, sg_ref, gn_ref, st_ref, og_ref, nst_ref):
    for h in range(RET_HEADS):
        lg = lg_ref[h]
        gamma = jnp.exp(lg)
        q = q_ref[0, :, h * RET_QK_DIM:(h + 1) * RET_QK_DIM]
        k = k_ref[0, :, h * RET_QK_DIM:(h + 1) * RET_QK_DIM]
        v = v_ref[0, :, h * RET_V_DIM:(h + 1) * RET_V_DIM]
        sg = sg_ref[0, :, h * RET_V_DIM:(h + 1) * RET_V_DIM]
        state = st_ref[0, 0, h]
        qk = jnp.sum(q * k, axis=1, keepdims=True)
        q8 = jnp.broadcast_to((q * gamma).astype(bf16), (8, RET_QK_DIM))
        cross = _dot(q8, state.astype(bf16))[0:1]
        k_col = jnp.transpose(jnp.broadcast_to(k, (8, RET_QK_DIM)))[:, 0:1]
        nst_ref[0, h] = gamma * state + k_col * v
        o = _rms(qk * v + cross, gn_ref[...])
        og_ref[0, :, h * RET_V_DIM:(h + 1) * RET_V_DIM] = o * sg


def _decode_retention(q, k, v, sg, gn, state, log_g):
    db = q.shape[0]
    row = lambda w: pl.BlockSpec((1, 1, w), lambda b, lg: (b, 0, 0))
    grid_spec = pltpu.PrefetchScalarGridSpec(
        num_scalar_prefetch=1,
        grid=(db,),
        in_specs=[row(RET_QK_W), row(RET_QK_W), row(RET_V_W), row(RET_V_W),
                  pl.BlockSpec((1, RET_V_DIM), lambda b, lg: (0, 0)),
                  pl.BlockSpec((1, 1, RET_HEADS, RET_QK_DIM, RET_V_DIM), lambda b, lg: (0, b, 0, 0, 0))],
        out_specs=[row(RET_V_W),
                   pl.BlockSpec((1, RET_HEADS, RET_QK_DIM, RET_V_DIM), lambda b, lg: (b, 0, 0, 0))],
    )
    return pl.pallas_call(
        _dec_ret_kernel,
        grid_spec=grid_spec,
        out_shape=[jax.ShapeDtypeStruct((db, 1, RET_V_W), f32),
                   jax.ShapeDtypeStruct((db, RET_HEADS, RET_QK_DIM, RET_V_DIM), f32)],
        compiler_params=_params(("arbitrary",)),
    )(log_g, q, k, v, sg, gn, state)


def _dec_score_kernel(pt_ref, qi_ref, w_ref, pool_ref, s_ref, buf_ref, sem_ref, *, n_pages):
    b = pl.program_id(0)
    slot = b % 2

    def page_copy(seq, p, slot_):
        dst = buf_ref.at[slot_, :, pl.ds(pl.multiple_of(p * PAGE_SIZE, PAGE_SIZE), PAGE_SIZE)]
        return pltpu.make_async_copy(pool_ref.at[pt_ref[seq, p]], dst, sem_ref.at[slot_])

    def start_all(seq, slot_):
        def body(p, c):
            page_copy(seq, p, slot_).start()
            return c
        lax.fori_loop(0, n_pages, body, 0)

    @pl.when(b == 0)
    def _():
        start_all(0, 0)

    @pl.when(b + 1 < pl.num_programs(0))
    def _():
        start_all(b + 1, 1 - slot)

    def wait_body(p, c):
        page_copy(b, p, slot).wait()
        return c
    lax.fori_loop(0, n_pages, wait_body, 0)

    s = jnp.maximum(_dot(qi_ref[0], buf_ref[slot].astype(bf16)), 0.0)
    row = jnp.sum(s * w_ref[0], axis=0, keepdims=True)
    for p in range(n_pages):
        s_ref[0, p:p + 1, :] = row[:, p * PAGE_SIZE:(p + 1) * PAGE_SIZE]


def _decode_scores(page_table, qi, w, pool_ikt):
    db, n_pages = page_table.shape
    kern = functools.partial(_dec_score_kernel, n_pages=n_pages)
    grid_spec = pltpu.PrefetchScalarGridSpec(
        num_scalar_prefetch=1,
        grid=(db,),
        in_specs=[
            pl.BlockSpec((1, IDX_HEADS, IDX_DIM), lambda b, pt: (b, 0, 0)),
            pl.BlockSpec((1, IDX_HEADS, 1), lambda b, pt: (b, 0, 0)),
            pl.BlockSpec(memory_space=pl.ANY),
        ],
        out_specs=pl.BlockSpec((1, n_pages, PAGE_SIZE), lambda b, pt: (b, 0, 0)),
        scratch_shapes=[pltpu.VMEM((2, IDX_DIM, n_pages * PAGE_SIZE), f32), pltpu.SemaphoreType.DMA((2,))],
    )
    return pl.pallas_call(
        kern,
        grid_spec=grid_spec,
        out_shape=jax.ShapeDtypeStruct((db, n_pages, PAGE_SIZE), f32),
        compiler_params=_params(("arbitrary",)),
    )(page_table, qi, w, pool_ikt)


def _dec_select_kernel(s_ref, qi_ref, w_ref, ikn_ref, idx_ref, cnt_ref, nsel_ref, rank_ref,
                       *, n_keep, kmax):
    db, n_pages, _ = s_ref.shape
    ikn = ikn_ref[...].astype(bf16).astype(f32)
    dots = jnp.sum(qi_ref[...].astype(f32) * ikn, axis=2, keepdims=True)
    s_new = jnp.sum(jnp.maximum(dots, 0.0) * w_ref[...], axis=1, keepdims=True)
    keys = _sortable_key(s_ref[...])
    key_new = _sortable_key(s_new)

    def count_ge(t):
        c = jnp.sum(jnp.where(keys >= t, 1.0, 0.0), axis=1, keepdims=True)
        return jnp.sum(c, axis=2, keepdims=True) + jnp.where(key_new >= t, 1.0, 0.0)

    thr = _kth_largest_key(count_ge, n_keep, (db, 1, 1))
    sel = jnp.where(keys >= thr, 1.0, 0.0)
    nsel_ref[...] = jnp.broadcast_to(jnp.where(key_new >= thr, 1.0, 0.0), (db, 1, LANES))

    pp = lax.broadcasted_iota(jnp.int32, (n_pages, n_pages), 0)
    pq = lax.broadcasted_iota(jnp.int32, (n_pages, n_pages), 1)
    earlier_page = jnp.where(pq < pp, 1.0, 0.0).astype(bf16)
    oo = lax.broadcasted_iota(jnp.int32, (PAGE_SIZE, PAGE_SIZE), 0)
    oq = lax.broadcasted_iota(jnp.int32, (PAGE_SIZE, PAGE_SIZE), 1)
    earlier_off = jnp.where(oo < oq, 1.0, 0.0).astype(bf16)
    r_iota = lax.broadcasted_iota(jnp.int32, (kmax, PAGE_SIZE), 0).astype(f32)
    lane = lax.broadcasted_iota(jnp.int32, (kmax, PAGE_SIZE), 1).astype(f32)
    for b in range(db):
        sb = sel[b]
        per_page = jnp.sum(sb, axis=1, keepdims=True)
        before = _dot(earlier_page, jnp.broadcast_to(per_page, (n_pages, PAGE_SIZE)).astype(bf16))
        within = _dot(sb.astype(bf16), earlier_off)
        rank_ref[b] = jnp.where(sb > 0.0, before + within, -1.0)

        def body(p, acc, b=b):
            row = rank_ref[b, pl.ds(p, 1), :]
            return jnp.where(row == r_iota, lane + jnp.asarray(p * PAGE_SIZE, f32), acc)

        acc = lax.fori_loop(0, n_pages, body, jnp.zeros((kmax, PAGE_SIZE), f32))
        idx_ref[b] = jnp.sum(acc, axis=1, keepdims=True).astype(jnp.int32)
        total = jnp.sum(per_page, axis=0, keepdims=True)
        cnt_ref[b] = jnp.broadcast_to(total, (1, LANES)).astype(jnp.int32)


def _decode_select(scores, qi, w, ik_new, n_keep):
    db, n_pages, _ = scores.shape
    kmax = -(-n_keep // 8) * 8
    kern = functools.partial(_dec_select_kernel, n_keep=n_keep, kmax=kmax)
    full = lambda shape: pl.BlockSpec(shape, lambda i: (0,) * len(shape))
    return pl.pallas_call(
        kern,
        grid=(1,),
        in_specs=[full(scores.shape), full(qi.shape), full(w.shape), full(ik_new.shape)],
        out_specs=[full((db, kmax, 1)), full((db, 1, LANES)), full((db, 1, LANES))],
        out_shape=[jax.ShapeDtypeStruct((db, kmax, 1), jnp.int32),
                   jax.ShapeDtypeStruct((db, 1, LANES), jnp.int32),
                   jax.ShapeDtypeStruct((db, 1, LANES), f32)],
        scratch_shapes=[pltpu.VMEM((db, n_pages, PAGE_SIZE), f32)],
        compiler_params=_params(("arbitrary",)),
    )(scores, qi, w, ik_new)


def _dec_attn_kernel(idx_ref, pt_ref, cnt_ref, q_ref, kn_ref, vn_ref, nsel_ref, kpool_ref, vpool_ref, o_ref,
                     kbuf_ref, vbuf_ref, sem_ref, *, kmax):
    b = pl.program_id(0)
    slot = b % 2
    page_shift = PAGE_SIZE.bit_length() - 1

    def row_copies(seq, r, slot_):
        pos = idx_ref[seq, r]
        phys = pt_ref[seq, lax.shift_right_logical(pos, page_shift)]
        src0 = pl.multiple_of((pos & (PAGE_SIZE - 1)) * ATT_KV_HEADS, ATT_KV_HEADS)
        dst0 = pl.multiple_of(r * ATT_KV_HEADS, ATT_KV_HEADS)
        return (pltpu.make_async_copy(kpool_ref.at[phys, pl.ds(src0, ATT_KV_HEADS), :],
                                      kbuf_ref.at[slot_, pl.ds(dst0, ATT_KV_HEADS), :], sem_ref.at[0, slot_]),
                pltpu.make_async_copy(vpool_ref.at[phys, pl.ds(src0, ATT_KV_HEADS), :],
                                      vbuf_ref.at[slot_, pl.ds(dst0, ATT_KV_HEADS), :], sem_ref.at[1, slot_]))

    def start_all(seq, slot_):
        def body(r, c):
            ck, cv = row_copies(seq, r, slot_)
            ck.start()
            cv.start()
            return c
        lax.fori_loop(0, kmax, body, 0, unroll=8)

    @pl.when(b == 0)
    def _():
        start_all(0, 0)

    @pl.when(b + 1 < pl.num_programs(0))
    def _():
        start_all(b + 1, 1 - slot)

    def wait_body(r, c):
        ck, cv = row_copies(b, r, slot)
        ck.wait()
        cv.wait()
        return c
    lax.fori_loop(0, kmax, wait_body, 0, unroll=8)

    q = q_ref[0]
    s = _dot_nt(q, kbuf_ref[slot].astype(bf16))
    col = lax.broadcasted_iota(jnp.int32, s.shape, 1)
    head = lax.broadcasted_iota(jnp.int32, s.shape, 0)
    ok = (col % ATT_KV_HEADS == head // GROUP) & (col // ATT_KV_HEADS < cnt_ref[b])
    s = jnp.where(ok, s, -jnp.inf)
    s_n = jnp.sum(q.astype(f32) * kn_ref[0].astype(f32), axis=1, keepdims=True)
    s_n = jnp.where(nsel_ref[0][:, 0:1] > 0.0, s_n, -jnp.inf)
    m = jnp.maximum(jnp.max(s, axis=1, keepdims=True), s_n)
    p = jnp.exp2(s - m)
    p_n = jnp.exp2(s_n - m)
    l = jnp.sum(p, axis=1, keepdims=True) + p_n
    pv = _dot(p.astype(bf16), vbuf_ref[slot].astype(bf16))
    o_ref[0] = (pv + p_n.astype(bf16).astype(f32) * vn_ref[0].astype(f32)) / l


def _decode_attention(idx, page_table, cnt, q, k_new, v_new, nsel, pool_k, pool_v):
    db, kmax = idx.shape
    kern = functools.partial(_dec_attn_kernel, kmax=kmax)
    seqb = lambda shape: pl.BlockSpec(shape, lambda b, *_: (b, 0, 0))
    grid_spec = pltpu.PrefetchScalarGridSpec(
        num_scalar_prefetch=3,
        grid=(db,),
        in_specs=[
            seqb((1, ATT_HEADS, HEAD_DIM)),
            seqb((1, ATT_HEADS, HEAD_DIM)),
            seqb((1, ATT_HEADS, HEAD_DIM)),
            seqb((1, 1, LANES)),
            pl.BlockSpec(memory_space=pl.ANY),
            pl.BlockSpec(memory_space=pl.ANY),
        ],
        out_specs=seqb((1, ATT_HEADS, HEAD_DIM)),
        scratch_shapes=[pltpu.VMEM((2, kmax * ATT_KV_HEADS, HEAD_DIM), f32),
                        pltpu.VMEM((2, kmax * ATT_KV_HEADS, HEAD_DIM), f32),
                        pltpu.SemaphoreType.DMA((2, 2))],
    )
    return pl.pallas_call(
        kern,
        grid_spec=grid_spec,
        out_shape=jax.ShapeDtypeStruct((db, ATT_HEADS, HEAD_DIM), f32),
        compiler_params=_params(("arbitrary",)),
    )(idx, page_table, cnt, q, k_new, v_new, nsel, pool_k, pool_v)


def _rope_tables(pos):
    half = RET_QK_DIM // 2
    inv = ROPE_BASE ** (-jnp.arange(half, dtype=f32) / half)
    ang = pos.astype(f32)[:, None] * inv[None, :]
    cos, sin = jnp.cos(ang), jnp.sin(ang)
    return jnp.concatenate([cos, cos], axis=1), jnp.concatenate([-sin, sin], axis=1)


def _pick(n, pref):
    for t in pref:
        if n % t == 0:
            return t
    return n


def kernel(x_prompt, x_sample, state_ret, cache_k, cache_v, cache_idx_k, page_table, ffn1_norm, ffn1_w1, ffn1_w2, mix_norm, w_in, q_norm, k_norm, ret_norm, w_ret_out, w_att_out, w_o, ffn2_norm, ffn2_w1, ffn2_w2):
    batch, seq, d = x_prompt.shape
    db, ts, _ = x_sample.shape
    depth = w_in.shape[0]
    assert depth == 1 and ts == 1 and d % (2 * TN) == 0
    n_pages = page_table.shape[1]
    past = n_pages * PAGE_SIZE
    n_phys = cache_k.shape[1]
    base = 2 * (d // TN)
    m = batch * seq
    ms = 16

    log_g = jnp.log1p(-jnp.exp2(-5.0 - jnp.arange(RET_HEADS, dtype=f32)))
    w1a, w2a = ffn1_w1[0].astype(bf16), ffn1_w2[0].astype(bf16)
    w1b, w2b = ffn2_w1[0].astype(bf16), ffn2_w2[0].astype(bf16)
    w_pad = jnp.swapaxes(w_in[0], 0, 1).astype(bf16)
    wr, wa, wo = w_ret_out[0].astype(bf16), w_att_out[0].astype(bf16), w_o[0].astype(bf16)
    g1, g2, g3 = ffn1_norm, mix_norm, ffn2_norm
    qn, kn, gn = q_norm, k_norm, ret_norm

    tm = _pick(m, (512, 256, 128))
    tf = _pick(ffn1_w2.shape[1], (512, 256, 128))
    tmp = _pick(seq, (1024, 512, 256, 128))
    chunk = _pick(seq, (256, 128))
    tq = _pick(seq, (256, 128))
    tmo = _pick(m, (256, 128))

    xp = x_prompt.reshape(m, d)
    x1 = _ffn(xp, g1, w1a, w2a, tm, tf)
    cos_p, sin_p = _rope_tables(jnp.arange(seq))
    p, k32, v32, ikw = _proj(x1, g2, w_pad, cos_p, sin_p, qn, kn, tmp)
    og, st_p = _retention(p, log_g, gn, batch, seq, base, chunk)
    att = _prompt_attention(p, ikw, batch, seq, base, tq, min(TOPK_MAX, seq // 4))
    x2 = _oproj(og, att, p, x1, wr, wa, wo, tmo)
    yp = _ffn(x2, g3, w1b, w2b, tm, tf).reshape(batch, seq, d)

    xs = jnp.zeros((ms, d), f32).at[:db].set(x_sample.reshape(db, d))
    s1 = _ffn(xs, g1, w1a, w2a, ms, tf)
    cos_s, sin_s = _rope_tables(jnp.full((ms,), past, jnp.int32))
    ps, k32s, v32s, ikws = _proj(s1, g2, w_pad, cos_s, sin_s, qn, kn, ms)
    seg = lambda t0, t1: ps[:db, (base + t0) * TN:(base + t1) * TN]
    row3 = lambda a: a.astype(f32).reshape(db, 1, a.shape[-1])
    og_s, st_s = _decode_retention(row3(seg(T_RQ, T_RK)), row3(seg(T_RK, T_RV)), row3(seg(T_RV, T_RG)),
                                   row3(seg(T_RG, T_AQ)), gn, state_ret, log_g)
    qi = seg(T_IQ, T_IKW).reshape(db, IDX_HEADS, IDX_DIM)
    wi = (ikws[:db, IDX_DIM:IDX_DIM + IDX_HEADS] * IDX_W_SCALE).reshape(db, IDX_HEADS, 1)
    ik_new = ikws[:db, :IDX_DIM].reshape(db, 1, IDX_DIM)
    pool_ikt = jnp.swapaxes(cache_idx_k.reshape(n_phys, PAGE_SIZE, IDX_DIM), 1, 2)
    scores = _decode_scores(page_table, qi, wi, pool_ikt)
    idx, cnt, nsel = _decode_select(scores, qi, wi, ik_new, min(TOPK_MAX, (past + ts) // 4))
    aq_s = seg(T_AQ, T_AK).reshape(db, ATT_HEADS, HEAD_DIM)
    expand = lambda a: jnp.repeat(a[:db].reshape(db, ATT_KV_HEADS, HEAD_DIM), GROUP, axis=1).astype(bf16)
    pool_rows = lambda c: c.reshape(n_phys, PAGE_SIZE * ATT_KV_HEADS, HEAD_DIM)
    att_s = _decode_attention(idx.reshape(db, -1), page_table, cnt[:, 0, 0], aq_s, expand(k32s), expand(v32s),
                              nsel, pool_rows(cache_k), pool_rows(cache_v))
    pad_rows = lambda a: jnp.pad(a.reshape(db, -1).astype(bf16), ((0, ms - db), (0, 0)))
    s2 = _oproj(pad_rows(og_s), pad_rows(att_s), ps, s1, wr, wa, wo, ms)
    ys = _ffn(s2, g3, w1b, w2b, ms, tf)[:db].reshape(db, ts, d)

    return (yp, ys,
            st_p[None],
            k32.reshape(1, batch, seq, ATT_KV_HEADS, HEAD_DIM),
            v32.reshape(1, batch, seq, ATT_KV_HEADS, HEAD_DIM),
            ikw[:, :IDX_DIM].reshape(1, batch, seq, IDX_DIM),
            st_s[None],
            k32s[:db].reshape(1, db, ts, ATT_KV_HEADS, HEAD_DIM),
            v32s[:db].reshape(1, db, ts, ATT_KV_HEADS, HEAD_DIM),
            ikws[:db, :IDX_DIM].reshape(1, db, ts, IDX_DIM))
```

```python
import functools

import jax
import jax.numpy as jnp
from jax import lax
from jax.experimental import pallas as pl
from jax.experimental.pallas import tpu as pltpu

RET_HEADS = 8
RET_QK_DIM = 128
RET_V_DIM = 256
ATT_HEADS = 16
ATT_KV_HEADS = 4
HEAD_DIM = 128
GROUP = ATT_HEADS // ATT_KV_HEADS
IDX_HEADS = 16
IDX_DIM = 64
IDX_W_SCALE = (IDX_HEADS ** -0.5) * (IDX_DIM ** -0.5)
TOPK_MAX = 256
PAGE_SIZE = 128
ROPE_BASE = 10000.0
EPS = 1e-6

RET_QK_W = RET_HEADS * RET_QK_DIM
RET_V_W = RET_HEADS * RET_V_DIM
ATT_Q_W = ATT_HEADS * HEAD_DIM
ATT_KV_W = ATT_KV_HEADS * HEAD_DIM
IDX_Q_W = IDX_HEADS * IDX_DIM
MAIN_W = 2 * RET_QK_W + 2 * RET_V_W + ATT_Q_W + 2 * ATT_KV_W + IDX_Q_W

LANES = 128
TN = 512
T_RQ, T_RK, T_RV, T_RG, T_AQ, T_AK, T_AV, T_IQ, T_IKW, T_END = 0, 2, 4, 8, 12, 16, 17, 18, 20, 21
VMEM_LIMIT = 56 * 1024 * 1024
INT_MIN = -2 ** 31
I16_MIN, I16_MAX = -2 ** 15, 2 ** 15 - 1
NEG_BIG = -1e30
LOG2E = 1.4426950408889634
VT_PAD = 16

bf16 = jnp.bfloat16
f32 = jnp.float32


def _params(sem, vmem=VMEM_LIMIT):
    return pltpu.CompilerParams(dimension_semantics=sem, vmem_limit_bytes=vmem)


def _sigmoid(x):
    return 1.0 / (1.0 + jnp.exp(-x))


def _dot(a, b):
    return jnp.dot(a, b, preferred_element_type=f32)


def _dot_nt(a, b):
    return lax.dot_general(a, b, (((1,), (1,)), ((), ())), preferred_element_type=f32)


def _rms(x, gain):
    ms = jnp.mean(x * x, axis=-1, keepdims=True)
    return x * lax.rsqrt(ms + EPS) * gain


def _ffn_kernel(x_ref, g_ref, w1g_ref, w1u_ref, w2_ref, o_ref, h_ref):
    j = pl.program_id(1)

    @pl.when(j == 0)
    def _():
        h_ref[...] = _rms(x_ref[...], g_ref[...]).astype(bf16)
        o_ref[...] = jnp.zeros_like(o_ref)

    h = h_ref[...]
    gate = _dot(h, w1g_ref[...].astype(bf16))
    up = _dot(h, w1u_ref[...].astype(bf16))
    act = (gate * _sigmoid(gate) * up).astype(bf16)
    o_ref[...] += _dot(act, w2_ref[...].astype(bf16))

    @pl.when(j == pl.num_programs(1) - 1)
    def _():
        o_ref[...] = x_ref[...] + 0.5 * o_ref[...]


def _ffn(x, gain, w1, w2, tm, tf):
    m, d = x.shape
    dff = w2.shape[0]
    nf = dff // tf
    return pl.pallas_call(
        _ffn_kernel,
        grid=(m // tm, nf),
        in_specs=[
            pl.BlockSpec((tm, d), lambda i, j: (i, 0)),
            pl.BlockSpec((1, d), lambda i, j: (0, 0)),
            pl.BlockSpec((d, tf), lambda i, j: (0, j)),
            pl.BlockSpec((d, tf), lambda i, j: (0, j + nf)),
            pl.BlockSpec((tf, d), lambda i, j: (j, 0)),
        ],
        out_specs=pl.BlockSpec((tm, d), lambda i, j: (i, 0)),
        out_shape=jax.ShapeDtypeStruct((m, d), f32),
        scratch_shapes=[pltpu.VMEM((tm, d), bf16)],
        compiler_params=_params(("arbitrary", "arbitrary")),
    )(x, gain, w1, w1, w2)


def _proj_kernel(x_ref, g_ref, w_ref, cos_ref, sin_ref, qn_ref, kn_ref,
                 p_ref, k32_ref, v32_ref, ikw_ref, h_ref, *, base):
    j = pl.program_id(1)

    @pl.when(j == 0)
    def _():
        h_ref[...] = _rms(x_ref[...], g_ref[...]).astype(bf16)

    w = w_ref[...].astype(bf16)
    heads = [slice(c * LANES, (c + 1) * LANES) for c in range(TN // LANES)]
    tm = h_ref.shape[0]
    rc = min(tm, 256)

    def segment(cond, epilogue):
        @pl.when(cond)
        def _():
            for r0 in range(0, tm, rc):
                rows = slice(r0, r0 + rc)
                epilogue(_dot_nt(h_ref[rows, :], w), rows)

    def gates(res, rows):
        p_ref[rows, :] = _sigmoid(res).astype(bf16)

    def rotary(res, rows):
        cos, sin = cos_ref[rows, :], sin_ref[rows, :]
        scale = jnp.where(j >= base + T_RK, RET_QK_DIM ** -0.5, 1.0).astype(f32)
        for sl in heads:
            x = res[:, sl]
            p_ref[rows, sl] = ((x * cos + pltpu.roll(x, LANES // 2, 1) * sin) * scale).astype(bf16)

    def plain(res, rows):
        p_ref[rows, :] = res.astype(bf16)

    def silu(res, rows):
        p_ref[rows, :] = (res * _sigmoid(res)).astype(bf16)

    def q_norm(res, rows):
        for sl in heads:
            p_ref[rows, sl] = (_rms(res[:, sl], qn_ref[...]) * (HEAD_DIM ** -0.5 * LOG2E)).astype(bf16)

    def k_norm(res, rows):
        for sl in heads:
            y = _rms(res[:, sl], kn_ref[...])
            k32_ref[rows, sl] = y
            p_ref[rows, sl] = y.astype(bf16)

    def value(res, rows):
        v32_ref[rows, :] = res
        p_ref[rows, :] = res.astype(bf16)

    def idx_kw(res, rows):
        ikw_ref[rows, :] = res[:, :LANES]
        p_ref[rows, :] = res.astype(bf16)

    segment(j < base, gates)
    segment((j >= base + T_RQ) & (j < base + T_RV), rotary)
    segment(((j >= base + T_RV) & (j < base + T_RG)) | ((j >= base + T_IQ) & (j < base + T_IKW)), plain)
    segment((j >= base + T_RG) & (j < base + T_AQ), silu)
    segment((j >= base + T_AQ) & (j < base + T_AK), q_norm)
    segment(j == base + T_AK, k_norm)
    segment(j == base + T_AV, value)
    segment(j == base + T_IKW, idx_kw)


def _proj(x, gain, w_t, cos2, sin2, qn, kn, tm):
    m, d = x.shape
    gate_row0 = MAIN_W + IDX_DIM + IDX_HEADS
    base = (w_t.shape[0] - gate_row0) // TN
    nt = base + T_END
    tab_blocks = cos2.shape[0] // tm
    kern = functools.partial(_proj_kernel, base=base)
    al = 16
    assert gate_row0 % al == 0
    w_row = lambda j: al * jnp.where(j < base, gate_row0 // al + j * (TN // al), (j - base) * (TN // al))
    return pl.pallas_call(
        kern,
        grid=(m // tm, nt),
        in_specs=[
            pl.BlockSpec((tm, d), lambda i, j: (i, 0)),
            pl.BlockSpec((1, d), lambda i, j: (0, 0)),
            pl.BlockSpec((pl.Element(TN), pl.Element(d)), lambda i, j: (w_row(j), 0)),
            pl.BlockSpec((tm, LANES), lambda i, j: (i % tab_blocks, 0)),
            pl.BlockSpec((tm, LANES), lambda i, j: (i % tab_blocks, 0)),
            pl.BlockSpec((1, LANES), lambda i, j: (0, 0)),
            pl.BlockSpec((1, LANES), lambda i, j: (0, 0)),
        ],
        out_specs=[
            pl.BlockSpec((tm, TN), lambda i, j: (i, j)),
            pl.BlockSpec((tm, TN), lambda i, j: (i, 0)),
            pl.BlockSpec((tm, TN), lambda i, j: (i, 0)),
            pl.BlockSpec((tm, LANES), lambda i, j: (i, 0)),
        ],
        out_shape=[
            jax.ShapeDtypeStruct((m, nt * TN), bf16),
            jax.ShapeDtypeStruct((m, ATT_KV_W), f32),
            jax.ShapeDtypeStruct((m, ATT_KV_W), f32),
            jax.ShapeDtypeStruct((m, LANES), f32),
        ],
        scratch_shapes=[pltpu.VMEM((tm, d), bf16)],
        compiler_params=_params(("arbitrary", "arbitrary")),
    )(x, gain, w_t, cos2, sin2, qn, kn)


def _ret_kernel(lg_ref, q_ref, k_ref, v_ref, sg_ref, gn_ref, og_ref, st_ref, state_ref, *, chunk):
    c = pl.program_id(1)

    @pl.when(c == 0)
    def _():
        state_ref[...] = jnp.zeros_like(state_ref)

    ii = lax.broadcasted_iota(jnp.int32, (chunk, chunk), 0)
    jj = lax.broadcasted_iota(jnp.int32, (chunk, chunk), 1)
    rel = (ii - jj).astype(f32)
    i1 = lax.broadcasted_iota(jnp.int32, (chunk, 1), 0).astype(f32)
    for h in range(RET_HEADS):
        lg = lg_ref[h]
        qk_cols = slice(h * RET_QK_DIM, (h + 1) * RET_QK_DIM)
        v_cols = slice(h * RET_V_DIM, (h + 1) * RET_V_DIM)
        q, k, v = q_ref[:, qk_cols], k_ref[:, qk_cols], v_ref[:, v_cols]
        dmat = jnp.where(rel >= 0, jnp.exp(lg * jnp.maximum(rel, 0.0)), 0.0)
        scores = _dot_nt(q, k) * dmat
        inner = _dot(scores.astype(bf16), v)
        q_dec = jnp.exp(lg * (i1 + 1.0))
        k_dec = jnp.exp(lg * (chunk - 1.0 - i1))
        state = state_ref[h]
        cross = _dot((q.astype(f32) * q_dec).astype(bf16), state.astype(bf16))
        kd = (k.astype(f32) * k_dec).astype(bf16)
        new_state = jnp.exp(lg * chunk) * state + lax.dot_general(
            kd, v, (((0,), (0,)), ((), ())), preferred_element_type=f32)
        state_ref[h] = new_state
        o = _rms(inner + cross, gn_ref[...])
        og_ref[:, v_cols] = (o * sg_ref[:, v_cols].astype(f32)).astype(bf16)

    @pl.when(c == pl.num_programs(1) - 1)
    def _():
        st_ref[0] = state_ref[...]


def _retention(p, log_g, gn, batch, seq, base, chunk):
    nc = seq // chunk
    qb = (base + T_RQ) * TN // RET_QK_W
    kb = (base + T_RK) * TN // RET_QK_W
    vb = (base + T_RV) * TN // RET_V_W
    gb = (base + T_RG) * TN // RET_V_W
    kern = functools.partial(_ret_kernel, chunk=chunk)
    grid_spec = pltpu.PrefetchScalarGridSpec(
        num_scalar_prefetch=1,
        grid=(batch, nc),
        in_specs=[
            pl.BlockSpec((chunk, RET_QK_W), lambda b, c, lg: (b * nc + c, qb)),
            pl.BlockSpec((chunk, RET_QK_W), lambda b, c, lg: (b * nc + c, kb)),
            pl.BlockSpec((chunk, RET_V_W), lambda b, c, lg: (b * nc + c, vb)),
            pl.BlockSpec((chunk, RET_V_W), lambda b, c, lg: (b * nc + c, gb)),
            pl.BlockSpec((1, RET_V_DIM), lambda b, c, lg: (0, 0)),
        ],
        out_specs=[
            pl.BlockSpec((chunk, RET_V_W), lambda b, c, lg: (b * nc + c, 0)),
            pl.BlockSpec((1, RET_HEADS, RET_QK_DIM, RET_V_DIM), lambda b, c, lg: (b, 0, 0, 0)),
        ],
        scratch_shapes=[pltpu.VMEM((RET_HEADS, RET_QK_DIM, RET_V_DIM), f32)],
    )
    return pl.pallas_call(
        kern,
        grid_spec=grid_spec,
        out_shape=[
            jax.ShapeDtypeStruct((batch * seq, RET_V_W), bf16),
            jax.ShapeDtypeStruct((batch, RET_HEADS, RET_QK_DIM, RET_V_DIM), f32),
        ],
        compiler_params=_params(("arbitrary", "arbitrary")),
    )(log_g, p, p, p, p, gn)


def _sortable_key(score):
    bits = lax.bitcast_convert_type(score, jnp.int32)
    return jnp.where(bits >= 0, bits, bits ^ jnp.int32(0x7FFFFFFF))


def _kth_largest_key(count_ge, n_keep, shape):
    def body(it, t_u):
        bit = jnp.left_shift(jnp.int32(1), 31 - it)
        cand_u = t_u | bit
        cnt = count_ge(cand_u ^ jnp.int32(INT_MIN))
        return jnp.where(cnt >= n_keep, cand_u, t_u)
    t_u = lax.fori_loop(0, 32, body, jnp.zeros(shape, jnp.int32))
    return t_u ^ jnp.int32(INT_MIN)


def _kth_largest_pos(count_ge, kth, shape, bits):
    def body(it, t):
        cand = t | jnp.left_shift(jnp.int32(1), bits - 1 - it)
        return jnp.where(count_ge(cand) >= kth, cand, t)
    return lax.fori_loop(0, bits, body, jnp.zeros(shape, jnp.int32))


def _kth_largest_16(count_ge, n_keep, shape):
    def body(it, t_u):
        cand_u = t_u | jnp.left_shift(jnp.int32(1), 15 - it)
        cnt = count_ge(cand_u + I16_MIN)
        return jnp.where(cnt >= n_keep, cand_u, t_u)
    t_u = lax.fori_loop(0, 16, body, jnp.zeros(shape, jnp.int32))
    return t_u + I16_MIN


def _attn_kernel(aq_ref, k_ref, v_ref, iq_ref, ikw_k_ref, ikw_q_ref, o_ref,
                 key_ref, hi_ref, lo_ref, iklo_ref, ikhi_ref, vt_ref, m_ref, acc_ref, p_ref, alpha_ref,
                 *, tq, ck, seq, n_keep):
    qb = pl.program_id(1)

    @pl.when(qb == 0)
    def _():
        ikw = ikw_k_ref[...]
        lane = lax.broadcasted_iota(jnp.int32, (seq, LANES), 1)
        iklo_ref[...] = jnp.where(lane < IDX_DIM, ikw, 0.0).astype(bf16)
        ikhi_ref[...] = jnp.where(lane >= IDX_DIM, pltpu.roll(ikw, IDX_DIM, 1), 0.0).astype(bf16)

        for c0 in range(0, seq, ck):
            v_t = jnp.transpose(v_ref[c0:c0 + ck, :].astype(f32)).astype(bf16)
            for g in range(ATT_KV_HEADS):
                r0 = g * (HEAD_DIM + VT_PAD)
                vt_ref[r0:r0 + HEAD_DIM, c0:c0 + ck] = v_t[g * HEAD_DIM:(g + 1) * HEAD_DIM]
                vt_ref[r0 + HEAD_DIM:r0 + HEAD_DIM + VT_PAD, c0:c0 + ck] = jnp.ones((VT_PAD, ck), bf16)

    n_chunks = (qb * tq + tq - 1) // ck + 1
    chunk = lambda kc: pl.ds(pl.multiple_of(kc * ck, ck), ck)
    stack = 4
    pairs = IDX_HEADS // 2

    iw_t = jnp.transpose(ikw_q_ref[...]) * IDX_W_SCALE
    q_stacks = [jnp.concatenate([iq_ref[:, hp * LANES:(hp + 1) * LANES] for hp in range(h0, h0 + stack)], axis=0)
                for h0 in range(0, pairs, stack)]
    k_off = lax.broadcasted_iota(jnp.int32, (ck, tq), 0)
    q_pos = qb * tq + lax.broadcasted_iota(jnp.int32, (ck, tq), 1)

    def score_body(kc, c):
        scores = jnp.zeros((ck, tq), f32)
        for si, qs in enumerate(q_stacks):
            for sub, ik_ref in enumerate((iklo_ref, ikhi_ref)):
                s = jnp.maximum(_dot_nt(ik_ref[chunk(kc), :], qs), 0.0)
                for i in range(stack):
                    row = IDX_DIM + 2 * (si * stack + i) + sub
                    scores = scores + s[:, i * tq:(i + 1) * tq] * iw_t[row:row + 1, :]
        visible = kc * ck + k_off <= q_pos
        key_ref[chunk(kc), :] = jnp.where(visible, _sortable_key(scores), jnp.int32(INT_MIN))
        return c
    lax.fori_loop(0, n_chunks, score_body, 0)

    i16 = jnp.int16
    sub = 16

    def count16_ge(half_ref):
        def count(t):
            t_b = jnp.broadcast_to(t, (sub, tq)).astype(i16)

            def body(kc, acc):
                x = half_ref[chunk(kc), :]
                for r0 in range(0, ck, sub):
                    acc = acc + jnp.where(x[r0:r0 + sub] >= t_b, i16(1), i16(0))
                return acc
            acc = lax.fori_loop(0, n_chunks, body, jnp.zeros((sub, tq), i16))
            return jnp.sum(acc.astype(f32), axis=0, keepdims=True)
        return count

    def hi_body(kc, c):
        hi_ref[chunk(kc), :] = lax.shift_right_arithmetic(key_ref[chunk(kc), :], 16).astype(i16)
        return c
    lax.fori_loop(0, n_chunks, hi_body, 0)
    count_hi = count16_ge(hi_ref)
    hi_thr = _kth_largest_16(count_hi, float(n_keep), (1, tq))
    above = jnp.where(hi_thr >= I16_MAX, 0.0, count_hi(jnp.minimum(hi_thr + 1, I16_MAX)))

    def lo_body(kc, c):
        keys = key_ref[chunk(kc), :]
        lo = ((keys & 0xFFFF) - (I16_MAX + 1)).astype(i16)
        same_hi = lax.shift_right_arithmetic(keys, 16) == hi_thr
        lo_ref[chunk(kc), :] = jnp.where(same_hi, lo, i16(I16_MIN))
        return c
    lax.fori_loop(0, n_chunks, lo_body, 0)
    lo_thr = _kth_largest_16(count16_ge(lo_ref), n_keep - above, (1, tq))
    thr = hi_thr * 65536 + (lo_thr + (I16_MAX + 1))
    thr = jnp.maximum(thr, jnp.int32(INT_MIN + 1))

    def count32(pred):
        def body(kc, acc):
            ind = jnp.where(pred(key_ref[chunk(kc), :], kc * ck + k_off), 1.0, 0.0)
            return acc + jnp.sum(ind.reshape(ck // 8, 8, tq), axis=0)
        acc = lax.fori_loop(0, n_chunks, body, jnp.zeros((8, tq), f32))
        return jnp.sum(acc, axis=0, keepdims=True)

    n_ge = count32(lambda keys, pos: keys >= thr)

    @pl.when(jnp.max(n_ge) > n_keep)
    def _():
        n_gt = count32(lambda keys, pos: keys > thr)
        kth = (n_ge - n_gt) - (n_keep - n_gt) + 1.0
        cut = _kth_largest_pos(lambda c: count32(lambda keys, pos: (keys == thr) & (pos >= c)),
                               kth, (1, tq), seq.bit_length())

        def demote(kc, c):
            keys = key_ref[chunk(kc), :]
            surplus = (keys == thr) & (kc * ck + k_off > cut)
            key_ref[chunk(kc), :] = jnp.where(surplus, keys - 1, keys)
            return c
        lax.fori_loop(0, n_chunks, demote, 0)

    m_ref[...] = jnp.full_like(m_ref, NEG_BIG)
    acc_ref[...] = jnp.zeros_like(acc_ref)
    hs = GROUP
    q_sets = [jnp.concatenate([aq_ref[:, hd * HEAD_DIM:(hd + 1) * HEAD_DIM] for hd in range(h0, h0 + hs)], axis=0)
              for h0 in range(0, ATT_HEADS, hs)]

    p_ref[...] = jnp.zeros_like(p_ref)
    alpha_ref[...] = jnp.ones_like(alpha_ref)

    def apply_pv(kc, si):
        g = si * hs // GROUP
        cols = slice(si * hs * tq, (si + 1) * hs * tq)
        r0 = g * (HEAD_DIM + VT_PAD)
        vt = vt_ref[r0:r0 + HEAD_DIM + VT_PAD, chunk(kc)]
        acc_ref[:, cols] = alpha_ref[:, cols] * acc_ref[:, cols] + _dot(vt, p_ref[:, cols])

    def attn_body(kc, c):
        bias = jnp.where(key_ref[chunk(kc), :] >= thr, 0.0, -jnp.inf)
        bias = jnp.concatenate([bias] * hs, axis=1)
        prev = jnp.maximum(kc - 1, 0)
        for si, qs in enumerate(q_sets):
            g = si * hs // GROUP
            cols = slice(si * hs * tq, (si + 1) * hs * tq)
            apply_pv(prev, si)
            s = _dot_nt(k_ref[chunk(kc), g * HEAD_DIM:(g + 1) * HEAD_DIM], qs) + bias
            m_old = m_ref[:, cols]
            m_new = jnp.maximum(m_old, jnp.max(s, axis=0, keepdims=True))
            p_ref[:, cols] = jnp.exp2(s - m_new).astype(bf16)
            alpha_ref[:, cols] = jnp.exp2(m_old - m_new)
            m_ref[:, cols] = m_new
        return c
    lax.fori_loop(0, n_chunks, attn_body, 0)
    for si in range(len(q_sets)):
        apply_pv(n_chunks - 1, si)

    for hd in range(ATT_HEADS):
        cols = slice(hd * tq, (hd + 1) * tq)
        o_t = acc_ref[:HEAD_DIM, cols] / acc_ref[HEAD_DIM:HEAD_DIM + 1, cols]
        o_ref[:, hd * HEAD_DIM:(hd + 1) * HEAD_DIM] = jnp.transpose(o_t).astype(bf16)


def _prompt_attention(p, ikw, batch, seq, base, tq, n_keep):
    nq = seq // tq
    ck = tq
    kern = functools.partial(_attn_kernel, tq=tq, ck=ck, seq=seq, n_keep=n_keep)
    aq_blk = (base + T_AQ) * TN // ATT_Q_W
    iq_blk = (base + T_IQ) * TN // IDX_Q_W
    return pl.pallas_call(
        kern,
        grid=(batch, nq),
        in_specs=[
            pl.BlockSpec((tq, ATT_Q_W), lambda b, t: (b * nq + t, aq_blk)),
            pl.BlockSpec((seq, ATT_KV_W), lambda b, t: (b, base + T_AK)),
            pl.BlockSpec((seq, ATT_KV_W), lambda b, t: (b, base + T_AV)),
            pl.BlockSpec((tq, IDX_Q_W), lambda b, t: (b * nq + t, iq_blk)),
            pl.BlockSpec((seq, LANES), lambda b, t: (b, 0)),
            pl.BlockSpec((tq, LANES), lambda b, t: (b * nq + t, 0)),
        ],
        out_specs=pl.BlockSpec((tq, ATT_Q_W), lambda b, t: (b * nq + t, 0)),
        out_shape=jax.ShapeDtypeStruct((batch * seq, ATT_Q_W), bf16),
        scratch_shapes=[pltpu.VMEM((seq, tq), jnp.int32), pltpu.VMEM((seq, tq), jnp.int16),
                        pltpu.VMEM((seq, tq), jnp.int16), pltpu.VMEM((seq, LANES), bf16),
                        pltpu.VMEM((seq, LANES), bf16),
                        pltpu.VMEM((ATT_KV_HEADS * (HEAD_DIM + VT_PAD), seq), bf16),
                        pltpu.VMEM((1, ATT_HEADS * tq), f32),
                        pltpu.VMEM((HEAD_DIM + VT_PAD, ATT_HEADS * tq), f32),
                        pltpu.VMEM((ck, ATT_HEADS * tq), bf16), pltpu.VMEM((1, ATT_HEADS * tq), f32)],
        compiler_params=_params(("arbitrary", "arbitrary")),
    )(p, p, p, p, ikw, ikw)


def _oproj_kernel(og_ref, att_ref, gr_ref, ga_ref, x_ref, wr_ref, wa_ref, wo_ref, o_ref):
    ret_out = _dot(og_ref[...], wr_ref[...])
    att_out = _dot(att_ref[...], wa_ref[...])
    merged = gr_ref[...].astype(f32) * ret_out + ga_ref[...].astype(f32) * att_out
    o_ref[...] = x_ref[...] + _dot(merged.astype(bf16), wo_ref[...])


def _oproj(og, att, p, x, wr, wa, wo, tm):
    m, d = x.shape
    const = lambda shape: pl.BlockSpec(shape, lambda i: (0, 0), pipeline_mode=pl.Buffered(1))
    return pl.pallas_call(
        _oproj_kernel,
        grid=(m // tm,),
        in_specs=[
            pl.BlockSpec((tm, RET_V_W), lambda i: (i, 0)),
            pl.BlockSpec((tm, ATT_Q_W), lambda i: (i, 0)),
            pl.BlockSpec((tm, d), lambda i: (i, 0)),
            pl.BlockSpec((tm, d), lambda i: (i, 1)),
            pl.BlockSpec((tm, d), lambda i: (i, 0)),
            const(wr.shape), const(wa.shape), const(wo.shape),
        ],
        out_specs=pl.BlockSpec((tm, d), lambda i: (i, 0)),
        out_shape=jax.ShapeDtypeStruct((m, d), f32),
        compiler_params=_params(("arbitrary",)),
    )(og, att, p, p, x, wr, wa, wo)


def _dec_ret_kernel(lg_ref, q_ref, k_ref, v_ref, sg_ref, gn_ref, st_ref, og_ref, nst_ref):
    for h in range(RET_HEADS):
        lg = lg_ref[h]
        gamma = jnp.exp(lg)
        q = q_ref[0, :, h * RET_QK_DIM:(h + 1) * RET_QK_DIM]
        k = k_ref[0, :, h * RET_QK_DIM:(h + 1) * RET_QK_DIM]
        v = v_ref[0, :, h * RET_V_DIM:(h + 1) * RET_V_DIM]
        sg = sg_ref[0, :, h * RET_V_DIM:(h + 1) * RET_V_DIM]
        state = st_ref[0, 0, h]
        qk = jnp.sum(q * k, axis=1, keepdims=True)
        q8 = jnp.broadcast_to((q * gamma).astype(bf16), (8, RET_QK_DIM))
        cross = _dot(q8, state.astype(bf16))[0:1]
        k_col = jnp.transpose(jnp.broadcast_to(k, (8, RET_QK_DIM)))[:, 0:1]
        nst_ref[0, h] = gamma * state + k_col * v
        o = _rms(qk * v + cross, gn_ref[...])
        og_ref[0, :, h * RET_V_DIM:(h + 1) * RET_V_DIM] = o * sg


def _decode_retention(q, k, v, sg, gn, state, log_g):
    db = q.shape[0]
    row = lambda w: pl.BlockSpec((1, 1, w), lambda b, lg: (b, 0, 0))
    grid_spec = pltpu.PrefetchScalarGridSpec(
        num_scalar_prefetch=1,
        grid=(db,),
        in_specs=[row(RET_QK_W), row(RET_QK_W), row(RET_V_W), row(RET_V_W),
                  pl.BlockSpec((1, RET_V_DIM), lambda b, lg: (0, 0)),
                  pl.BlockSpec((1, 1, RET_HEADS, RET_QK_DIM, RET_V_DIM), lambda b, lg: (0, b, 0, 0, 0))],
        out_specs=[row(RET_V_W),
                   pl.BlockSpec((1, RET_HEADS, RET_QK_DIM, RET_V_DIM), lambda b, lg: (b, 0, 0, 0))],
    )
    return pl.pallas_call(
        _dec_ret_kernel,
        grid_spec=grid_spec,
        out_shape=[jax.ShapeDtypeStruct((db, 1, RET_V_W), f32),
                   jax.ShapeDtypeStruct((db, RET_HEADS, RET_QK_DIM, RET_V_DIM), f32)],
        compiler_params=_params(("arbitrary",)),
    )(log_g, q, k, v, sg, gn, state)


def _dec_score_kernel(pt_ref, qi_ref, w_ref, pool_ref, s_ref, buf_ref, sem_ref, *, n_pages):
    b = pl.program_id(0)
    slot = b % 2

    def page_copy(seq, p, slot_):
        dst = buf_ref.at[slot_, :, pl.ds(pl.multiple_of(p * PAGE_SIZE, PAGE_SIZE), PAGE_SIZE)]
        return pltpu.make_async_copy(pool_ref.at[pt_ref[seq, p]], dst, sem_ref.at[slot_])

    def start_all(seq, slot_):
        def body(p, c):
            page_copy(seq, p, slot_).start()
            return c
        lax.fori_loop(0, n_pages, body, 0)

    @pl.when(b == 0)
    def _():
        start_all(0, 0)

    @pl.when(b + 1 < pl.num_programs(0))
    def _():
        start_all(b + 1, 1 - slot)

    def wait_body(p, c):
        page_copy(b, p, slot).wait()
        return c
    lax.fori_loop(0, n_pages, wait_body, 0)

    s = jnp.maximum(_dot(qi_ref[0], buf_ref[slot].astype(bf16)), 0.0)
    row = jnp.sum(s * w_ref[0], axis=0, keepdims=True)
    for p in range(n_pages):
        s_ref[0, p:p + 1, :] = row[:, p * PAGE_SIZE:(p + 1) * PAGE_SIZE]


def _decode_scores(page_table, qi, w, pool_ikt):
    db, n_pages = page_table.shape
    kern = functools.partial(_dec_score_kernel, n_pages=n_pages)
    grid_spec = pltpu.PrefetchScalarGridSpec(
        num_scalar_prefetch=1,
        grid=(db,),
        in_specs=[
            pl.BlockSpec((1, IDX_HEADS, IDX_DIM), lambda b, pt: (b, 0, 0)),
            pl.BlockSpec((1, IDX_HEADS, 1), lambda b, pt: (b, 0, 0)),
            pl.BlockSpec(memory_space=pl.ANY),
        ],
        out_specs=pl.BlockSpec((1, n_pages, PAGE_SIZE), lambda b, pt: (b, 0, 0)),
        scratch_shapes=[pltpu.VMEM((2, IDX_DIM, n_pages * PAGE_SIZE), f32), pltpu.SemaphoreType.DMA((2,))],
    )
    return pl.pallas_call(
        kern,
        grid_spec=grid_spec,
        out_shape=jax.ShapeDtypeStruct((db, n_pages, PAGE_SIZE), f32),
        compiler_params=_params(("arbitrary",)),
    )(page_table, qi, w, pool_ikt)


def _dec_select_kernel(s_ref, qi_ref, w_ref, ikn_ref, idx_ref, cnt_ref, nsel_ref, rank_ref, sel_ref,
                       *, n_keep, kmax):
    db, n_pages, _ = s_ref.shape
    ikn = ikn_ref[...].astype(bf16).astype(f32)
    dots = jnp.sum(qi_ref[...].astype(f32) * ikn, axis=2, keepdims=True)
    s_new = jnp.sum(jnp.maximum(dots, 0.0) * w_ref[...], axis=1, keepdims=True)
    keys = _sortable_key(s_ref[...])
    key_new = _sortable_key(s_new)

    def count_ge(t):
        c = jnp.sum(jnp.where(keys >= t, 1.0, 0.0), axis=1, keepdims=True)
        return jnp.sum(c, axis=2, keepdims=True) + jnp.where(key_new >= t, 1.0, 0.0)

    thr = _kth_largest_key(count_ge, n_keep, (db, 1, 1))
    sel_ref[...] = jnp.where(keys >= thr, 1.0, 0.0)
    nsel_ref[...] = jnp.broadcast_to(jnp.where(key_new >= thr, 1.0, 0.0), (db, 1, LANES))

    def count3(mask):
        c = jnp.sum(jnp.where(mask, 1.0, 0.0), axis=1, keepdims=True)
        return jnp.sum(c, axis=2, keepdims=True)

    n_ge = count_ge(thr)

    @pl.when(jnp.max(n_ge) > n_keep)
    def _():
        pos = (lax.broadcasted_iota(jnp.int32, keys.shape, 1) * PAGE_SIZE
               + lax.broadcasted_iota(jnp.int32, keys.shape, 2))
        quota = n_keep - (count3(keys > thr) + jnp.where(key_new > thr, 1.0, 0.0))
        eq = keys == thr
        ties_past = count3(eq)
        kth = ties_past - jnp.minimum(quota, ties_past) + 1.0
        cut = _kth_largest_pos(lambda c: count3(eq & (pos >= c)), kth, (db, 1, 1),
                               (n_pages * PAGE_SIZE).bit_length())
        sel_ref[...] = jnp.where((keys > thr) | (eq & (pos <= cut)), 1.0, 0.0)
        new_sel = (key_new > thr) | ((key_new == thr) & (quota > ties_past))
        nsel_ref[...] = jnp.broadcast_to(jnp.where(new_sel, 1.0, 0.0), (db, 1, LANES))

    pp = lax.broadcasted_iota(jnp.int32, (n_pages, n_pages), 0)
    pq = lax.broadcasted_iota(jnp.int32, (n_pages, n_pages), 1)
    earlier_page = jnp.where(pq < pp, 1.0, 0.0).astype(bf16)
    oo = lax.broadcasted_iota(jnp.int32, (PAGE_SIZE, PAGE_SIZE), 0)
    oq = lax.broadcasted_iota(jnp.int32, (PAGE_SIZE, PAGE_SIZE), 1)
    earlier_off = jnp.where(oo < oq, 1.0, 0.0).astype(bf16)
    r_iota = lax.broadcasted_iota(jnp.int32, (kmax, PAGE_SIZE), 0).astype(f32)
    lane = lax.broadcasted_iota(jnp.int32, (kmax, PAGE_SIZE), 1).astype(f32)
    for b in range(db):
        sb = sel_ref[b]
        per_page = jnp.sum(sb, axis=1, keepdims=True)
        before = _dot(earlier_page, jnp.broadcast_to(per_page, (n_pages, PAGE_SIZE)).astype(bf16))
        within = _dot(sb.astype(bf16), earlier_off)
        rank_ref[b] = jnp.where(sb > 0.0, before + within, -1.0)

        def body(p, acc, b=b):
            row = rank_ref[b, pl.ds(p, 1), :]
            return jnp.where(row == r_iota, lane + jnp.asarray(p * PAGE_SIZE, f32), acc)

        acc = lax.fori_loop(0, n_pages, body, jnp.zeros((kmax, PAGE_SIZE), f32))
        idx_ref[b] = jnp.sum(acc, axis=1, keepdims=True).astype(jnp.int32)
        total = jnp.sum(per_page, axis=0, keepdims=True)
        cnt_ref[b] = jnp.broadcast_to(total, (1, LANES)).astype(jnp.int32)


def _decode_select(scores, qi, w, ik_new, n_keep):
    db, n_pages, _ = scores.shape
    kmax = -(-n_keep // 8) * 8
    kern = functools.partial(_dec_select_kernel, n_keep=n_keep, kmax=kmax)
    full = lambda shape: pl.BlockSpec(shape, lambda i: (0,) * len(shape))
    return pl.pallas_call(
        kern,
        grid=(1,),
        in_specs=[full(scores.shape), full(qi.shape), full(w.shape), full(ik_new.shape)],
        out_specs=[full((db, kmax, 1)), full((db, 1, LANES)), full((db, 1, LANES))],
        out_shape=[jax.ShapeDtypeStruct((db, kmax, 1), jnp.int32),
                   jax.ShapeDtypeStruct((db, 1, LANES), jnp.int32),
                   jax.ShapeDtypeStruct((db, 1, LANES), f32)],
        scratch_shapes=[pltpu.VMEM((db, n_pages, PAGE_SIZE), f32), pltpu.VMEM((db, n_pages, PAGE_SIZE), f32)],
        compiler_params=_params(("arbitrary",)),
    )(scores, qi, w, ik_new)


def _dec_attn_kernel(idx_ref, pt_ref, cnt_ref, q_ref, kn_ref, vn_ref, nsel_ref, kpool_ref, vpool_ref, o_ref,
                     kbuf_ref, vbuf_ref, sem_ref, *, kmax):
    b = pl.program_id(0)
    slot = b % 2
    page_shift = PAGE_SIZE.bit_length() - 1

    def row_copies(seq, r, slot_):
        pos = idx_ref[seq, r]
        phys = pt_ref[seq, lax.shift_right_logical(pos, page_shift)]
        src0 = pl.multiple_of((pos & (PAGE_SIZE - 1)) * ATT_KV_HEADS, ATT_KV_HEADS)
        dst0 = pl.multiple_of(r * ATT_KV_HEADS, ATT_KV_HEADS)
        return (pltpu.make_async_copy(kpool_ref.at[phys, pl.ds(src0, ATT_KV_HEADS), :],
                                      kbuf_ref.at[slot_, pl.ds(dst0, ATT_KV_HEADS), :], sem_ref.at[0, slot_]),
                pltpu.make_async_copy(vpool_ref.at[phys, pl.ds(src0, ATT_KV_HEADS), :],
                                      vbuf_ref.at[slot_, pl.ds(dst0, ATT_KV_HEADS), :], sem_ref.at[1, slot_]))

    def start_all(seq, slot_):
        def body(r, c):
            ck, cv = row_copies(seq, r, slot_)
            ck.start()
            cv.start()
            return c
        lax.fori_loop(0, kmax, body, 0, unroll=8)

    @pl.when(b == 0)
    def _():
        start_all(0, 0)

    @pl.when(b + 1 < pl.num_programs(0))
    def _():
        start_all(b + 1, 1 - slot)

    def wait_body(r, c):
        ck, cv = row_copies(b, r, slot)
        ck.wait()
        cv.wait()
        return c
    lax.fori_loop(0, kmax, wait_body, 0, unroll=8)

    q = q_ref[0]
    s = _dot_nt(q, kbuf_ref[slot].astype(bf16))
    col = lax.broadcasted_iota(jnp.int32, s.shape, 1)
    head = lax.broadcasted_iota(jnp.int32, s.shape, 0)
    ok = (col % ATT_KV_HEADS == head // GROUP) & (col // ATT_KV_HEADS < cnt_ref[b])
    s = jnp.where(ok, s, -jnp.inf)
    s_n = jnp.sum(q.astype(f32) * kn_ref[0].astype(f32), axis=1, keepdims=True)
    s_n = jnp.where(nsel_ref[0][:, 0:1] > 0.0, s_n, -jnp.inf)
    m = jnp.maximum(jnp.max(s, axis=1, keepdims=True), s_n)
    p = jnp.exp2(s - m)
    p_n = jnp.exp2(s_n - m)
    l = jnp.sum(p, axis=1, keepdims=True) + p_n
    pv = _dot(p.astype(bf16), vbuf_ref[slot].astype(bf16))
    o_ref[0] = (pv + p_n.astype(bf16).astype(f32) * vn_ref[0].astype(f32)) / l


def _decode_attention(idx, page_table, cnt, q, k_new, v_new, nsel, pool_k, pool_v):
    db, kmax = idx.shape
    kern = functools.partial(_dec_attn_kernel, kmax=kmax)
    seqb = lambda shape: pl.BlockSpec(shape, lambda b, *_: (b, 0, 0))
    grid_spec = pltpu.PrefetchScalarGridSpec(
        num_scalar_prefetch=3,
        grid=(db,),
        in_specs=[
            seqb((1, ATT_HEADS, HEAD_DIM)),
            seqb((1, ATT_HEADS, HEAD_DIM)),
            seqb((1, ATT_HEADS, HEAD_DIM)),
            seqb((1, 1, LANES)),
            pl.BlockSpec(memory_space=pl.ANY),
            pl.BlockSpec(memory_space=pl.ANY),
        ],
        out_specs=seqb((1, ATT_HEADS, HEAD_DIM)),
        scratch_shapes=[pltpu.VMEM((2, kmax * ATT_KV_HEADS, HEAD_DIM), f32),
                        pltpu.VMEM((2, kmax * ATT_KV_HEADS, HEAD_DIM), f32),
                        pltpu.SemaphoreType.DMA((2, 2))],
    )
    return pl.pallas_call(
        kern,
        grid_spec=grid_spec,
        out_shape=jax.ShapeDtypeStruct((db, ATT_HEADS, HEAD_DIM), f32),
        compiler_params=_params(("arbitrary",)),
    )(idx, page_table, cnt, q, k_new, v_new, nsel, pool_k, pool_v)


def _rope_tables(pos):
    half = RET_QK_DIM // 2
    inv = ROPE_BASE ** (-jnp.arange(half, dtype=f32) / half)
    ang = pos.astype(f32)[:, None] * inv[None, :]
    cos, sin = jnp.cos(ang), jnp.sin(ang)
    return jnp.concatenate([cos, cos], axis=1), jnp.concatenate([-sin, sin], axis=1)


def _pick(n, pref):
    for t in pref:
        if n % t == 0:
            return t
    return n


def kernel(x_prompt, x_sample, state_ret, cache_k, cache_v, cache_idx_k, page_table, ffn1_norm, ffn1_w1, ffn1_w2, mix_norm, w_in, q_norm, k_norm, ret_norm, w_ret_out, w_att_out, w_o, ffn2_norm, ffn2_w1, ffn2_w2):
    batch, seq, d = x_prompt.shape
    db, ts, _ = x_sample.shape
    depth = w_in.shape[0]
    assert depth == 1 and ts == 1 and d % (2 * TN) == 0
    n_pages = page_table.shape[1]
    past = n_pages * PAGE_SIZE
    n_phys = cache_k.shape[1]
    base = 2 * (d // TN)
    m = batch * seq
    ms = 16

    log_g = jnp.log1p(-jnp.exp2(-5.0 - jnp.arange(RET_HEADS, dtype=f32)))
    w1a, w2a = ffn1_w1[0].astype(bf16), ffn1_w2[0].astype(bf16)
    w1b, w2b = ffn2_w1[0].astype(bf16), ffn2_w2[0].astype(bf16)
    w_pad = jnp.swapaxes(w_in[0], 0, 1).astype(bf16)
    wr, wa, wo = w_ret_out[0].astype(bf16), w_att_out[0].astype(bf16), w_o[0].astype(bf16)
    g1, g2, g3 = ffn1_norm, mix_norm, ffn2_norm
    qn, kn, gn = q_norm, k_norm, ret_norm

    tm = _pick(m, (512, 256, 128))
    tf = _pick(ffn1_w2.shape[1], (512, 256, 128))
    tmp = _pick(seq, (1024, 512, 256, 128))
    chunk = _pick(seq, (256, 128))
    tq = _pick(seq, (256, 128))
    tmo = _pick(m, (256, 128))

    xp = x_prompt.reshape(m, d)
    x1 = _ffn(xp, g1, w1a, w2a, tm, tf)
    cos_p, sin_p = _rope_tables(jnp.arange(seq))
    p, k32, v32, ikw = _proj(x1, g2, w_pad, cos_p, sin_p, qn, kn, tmp)
    og, st_p = _retention(p, log_g, gn, batch, seq, base, chunk)
    att = _prompt_attention(p, ikw, batch, seq, base, tq, min(TOPK_MAX, seq // 4))
    x2 = _oproj(og, att, p, x1, wr, wa, wo, tmo)
    yp = _ffn(x2, g3, w1b, w2b, tm, tf).reshape(batch, seq, d)

    xs = jnp.zeros((ms, d), f32).at[:db].set(x_sample.reshape(db, d))
    s1 = _ffn(xs, g1, w1a, w2a, ms, tf)
    cos_s, sin_s = _rope_tables(jnp.full((ms,), past, jnp.int32))
    ps, k32s, v32s, ikws = _proj(s1, g2, w_pad, cos_s, sin_s, qn, kn, ms)
    seg = lambda t0, t1: ps[:db, (base + t0) * TN:(base + t1) * TN]
    row3 = lambda a: a.astype(f32).reshape(db, 1, a.shape[-1])
    og_s, st_s = _decode_retention(row3(seg(T_RQ, T_RK)), row3(seg(T_RK, T_RV)), row3(seg(T_RV, T_RG)),
                                   row3(seg(T_RG, T_AQ)), gn, state_ret, log_g)
    qi = seg(T_IQ, T_IKW).reshape(db, IDX_HEADS, IDX_DIM)
    wi = (ikws[:db, IDX_DIM:IDX_DIM + IDX_HEADS] * IDX_W_SCALE).reshape(db, IDX_HEADS, 1)
    ik_new = ikws[:db, :IDX_DIM].reshape(db, 1, IDX_DIM)
    pool_ikt = jnp.swapaxes(cache_idx_k.reshape(n_phys, PAGE_SIZE, IDX_DIM), 1, 2)
    scores = _decode_scores(page_table, qi, wi, pool_ikt)
    idx, cnt, nsel = _decode_select(scores, qi, wi, ik_new, min(TOPK_MAX, (past + ts) // 4))
    aq_s = seg(T_AQ, T_AK).reshape(db, ATT_HEADS, HEAD_DIM)
    expand = lambda a: jnp.repeat(a[:db].reshape(db, ATT_KV_HEADS, HEAD_DIM), GROUP, axis=1).astype(bf16)
    pool_rows = lambda c: c.reshape(n_phys, PAGE_SIZE * ATT_KV_HEADS, HEAD_DIM)
    att_s = _decode_attention(idx.reshape(db, -1), page_table, cnt[:, 0, 0], aq_s, expand(k32s), expand(v32s),
                              nsel, pool_rows(cache_k), pool_rows(cache_v))
    pad_rows = lambda a: jnp.pad(a.reshape(db, -1).astype(bf16), ((0, ms - db), (0, 0)))
    s2 = _oproj(pad_rows(og_s), pad_rows(att_s), ps, s1, wr, wa, wo, ms)
    ys = _ffn(s2, g3, w1b, w2b, ms, tf)[:db].reshape(db, ts, d)

    return (yp, ys,
            st_p[None],
            k32.reshape(1, batch, seq, ATT_KV_HEADS, HEAD_DIM),
            v32.reshape(1, batch, seq, ATT_KV_HEADS, HEAD_DIM),
            ikw[:, :IDX_DIM].reshape(1, batch, seq, IDX_DIM),
            st_s[None],
            k32s[:db].reshape(1, db, ts, ATT_KV_HEADS, HEAD_DIM),
            v32s[:db].reshape(1, db, ts, ATT_KV_HEADS, HEAD_DIM),
            ikws[:db, :IDX_DIM].reshape(1, db, ts, IDX_DIM))
```

```python
import functools

import jax
import jax.numpy as jnp
from jax import lax
from jax.experimental import pallas as pl
from jax.experimental.pallas import tpu as pltpu

RET_HEADS = 8
RET_QK_DIM = 128
RET_V_DIM = 256
ATT_HEADS = 16
ATT_KV_HEADS = 4
HEAD_DIM = 128
GROUP = ATT_HEADS // ATT_KV_HEADS
IDX_HEADS = 16
IDX_DIM = 64
IDX_W_SCALE = (IDX_HEADS ** -0.5) * (IDX_DIM ** -0.5)
TOPK_MAX = 256
PAGE_SIZE = 128
ROPE_BASE = 10000.0
EPS = 1e-6

RET_QK_W = RET_HEADS * RET_QK_DIM
RET_V_W = RET_HEADS * RET_V_DIM
ATT_Q_W = ATT_HEADS * HEAD_DIM
ATT_KV_W = ATT_KV_HEADS * HEAD_DIM
IDX_Q_W = IDX_HEADS * IDX_DIM
MAIN_W = 2 * RET_QK_W + 2 * RET_V_W + ATT_Q_W + 2 * ATT_KV_W + IDX_Q_W

LANES = 128
TN = 512
T_RQ, T_RK, T_RV, T_RG, T_AQ, T_AK, T_AV, T_IQ, T_IKW, T_END = 0, 2, 4, 8, 12, 16, 17, 18, 20, 21
VMEM_LIMIT = 56 * 1024 * 1024
INT_MIN = -2 ** 31
I16_MIN, I16_MAX = -2 ** 15, 2 ** 15 - 1
HALF_RANGE = 2 ** 16
SUBLANES = 8
BF16_ROWS = 16
NEG_BIG = -1e30
LOG2E = 1.4426950408889634
VT_PAD = 16

bf16 = jnp.bfloat16
f32 = jnp.float32


def _params(sem, vmem=VMEM_LIMIT):
    return pltpu.CompilerParams(dimension_semantics=sem, vmem_limit_bytes=vmem)


def _sigmoid(x):
    return 1.0 / (1.0 + jnp.exp(-x))


def _dot(a, b):
    return jnp.dot(a, b, preferred_element_type=f32)


def _dot_nt(a, b):
    return lax.dot_general(a, b, (((1,), (1,)), ((), ())), preferred_element_type=f32)


def _rms(x, gain):
    ms = jnp.mean(x * x, axis=-1, keepdims=True)
    return x * lax.rsqrt(ms + EPS) * gain


def _ffn_kernel(x_ref, g_ref, w1g_ref, w1u_ref, w2_ref, o_ref, h_ref):
    j = pl.program_id(1)

    @pl.when(j == 0)
    def _():
        h_ref[...] = _rms(x_ref[...], g_ref[...]).astype(bf16)
        o_ref[...] = jnp.zeros_like(o_ref)

    h = h_ref[...]
    gate = _dot(h, w1g_ref[...])
    up = _dot(h, w1u_ref[...])
    act = (gate * _sigmoid(gate) * up).astype(bf16)
    o_ref[...] += _dot(act, w2_ref[...])

    @pl.when(j == pl.num_programs(1) - 1)
    def _():
        o_ref[...] = x_ref[...] + 0.5 * o_ref[...]


def _ffn(x, gain, w1, w2, tm, tf):
    m, d = x.shape
    dff = w2.shape[0]
    nf = dff // tf
    return pl.pallas_call(
        _ffn_kernel,
        grid=(m // tm, nf),
        in_specs=[
            pl.BlockSpec((tm, d), lambda i, j: (i, 0)),
            pl.BlockSpec((1, d), lambda i, j: (0, 0)),
            pl.BlockSpec((d, tf), lambda i, j: (0, j)),
            pl.BlockSpec((d, tf), lambda i, j: (0, j + nf)),
            pl.BlockSpec((tf, d), lambda i, j: (j, 0)),
        ],
        out_specs=pl.BlockSpec((tm, d), lambda i, j: (i, 0)),
        out_shape=jax.ShapeDtypeStruct((m, d), f32),
        scratch_shapes=[pltpu.VMEM((tm, d), bf16)],
        compiler_params=_params(("arbitrary", "arbitrary")),
    )(x, gain, w1, w1, w2)


def _proj_kernel(x_ref, g_ref, w_ref, cos_ref, sin_ref, qn_ref, kn_ref,
                 p_ref, k32_ref, v32_ref, ikw_ref, h_ref, *, base):
    j = pl.program_id(1)

    @pl.when(j == 0)
    def _():
        h_ref[...] = _rms(x_ref[...], g_ref[...]).astype(bf16)

    w = w_ref[...]
    heads = [slice(c * LANES, (c + 1) * LANES) for c in range(TN // LANES)]
    tm = h_ref.shape[0]
    rc = min(tm, 256)

    def segment(cond, epilogue):
        @pl.when(cond)
        def _():
            for r0 in range(0, tm, rc):
                rows = slice(r0, r0 + rc)
                epilogue(_dot_nt(h_ref[rows, :], w), rows)

    def gates(res, rows):
        p_ref[rows, :] = _sigmoid(res).astype(bf16)

    def rotary(res, rows):
        cos, sin = cos_ref[rows, :], sin_ref[rows, :]
        scale = jnp.where(j >= base + T_RK, RET_QK_DIM ** -0.5, 1.0).astype(f32)
        for sl in heads:
            x = res[:, sl]
            p_ref[rows, sl] = ((x * cos + pltpu.roll(x, LANES // 2, 1) * sin) * scale).astype(bf16)

    def plain(res, rows):
        p_ref[rows, :] = res.astype(bf16)

    def silu(res, rows):
        p_ref[rows, :] = (res * _sigmoid(res)).astype(bf16)

    def q_norm(res, rows):
        for sl in heads:
            p_ref[rows, sl] = (_rms(res[:, sl], qn_ref[...]) * (HEAD_DIM ** -0.5 * LOG2E)).astype(bf16)

    def k_norm(res, rows):
        for sl in heads:
            y = _rms(res[:, sl], kn_ref[...])
            k32_ref[rows, sl] = y
            p_ref[rows, sl] = y.astype(bf16)

    def value(res, rows):
        v32_ref[rows, :] = res
        p_ref[rows, :] = res.astype(bf16)

    def idx_kw(res, rows):
        ikw_ref[rows, :] = res[:, :LANES]
        p_ref[rows, :] = res.astype(bf16)

    segment(j < base, gates)
    segment((j >= base + T_RQ) & (j < base + T_RV), rotary)
    segment(((j >= base + T_RV) & (j < base + T_RG)) | ((j >= base + T_IQ) & (j < base + T_IKW)), plain)
    segment((j >= base + T_RG) & (j < base + T_AQ), silu)
    segment((j >= base + T_AQ) & (j < base + T_AK), q_norm)
    segment(j == base + T_AK, k_norm)
    segment(j == base + T_AV, value)
    segment(j == base + T_IKW, idx_kw)


def _proj(x, gain, w_t, cos2, sin2, qn, kn, tm):
    m, d = x.shape
    gate_row0 = MAIN_W + IDX_DIM + IDX_HEADS
    base = (w_t.shape[0] - gate_row0) // TN
    nt = base + T_END
    tab_blocks = cos2.shape[0] // tm
    kern = functools.partial(_proj_kernel, base=base)
    al = BF16_ROWS
    assert gate_row0 % al == 0
    w_row = lambda j: al * jnp.where(j < base, gate_row0 // al + j * (TN // al), (j - base) * (TN // al))
    return pl.pallas_call(
        kern,
        grid=(m // tm, nt),
        in_specs=[
            pl.BlockSpec((tm, d), lambda i, j: (i, 0)),
            pl.BlockSpec((1, d), lambda i, j: (0, 0)),
            pl.BlockSpec((pl.Element(TN), pl.Element(d)), lambda i, j: (w_row(j), 0)),
            pl.BlockSpec((tm, LANES), lambda i, j: (i % tab_blocks, 0)),
            pl.BlockSpec((tm, LANES), lambda i, j: (i % tab_blocks, 0)),
            pl.BlockSpec((1, LANES), lambda i, j: (0, 0)),
            pl.BlockSpec((1, LANES), lambda i, j: (0, 0)),
        ],
        out_specs=[
            pl.BlockSpec((tm, TN), lambda i, j: (i, j)),
            pl.BlockSpec((tm, TN), lambda i, j: (i, 0)),
            pl.BlockSpec((tm, TN), lambda i, j: (i, 0)),
            pl.BlockSpec((tm, LANES), lambda i, j: (i, 0)),
        ],
        out_shape=[
            jax.ShapeDtypeStruct((m, nt * TN), bf16),
            jax.ShapeDtypeStruct((m, ATT_KV_W), f32),
            jax.ShapeDtypeStruct((m, ATT_KV_W), f32),
            jax.ShapeDtypeStruct((m, LANES), f32),
        ],
        scratch_shapes=[pltpu.VMEM((tm, d), bf16)],
        compiler_params=_params(("arbitrary", "arbitrary")),
    )(x, gain, w_t, cos2, sin2, qn, kn)


def _ret_kernel(lg_ref, q_ref, k_ref, v_ref, sg_ref, gn_ref, og_ref, st_ref, state_ref, *, chunk):
    c = pl.program_id(1)

    @pl.when(c == 0)
    def _():
        state_ref[...] = jnp.zeros_like(state_ref)

    ii = lax.broadcasted_iota(jnp.int32, (chunk, chunk), 0)
    jj = lax.broadcasted_iota(jnp.int32, (chunk, chunk), 1)
    rel = (ii - jj).astype(f32)
    i1 = lax.broadcasted_iota(jnp.int32, (chunk, 1), 0).astype(f32)
    for h in range(RET_HEADS):
        lg = lg_ref[h]
        qk_cols = slice(h * RET_QK_DIM, (h + 1) * RET_QK_DIM)
        v_cols = slice(h * RET_V_DIM, (h + 1) * RET_V_DIM)
        q, k, v = q_ref[:, qk_cols], k_ref[:, qk_cols], v_ref[:, v_cols]
        dmat = jnp.where(rel >= 0, jnp.exp(lg * jnp.maximum(rel, 0.0)), 0.0)
        scores = _dot_nt(q, k) * dmat
        inner = _dot(scores.astype(bf16), v)
        q_dec = jnp.exp(lg * (i1 + 1.0))
        k_dec = jnp.exp(lg * (chunk - 1.0 - i1))
        state = state_ref[h]
        cross = _dot((q.astype(f32) * q_dec).astype(bf16), state.astype(bf16))
        kd = (k.astype(f32) * k_dec).astype(bf16)
        new_state = jnp.exp(lg * chunk) * state + lax.dot_general(
            kd, v, (((0,), (0,)), ((), ())), preferred_element_type=f32)
        state_ref[h] = new_state
        o = _rms(inner + cross, gn_ref[...])
        og_ref[:, v_cols] = (o * sg_ref[:, v_cols].astype(f32)).astype(bf16)

    @pl.when(c == pl.num_programs(1) - 1)
    def _():
        st_ref[0] = state_ref[...]


def _retention(p, log_g, gn, batch, seq, base, chunk):
    nc = seq // chunk
    qb = (base + T_RQ) * TN // RET_QK_W
    kb = (base + T_RK) * TN // RET_QK_W
    vb = (base + T_RV) * TN // RET_V_W
    gb = (base + T_RG) * TN // RET_V_W
    kern = functools.partial(_ret_kernel, chunk=chunk)
    grid_spec = pltpu.PrefetchScalarGridSpec(
        num_scalar_prefetch=1,
        grid=(batch, nc),
        in_specs=[
            pl.BlockSpec((chunk, RET_QK_W), lambda b, c, lg: (b * nc + c, qb)),
            pl.BlockSpec((chunk, RET_QK_W), lambda b, c, lg: (b * nc + c, kb)),
            pl.BlockSpec((chunk, RET_V_W), lambda b, c, lg: (b * nc + c, vb)),
            pl.BlockSpec((chunk, RET_V_W), lambda b, c, lg: (b * nc + c, gb)),
            pl.BlockSpec((1, RET_V_DIM), lambda b, c, lg: (0, 0)),
        ],
        out_specs=[
            pl.BlockSpec((chunk, RET_V_W), lambda b, c, lg: (b * nc + c, 0)),
            pl.BlockSpec((1, RET_HEADS, RET_QK_DIM, RET_V_DIM), lambda b, c, lg: (b, 0, 0, 0)),
        ],
        scratch_shapes=[pltpu.VMEM((RET_HEADS, RET_QK_DIM, RET_V_DIM), f32)],
    )
    return pl.pallas_call(
        kern,
        grid_spec=grid_spec,
        out_shape=[
            jax.ShapeDtypeStruct((batch * seq, RET_V_W), bf16),
            jax.ShapeDtypeStruct((batch, RET_HEADS, RET_QK_DIM, RET_V_DIM), f32),
        ],
        compiler_params=_params(("arbitrary", "arbitrary")),
    )(log_g, p, p, p, p, gn)


def _sortable_key(score):
    bits = lax.bitcast_convert_type(score, jnp.int32)
    return jnp.where(bits >= 0, bits, bits ^ jnp.int32(0x7FFFFFFF))


def _kth_largest_key(count_ge, n_keep, shape):
    def body(it, t_u):
        bit = jnp.left_shift(jnp.int32(1), 31 - it)
        cand_u = t_u | bit
        cnt = count_ge(cand_u ^ jnp.int32(INT_MIN))
        return jnp.where(cnt >= n_keep, cand_u, t_u)
    t_u = lax.fori_loop(0, 32, body, jnp.zeros(shape, jnp.int32))
    return t_u ^ jnp.int32(INT_MIN)


def _kth_largest_pos(count_ge, kth, shape, bits):
    def body(it, t):
        cand = t | jnp.left_shift(jnp.int32(1), bits - 1 - it)
        return jnp.where(count_ge(cand) >= kth, cand, t)
    return lax.fori_loop(0, bits, body, jnp.zeros(shape, jnp.int32))


def _kth_largest_16(count_ge, n_keep, shape):
    def body(it, t_u):
        cand_u = t_u | jnp.left_shift(jnp.int32(1), 15 - it)
        cnt = count_ge(cand_u + I16_MIN)
        return jnp.where(cnt >= n_keep, cand_u, t_u)
    t_u = lax.fori_loop(0, 16, body, jnp.zeros(shape, jnp.int32))
    return t_u + I16_MIN


def _attn_kernel(aq_ref, k_ref, v_ref, iq_ref, ikw_k_ref, ikw_q_ref, o_ref,
                 key_ref, hi_ref, lo_ref, iklo_ref, ikhi_ref, vt_ref, m_ref, acc_ref, p_ref, alpha_ref,
                 *, tq, ck, seq, n_keep):
    qb = pl.program_id(1)

    @pl.when(qb == 0)
    def _():
        ikw = ikw_k_ref[...]
        lane = lax.broadcasted_iota(jnp.int32, (seq, LANES), 1)
        iklo_ref[...] = jnp.where(lane < IDX_DIM, ikw, 0.0).astype(bf16)
        ikhi_ref[...] = jnp.where(lane >= IDX_DIM, pltpu.roll(ikw, IDX_DIM, 1), 0.0).astype(bf16)

        for c0 in range(0, seq, ck):
            v_t = jnp.transpose(v_ref[c0:c0 + ck, :].astype(f32)).astype(bf16)
            for g in range(ATT_KV_HEADS):
                r0 = g * (HEAD_DIM + VT_PAD)
                vt_ref[r0:r0 + HEAD_DIM, c0:c0 + ck] = v_t[g * HEAD_DIM:(g + 1) * HEAD_DIM]
                vt_ref[r0 + HEAD_DIM:r0 + HEAD_DIM + VT_PAD, c0:c0 + ck] = jnp.ones((VT_PAD, ck), bf16)

    n_chunks = (qb * tq + tq - 1) // ck + 1
    chunk = lambda kc: pl.ds(kc * ck if isinstance(kc, int) else pl.multiple_of(kc * ck, ck), ck)
    stack = 4
    pairs = IDX_HEADS // 2

    iw_t = jnp.transpose(ikw_q_ref[...]) * IDX_W_SCALE
    q_stacks = [jnp.concatenate([iq_ref[:, hp * LANES:(hp + 1) * LANES] for hp in range(h0, h0 + stack)], axis=0)
                for h0 in range(0, pairs, stack)]
    k_off = lax.broadcasted_iota(jnp.int32, (ck, tq), 0)
    q_pos = qb * tq + lax.broadcasted_iota(jnp.int32, (ck, tq), 1)

    def score_body(kc, c):
        scores = jnp.zeros((ck, tq), f32)
        for si, qs in enumerate(q_stacks):
            for sub, ik_ref in enumerate((iklo_ref, ikhi_ref)):
                s = jnp.maximum(_dot_nt(ik_ref[chunk(kc), :], qs), 0.0)
                for i in range(stack):
                    row = IDX_DIM + 2 * (si * stack + i) + sub
                    scores = scores + s[:, i * tq:(i + 1) * tq] * iw_t[row:row + 1, :]
        visible = kc * ck + k_off <= q_pos
        key_ref[chunk(kc), :] = jnp.where(visible, _sortable_key(scores), jnp.int32(INT_MIN))
        return c
    lax.fori_loop(0, n_chunks, score_body, 0)

    i16 = jnp.int16
    sub = BF16_ROWS

    def count16_ge(half_ref):
        def count(t):
            t_b = jnp.broadcast_to(t, (sub, tq)).astype(i16)

            def body(kc, acc):
                x = half_ref[chunk(kc), :]
                for r0 in range(0, ck, sub):
                    acc = acc + jnp.where(x[r0:r0 + sub] >= t_b, i16(1), i16(0))
                return acc
            acc = lax.fori_loop(0, n_chunks, body, jnp.zeros((sub, tq), i16))
            return jnp.sum(acc.astype(f32), axis=0, keepdims=True)
        return count

    def hi_body(kc, c):
        hi_ref[chunk(kc), :] = lax.shift_right_arithmetic(key_ref[chunk(kc), :], 16).astype(i16)
        return c
    lax.fori_loop(0, n_chunks, hi_body, 0)
    count_hi = count16_ge(hi_ref)
    hi_thr = _kth_largest_16(count_hi, float(n_keep), (1, tq))
    above = jnp.where(hi_thr >= I16_MAX, 0.0, count_hi(jnp.minimum(hi_thr + 1, I16_MAX)))

    def lo_body(kc, c):
        keys = key_ref[chunk(kc), :]
        lo = ((keys & (HALF_RANGE - 1)) + I16_MIN).astype(i16)
        same_hi = lax.shift_right_arithmetic(keys, 16) == hi_thr
        lo_ref[chunk(kc), :] = jnp.where(same_hi, lo, i16(I16_MIN))
        return c
    lax.fori_loop(0, n_chunks, lo_body, 0)
    lo_thr = _kth_largest_16(count16_ge(lo_ref), n_keep - above, (1, tq))
    thr = hi_thr * HALF_RANGE + (lo_thr - I16_MIN)
    thr = jnp.maximum(thr, jnp.int32(INT_MIN + 1))

    def count32(pred):
        def body(kc, acc):
            ind = jnp.where(pred(key_ref[chunk(kc), :], kc * ck + k_off), 1.0, 0.0)
            return acc + jnp.sum(ind.reshape(ck // SUBLANES, SUBLANES, tq), axis=0)
        acc = lax.fori_loop(0, n_chunks, body, jnp.zeros((SUBLANES, tq), f32))
        return jnp.sum(acc, axis=0, keepdims=True)

    n_ge = count32(lambda keys, pos: keys >= thr)

    @pl.when(jnp.max(n_ge) > n_keep)
    def _():
        n_gt = count32(lambda keys, pos: keys > thr)
        kth = (n_ge - n_gt) - (n_keep - n_gt) + 1.0
        cut = _kth_largest_pos(lambda c: count32(lambda keys, pos: (keys == thr) & (pos >= c)),
                               kth, (1, tq), seq.bit_length())

        def demote(kc, c):
            keys = key_ref[chunk(kc), :]
            surplus = (keys == thr) & (kc * ck + k_off > cut)
            key_ref[chunk(kc), :] = jnp.where(surplus, keys - 1, keys)
            return c
        lax.fori_loop(0, n_chunks, demote, 0)

    m_ref[...] = jnp.full_like(m_ref, NEG_BIG)
    acc_ref[...] = jnp.zeros_like(acc_ref)
    hs = GROUP
    q_sets = [jnp.concatenate([aq_ref[:, hd * HEAD_DIM:(hd + 1) * HEAD_DIM] for hd in range(h0, h0 + hs)], axis=0)
              for h0 in range(0, ATT_HEADS, hs)]

    def apply_pv(kc, si):
        g = si * hs // GROUP
        cols = slice(si * hs * tq, (si + 1) * hs * tq)
        r0 = g * (HEAD_DIM + VT_PAD)
        vt = vt_ref[r0:r0 + HEAD_DIM + VT_PAD, chunk(kc)]
        acc_ref[:, cols] = alpha_ref[:, cols] * acc_ref[:, cols] + _dot(vt, p_ref[:, cols])

    def softmax_chunk(kc, with_pv):
        bias = jnp.where(key_ref[chunk(kc), :] >= thr, 0.0, -jnp.inf)
        bias = jnp.concatenate([bias] * hs, axis=1)
        for si, qs in enumerate(q_sets):
            g = si * hs // GROUP
            cols = slice(si * hs * tq, (si + 1) * hs * tq)
            if with_pv:
                apply_pv(kc - 1, si)
            s = _dot_nt(k_ref[chunk(kc), g * HEAD_DIM:(g + 1) * HEAD_DIM], qs) + bias
            m_old = m_ref[:, cols]
            m_new = jnp.maximum(m_old, jnp.max(s, axis=0, keepdims=True))
            p_ref[:, cols] = jnp.exp2(s - m_new).astype(bf16)
            alpha_ref[:, cols] = jnp.exp2(m_old - m_new)
            m_ref[:, cols] = m_new

    softmax_chunk(0, False)

    def attn_body(kc, c):
        softmax_chunk(kc, True)
        return c
    lax.fori_loop(1, n_chunks, attn_body, 0)
    for si in range(len(q_sets)):
        apply_pv(n_chunks - 1, si)

    for hd in range(ATT_HEADS):
        cols = slice(hd * tq, (hd + 1) * tq)
        o_t = acc_ref[:HEAD_DIM, cols] / acc_ref[HEAD_DIM:HEAD_DIM + 1, cols]
        o_ref[:, hd * HEAD_DIM:(hd + 1) * HEAD_DIM] = jnp.transpose(o_t).astype(bf16)


def _prompt_attention(p, ikw, batch, seq, base, tq, n_keep):
    nq = seq // tq
    ck = tq
    kern = functools.partial(_attn_kernel, tq=tq, ck=ck, seq=seq, n_keep=n_keep)
    aq_blk = (base + T_AQ) * TN // ATT_Q_W
    iq_blk = (base + T_IQ) * TN // IDX_Q_W
    return pl.pallas_call(
        kern,
        grid=(batch, nq),
        in_specs=[
            pl.BlockSpec((tq, ATT_Q_W), lambda b, t: (b * nq + t, aq_blk)),
            pl.BlockSpec((seq, ATT_KV_W), lambda b, t: (b, base + T_AK)),
            pl.BlockSpec((seq, ATT_KV_W), lambda b, t: (b, base + T_AV)),
            pl.BlockSpec((tq, IDX_Q_W), lambda b, t: (b * nq + t, iq_blk)),
            pl.BlockSpec((seq, LANES), lambda b, t: (b, 0)),
            pl.BlockSpec((tq, LANES), lambda b, t: (b * nq + t, 0)),
        ],
        out_specs=pl.BlockSpec((tq, ATT_Q_W), lambda b, t: (b * nq + t, 0)),
        out_shape=jax.ShapeDtypeStruct((batch * seq, ATT_Q_W), bf16),
        scratch_shapes=[pltpu.VMEM((seq, tq), jnp.int32), pltpu.VMEM((seq, tq), jnp.int16),
                        pltpu.VMEM((seq, tq), jnp.int16), pltpu.VMEM((seq, LANES), bf16),
                        pltpu.VMEM((seq, LANES), bf16),
                        pltpu.VMEM((ATT_KV_HEADS * (HEAD_DIM + VT_PAD), seq), bf16),
                        pltpu.VMEM((1, ATT_HEADS * tq), f32),
                        pltpu.VMEM((HEAD_DIM + VT_PAD, ATT_HEADS * tq), f32),
                        pltpu.VMEM((ck, ATT_HEADS * tq), bf16), pltpu.VMEM((1, ATT_HEADS * tq), f32)],
        compiler_params=_params(("arbitrary", "arbitrary")),
    )(p, p, p, p, ikw, ikw)


def _oproj_kernel(og_ref, att_ref, gr_ref, ga_ref, x_ref, wr_ref, wa_ref, wo_ref, o_ref):
    ret_out = _dot(og_ref[...], wr_ref[...])
    att_out = _dot(att_ref[...], wa_ref[...])
    merged = gr_ref[...].astype(f32) * ret_out + ga_ref[...].astype(f32) * att_out
    o_ref[...] = x_ref[...] + _dot(merged.astype(bf16), wo_ref[...])


def _oproj(og, att, p, x, wr, wa, wo, tm):
    m, d = x.shape
    const = lambda shape: pl.BlockSpec(shape, lambda i: (0, 0), pipeline_mode=pl.Buffered(1))
    return pl.pallas_call(
        _oproj_kernel,
        grid=(m // tm,),
        in_specs=[
            pl.BlockSpec((tm, RET_V_W), lambda i: (i, 0)),
            pl.BlockSpec((tm, ATT_Q_W), lambda i: (i, 0)),
            pl.BlockSpec((tm, d), lambda i: (i, 0)),
            pl.BlockSpec((tm, d), lambda i: (i, 1)),
            pl.BlockSpec((tm, d), lambda i: (i, 0)),
            const(wr.shape), const(wa.shape), const(wo.shape),
        ],
        out_specs=pl.BlockSpec((tm, d), lambda i: (i, 0)),
        out_shape=jax.ShapeDtypeStruct((m, d), f32),
        compiler_params=_params(("arbitrary",)),
    )(og, att, p, p, x, wr, wa, wo)


def _dec_ret_kernel(lg_ref, q_ref, k_ref, v_ref, sg_ref, gn_ref, st_ref, og_ref, nst_ref):
    for h in range(RET_HEADS):
        lg = lg_ref[h]
        gamma = jnp.exp(lg)
        q = q_ref[0, :, h * RET_QK_DIM:(h + 1) * RET_QK_DIM]
        k = k_ref[0, :, h * RET_QK_DIM:(h + 1) * RET_QK_DIM]
        v = v_ref[0, :, h * RET_V_DIM:(h + 1) * RET_V_DIM]
        sg = sg_ref[0, :, h * RET_V_DIM:(h + 1) * RET_V_DIM]
        state = st_ref[0, 0, h]
        qk = jnp.sum(q * k, axis=1, keepdims=True)
        q8 = jnp.broadcast_to((q * gamma).astype(bf16), (8, RET_QK_DIM))
        cross = _dot(q8, state.astype(bf16))[0:1]
        k_col = jnp.transpose(jnp.broadcast_to(k, (8, RET_QK_DIM)))[:, 0:1]
        nst_ref[0, h] = gamma * state + k_col * v
        o = _rms(qk * v + cross, gn_ref[...])
        og_ref[0, :, h * RET_V_DIM:(h + 1) * RET_V_DIM] = o * sg


def _decode_retention(q, k, v, sg, gn, state, log_g):
    db = q.shape[0]
    row = lambda w: pl.BlockSpec((1, 1, w), lambda b, lg: (b, 0, 0))
    grid_spec = pltpu.PrefetchScalarGridSpec(
        num_scalar_prefetch=1,
        grid=(db,),
        in_specs=[row(RET_QK_W), row(RET_QK_W), row(RET_V_W), row(RET_V_W),
                  pl.BlockSpec((1, RET_V_DIM), lambda b, lg: (0, 0)),
                  pl.BlockSpec((1, 1, RET_HEADS, RET_QK_DIM, RET_V_DIM), lambda b, lg: (0, b, 0, 0, 0))],
        out_specs=[row(RET_V_W),
                   pl.BlockSpec((1, RET_HEADS, RET_QK_DIM, RET_V_DIM), lambda b, lg: (b, 0, 0, 0))],
    )
    return pl.pallas_call(
        _dec_ret_kernel,
        grid_spec=grid_spec,
        out_shape=[jax.ShapeDtypeStruct((db, 1, RET_V_W), f32),
                   jax.ShapeDtypeStruct((db, RET_HEADS, RET_QK_DIM, RET_V_DIM), f32)],
        compiler_params=_params(("arbitrary",)),
    )(log_g, q, k, v, sg, gn, state)


def _dec_score_kernel(pt_ref, qi_ref, w_ref, pool_ref, s_ref, buf_ref, sem_ref, *, n_pages):
    b = pl.program_id(0)
    slot = b % 2

    def page_copy(seq, p, slot_):
        dst = buf_ref.at[slot_, :, pl.ds(pl.multiple_of(p * PAGE_SIZE, PAGE_SIZE), PAGE_SIZE)]
        return pltpu.make_async_copy(pool_ref.at[pt_ref[seq, p]], dst, sem_ref.at[slot_])

    def start_all(seq, slot_):
        def body(p, c):
            page_copy(seq, p, slot_).start()
            return c
        lax.fori_loop(0, n_pages, body, 0)

    @pl.when(b == 0)
    def _():
        start_all(0, 0)

    @pl.when(b + 1 < pl.num_programs(0))
    def _():
        start_all(b + 1, 1 - slot)

    def wait_body(p, c):
        page_copy(b, p, slot).wait()
        return c
    lax.fori_loop(0, n_pages, wait_body, 0)

    s = jnp.maximum(_dot(qi_ref[0], buf_ref[slot].astype(bf16)), 0.0)
    row = jnp.sum(s * w_ref[0], axis=0, keepdims=True)
    for p in range(n_pages):
        s_ref[0, p:p + 1, :] = row[:, p * PAGE_SIZE:(p + 1) * PAGE_SIZE]


def _decode_scores(page_table, qi, w, pool_ikt):
    db, n_pages = page_table.shape
    kern = functools.partial(_dec_score_kernel, n_pages=n_pages)
    grid_spec = pltpu.PrefetchScalarGridSpec(
        num_scalar_prefetch=1,
        grid=(db,),
        in_specs=[
            pl.BlockSpec((1, IDX_HEADS, IDX_DIM), lambda b, pt: (b, 0, 0)),
            pl.BlockSpec((1, IDX_HEADS, 1), lambda b, pt: (b, 0, 0)),
            pl.BlockSpec(memory_space=pl.ANY),
        ],
        out_specs=pl.BlockSpec((1, n_pages, PAGE_SIZE), lambda b, pt: (b, 0, 0)),
        scratch_shapes=[pltpu.VMEM((2, IDX_DIM, n_pages * PAGE_SIZE), f32), pltpu.SemaphoreType.DMA((2,))],
    )
    return pl.pallas_call(
        kern,
        grid_spec=grid_spec,
        out_shape=jax.ShapeDtypeStruct((db, n_pages, PAGE_SIZE), f32),
        compiler_params=_params(("arbitrary",)),
    )(page_table, qi, w, pool_ikt)


def _dec_select_kernel(s_ref, qi_ref, w_ref, ikn_ref, idx_ref, cnt_ref, nsel_ref, rank_ref, sel_ref,
                       *, n_keep, kmax):
    db, n_pages, _ = s_ref.shape
    ikn = ikn_ref[...].astype(bf16).astype(f32)
    dots = jnp.sum(qi_ref[...].astype(f32) * ikn, axis=2, keepdims=True)
    s_new = jnp.sum(jnp.maximum(dots, 0.0) * w_ref[...], axis=1, keepdims=True)
    keys = _sortable_key(s_ref[...])
    key_new = _sortable_key(s_new)

    def count_ge(t):
        c = jnp.sum(jnp.where(keys >= t, 1.0, 0.0), axis=1, keepdims=True)
        return jnp.sum(c, axis=2, keepdims=True) + jnp.where(key_new >= t, 1.0, 0.0)

    thr = _kth_largest_key(count_ge, n_keep, (db, 1, 1))
    sel_ref[...] = jnp.where(keys >= thr, 1.0, 0.0)
    nsel_ref[...] = jnp.broadcast_to(jnp.where(key_new >= thr, 1.0, 0.0), (db, 1, LANES))

    def count3(mask):
        c = jnp.sum(jnp.where(mask, 1.0, 0.0), axis=1, keepdims=True)
        return jnp.sum(c, axis=2, keepdims=True)

    n_ge = count_ge(thr)

    @pl.when(jnp.max(n_ge) > n_keep)
    def _():
        pos = (lax.broadcasted_iota(jnp.int32, keys.shape, 1) * PAGE_SIZE
               + lax.broadcasted_iota(jnp.int32, keys.shape, 2))
        quota = n_keep - (count3(keys > thr) + jnp.where(key_new > thr, 1.0, 0.0))
        eq = keys == thr
        ties_past = count3(eq)
        kth = ties_past - jnp.minimum(quota, ties_past) + 1.0
        cut = _kth_largest_pos(lambda c: count3(eq & (pos >= c)), kth, (db, 1, 1),
                               (n_pages * PAGE_SIZE).bit_length())
        sel_ref[...] = jnp.where((keys > thr) | (eq & (pos <= cut)), 1.0, 0.0)
        new_sel = (key_new > thr) | ((key_new == thr) & (quota > ties_past))
        nsel_ref[...] = jnp.broadcast_to(jnp.where(new_sel, 1.0, 0.0), (db, 1, LANES))

    pp = lax.broadcasted_iota(jnp.int32, (n_pages, n_pages), 0)
    pq = lax.broadcasted_iota(jnp.int32, (n_pages, n_pages), 1)
    earlier_page = jnp.where(pq < pp, 1.0, 0.0).astype(bf16)
    oo = lax.broadcasted_iota(jnp.int32, (PAGE_SIZE, PAGE_SIZE), 0)
    oq = lax.broadcasted_iota(jnp.int32, (PAGE_SIZE, PAGE_SIZE), 1)
    earlier_off = jnp.where(oo < oq, 1.0, 0.0).astype(bf16)
    r_iota = lax.broadcasted_iota(jnp.int32, (kmax, PAGE_SIZE), 0).astype(f32)
    lane = lax.broadcasted_iota(jnp.int32, (kmax, PAGE_SIZE), 1).astype(f32)
    for b in range(db):
        sb = sel_ref[b]
        per_page = jnp.sum(sb, axis=1, keepdims=True)
        before = _dot(earlier_page, jnp.broadcast_to(per_page, (n_pages, PAGE_SIZE)).astype(bf16))
        within = _dot(sb.astype(bf16), earlier_off)
        rank_ref[b] = jnp.where(sb > 0.0, before + within, -1.0)

        def body(p, acc, b=b):
            row = rank_ref[b, pl.ds(p, 1), :]
            return jnp.where(row == r_iota, lane + jnp.asarray(p * PAGE_SIZE, f32), acc)

        acc = lax.fori_loop(0, n_pages, body, jnp.zeros((kmax, PAGE_SIZE), f32))
        idx_ref[b] = jnp.sum(acc, axis=1, keepdims=True).astype(jnp.int32)
        total = jnp.sum(per_page, axis=0, keepdims=True)
        cnt_ref[b] = jnp.broadcast_to(total, (1, LANES)).astype(jnp.int32)


def _decode_select(scores, qi, w, ik_new, n_keep):
    db, n_pages, _ = scores.shape
    kmax = -(-n_keep // 8) * 8
    kern = functools.partial(_dec_select_kernel, n_keep=n_keep, kmax=kmax)
    full = lambda shape: pl.BlockSpec(shape, lambda i: (0,) * len(shape))
    return pl.pallas_call(
        kern,
        grid=(1,),
        in_specs=[full(scores.shape), full(qi.shape), full(w.shape), full(ik_new.shape)],
        out_specs=[full((db, kmax, 1)), full((db, 1, LANES)), full((db, 1, LANES))],
        out_shape=[jax.ShapeDtypeStruct((db, kmax, 1), jnp.int32),
                   jax.ShapeDtypeStruct((db, 1, LANES), jnp.int32),
                   jax.ShapeDtypeStruct((db, 1, LANES), f32)],
        scratch_shapes=[pltpu.VMEM((db, n_pages, PAGE_SIZE), f32), pltpu.VMEM((db, n_pages, PAGE_SIZE), f32)],
        compiler_params=_params(("arbitrary",)),
    )(scores, qi, w, ik_new)


def _dec_attn_kernel(idx_ref, pt_ref, cnt_ref, q_ref, kn_ref, vn_ref, nsel_ref, kpool_ref, vpool_ref, o_ref,
                     kbuf_ref, vbuf_ref, sem_ref, *, kmax):
    b = pl.program_id(0)
    slot = b % 2
    page_shift = PAGE_SIZE.bit_length() - 1

    def row_copies(seq, r, slot_):
        pos = idx_ref[seq, r]
        phys = pt_ref[seq, lax.shift_right_logical(pos, page_shift)]
        src0 = pl.multiple_of((pos & (PAGE_SIZE - 1)) * ATT_KV_HEADS, ATT_KV_HEADS)
        dst0 = pl.multiple_of(r * ATT_KV_HEADS, ATT_KV_HEADS)
        return (pltpu.make_async_copy(kpool_ref.at[phys, pl.ds(src0, ATT_KV_HEADS), :],
                                      kbuf_ref.at[slot_, pl.ds(dst0, ATT_KV_HEADS), :], sem_ref.at[0, slot_]),
                pltpu.make_async_copy(vpool_ref.at[phys, pl.ds(src0, ATT_KV_HEADS), :],
                                      vbuf_ref.at[slot_, pl.ds(dst0, ATT_KV_HEADS), :], sem_ref.at[1, slot_]))

    def start_all(seq, slot_):
        def body(r, c):
            ck, cv = row_copies(seq, r, slot_)
            ck.start()
            cv.start()
            return c
        lax.fori_loop(0, kmax, body, 0, unroll=8)

    @pl.when(b == 0)
    def _():
        start_all(0, 0)

    @pl.when(b + 1 < pl.num_programs(0))
    def _():
        start_all(b + 1, 1 - slot)

    def wait_body(r, c):
        ck, cv = row_copies(b, r, slot)
        ck.wait()
        cv.wait()
        return c
    lax.fori_loop(0, kmax, wait_body, 0, unroll=8)

    q = q_ref[0]
    s = _dot_nt(q, kbuf_ref[slot].astype(bf16))
    col = lax.broadcasted_iota(jnp.int32, s.shape, 1)
    head = lax.broadcasted_iota(jnp.int32, s.shape, 0)
    ok = (col % ATT_KV_HEADS == head // GROUP) & (col // ATT_KV_HEADS < cnt_ref[b])
    s = jnp.where(ok, s, -jnp.inf)
    s_n = jnp.sum(q.astype(f32) * kn_ref[0].astype(f32), axis=1, keepdims=True)
    s_n = jnp.where(nsel_ref[0][:, 0:1] > 0.0, s_n, -jnp.inf)
    m = jnp.maximum(jnp.max(s, axis=1, keepdims=True), s_n)
    p = jnp.exp2(s - m)
    p_n = jnp.exp2(s_n - m)
    l = jnp.sum(p, axis=1, keepdims=True) + p_n
    pv = _dot(p.astype(bf16), vbuf_ref[slot].astype(bf16))
    o_ref[0] = (pv + p_n.astype(bf16).astype(f32) * vn_ref[0].astype(f32)) / l


def _decode_attention(idx, page_table, cnt, q, k_new, v_new, nsel, pool_k, pool_v):
    db, kmax = idx.shape
    kern = functools.partial(_dec_attn_kernel, kmax=kmax)
    seqb = lambda shape: pl.BlockSpec(shape, lambda b, *_: (b, 0, 0))
    grid_spec = pltpu.PrefetchScalarGridSpec(
        num_scalar_prefetch=3,
        grid=(db,),
        in_specs=[
            seqb((1, ATT_HEADS, HEAD_DIM)),
            seqb((1, ATT_HEADS, HEAD_DIM)),
            seqb((1, ATT_HEADS, HEAD_DIM)),
            seqb((1, 1, LANES)),
            pl.BlockSpec(memory_space=pl.ANY),
            pl.BlockSpec(memory_space=pl.ANY),
        ],
        out_specs=seqb((1, ATT_HEADS, HEAD_DIM)),
        scratch_shapes=[pltpu.VMEM((2, kmax * ATT_KV_HEADS, HEAD_DIM), f32),
                        pltpu.VMEM((2, kmax * ATT_KV_HEADS, HEAD_DIM), f32),
                        pltpu.SemaphoreType.DMA((2, 2))],
    )
    return pl.pallas_call(
        kern,
        grid_spec=grid_spec,
        out_shape=jax.ShapeDtypeStruct((db, ATT_HEADS, HEAD_DIM), f32),
        compiler_params=_params(("arbitrary",)),
    )(idx, page_table, cnt, q, k_new, v_new, nsel, pool_k, pool_v)


def _rope_tables(pos):
    half = RET_QK_DIM // 2
    inv = ROPE_BASE ** (-jnp.arange(half, dtype=f32) / half)
    ang = pos.astype(f32)[:, None] * inv[None, :]
    cos, sin = jnp.cos(ang), jnp.sin(ang)
    return jnp.concatenate([cos, cos], axis=1), jnp.concatenate([-sin, sin], axis=1)


def _pick(n, pref):
    for t in pref:
        if n % t == 0:
            return t
    return n


def kernel(x_prompt, x_sample, state_ret, cache_k, cache_v, cache_idx_k, page_table, ffn1_norm, ffn1_w1, ffn1_w2, mix_norm, w_in, q_norm, k_norm, ret_norm, w_ret_out, w_att_out, w_o, ffn2_norm, ffn2_w1, ffn2_w2):
    batch, seq, d = x_prompt.shape
    db, ts, _ = x_sample.shape
    depth = w_in.shape[0]
    assert depth == 1 and ts == 1 and d % (2 * TN) == 0
    n_pages = page_table.shape[1]
    past = n_pages * PAGE_SIZE
    n_phys = cache_k.shape[1]
    base = 2 * (d // TN)
    m = batch * seq
    ms = BF16_ROWS

    log_g = jnp.log1p(-jnp.exp2(-5.0 - jnp.arange(RET_HEADS, dtype=f32)))
    w1a, w2a = ffn1_w1[0].astype(bf16), ffn1_w2[0].astype(bf16)
    w1b, w2b = ffn2_w1[0].astype(bf16), ffn2_w2[0].astype(bf16)
    w_pad = jnp.swapaxes(w_in[0], 0, 1).astype(bf16)
    wr, wa, wo = w_ret_out[0].astype(bf16), w_att_out[0].astype(bf16), w_o[0].astype(bf16)
    g1, g2, g3 = ffn1_norm, mix_norm, ffn2_norm
    qn, kn, gn = q_norm, k_norm, ret_norm

    tm = _pick(m, (512, 256, 128))
    tf = _pick(ffn1_w2.shape[1], (512, 256, 128))
    tmp = _pick(seq, (1024, 512, 256, 128))
    chunk = _pick(seq, (256, 128))
    tq = _pick(seq, (256, 128))
    tmo = _pick(m, (256, 128))

    xp = x_prompt.reshape(m, d)
    x1 = _ffn(xp, g1, w1a, w2a, tm, tf)
    cos_p, sin_p = _rope_tables(jnp.arange(seq))
    p, k32, v32, ikw = _proj(x1, g2, w_pad, cos_p, sin_p, qn, kn, tmp)
    og, st_p = _retention(p, log_g, gn, batch, seq, base, chunk)
    att = _prompt_attention(p, ikw, batch, seq, base, tq, min(TOPK_MAX, seq // 4))
    x2 = _oproj(og, att, p, x1, wr, wa, wo, tmo)
    yp = _ffn(x2, g3, w1b, w2b, tm, tf).reshape(batch, seq, d)

    xs = jnp.zeros((ms, d), f32).at[:db].set(x_sample.reshape(db, d))
    s1 = _ffn(xs, g1, w1a, w2a, ms, tf)
    cos_s, sin_s = _rope_tables(jnp.full((ms,), past, jnp.int32))
    ps, k32s, v32s, ikws = _proj(s1, g2, w_pad, cos_s, sin_s, qn, kn, ms)
    seg = lambda t0, t1: ps[:db, (base + t0) * TN:(base + t1) * TN]
    row3 = lambda a: a.astype(f32).reshape(db, 1, a.shape[-1])
    og_s, st_s = _decode_retention(row3(seg(T_RQ, T_RK)), row3(seg(T_RK, T_RV)), row3(seg(T_RV, T_RG)),
                                   row3(seg(T_RG, T_AQ)), gn, state_ret, log_g)
    qi = seg(T_IQ, T_IKW).reshape(db, IDX_HEADS, IDX_DIM)
    wi = (ikws[:db, IDX_DIM:IDX_DIM + IDX_HEADS] * IDX_W_SCALE).reshape(db, IDX_HEADS, 1)
    ik_new = ikws[:db, :IDX_DIM].reshape(db, 1, IDX_DIM)
    pool_ikt = jnp.swapaxes(cache_idx_k.reshape(n_phys, PAGE_SIZE, IDX_DIM), 1, 2)
    scores = _decode_scores(page_table, qi, wi, pool_ikt)
    idx, cnt, nsel = _decode_select(scores, qi, wi, ik_new, min(TOPK_MAX, (past + ts) // 4))
    aq_s = seg(T_AQ, T_AK).reshape(db, ATT_HEADS, HEAD_DIM)
    expand = lambda a: jnp.repeat(a[:db].reshape(db, ATT_KV_HEADS, HEAD_DIM), GROUP, axis=1).astype(bf16)
    pool_rows = lambda c: c.reshape(n_phys, PAGE_SIZE * ATT_KV_HEADS, HEAD_DIM)
    att_s = _decode_attention(idx.reshape(db, -1), page_table, cnt[:, 0, 0], aq_s, expand(k32s), expand(v32s),
                              nsel, pool_rows(cache_k), pool_rows(cache_v))
    pad_rows = lambda a: jnp.pad(a.reshape(db, -1).astype(bf16), ((0, ms - db), (0, 0)))
    s2 = _oproj(pad_rows(og_s), pad_rows(att_s), ps, s1, wr, wa, wo, ms)
    ys = _ffn(s2, g3, w1b, w2b, ms, tf)[:db].reshape(db, ts, d)

    return (yp, ys,
            st_p[None],
            k32.reshape(1, batch, seq, ATT_KV_HEADS, HEAD_DIM),
            v32.reshape(1, batch, seq, ATT_KV_HEADS, HEAD_DIM),
            ikw[:, :IDX_DIM].reshape(1, batch, seq, IDX_DIM),
            st_s[None],
            k32s[:db].reshape(1, db, ts, ATT_KV_HEADS, HEAD_DIM),
            v32s[:db].reshape(1, db, ts, ATT_KV_HEADS, HEAD_DIM),
            ikws[:db, :IDX_DIM].reshape(1, db, ts, IDX_DIM))
```

```python
import functools

import jax
import jax.numpy as jnp
from jax import lax
from jax.experimental import pallas as pl
from jax.experimental.pallas import tpu as pltpu

RET_HEADS = 8
RET_QK_DIM = 128
RET_V_DIM = 256
ATT_HEADS = 16
ATT_KV_HEADS = 4
HEAD_DIM = 128
GROUP = ATT_HEADS // ATT_KV_HEADS
IDX_HEADS = 16
IDX_DIM = 64
IDX_W_SCALE = (IDX_HEADS ** -0.5) * (IDX_DIM ** -0.5)
TOPK_MAX = 256
PAGE_SIZE = 128
ROPE_BASE = 10000.0
EPS = 1e-6

RET_QK_W = RET_HEADS * RET_QK_DIM
RET_V_W = RET_HEADS * RET_V_DIM
ATT_Q_W = ATT_HEADS * HEAD_DIM
ATT_KV_W = ATT_KV_HEADS * HEAD_DIM
IDX_Q_W = IDX_HEADS * IDX_DIM
MAIN_W = 2 * RET_QK_W + 2 * RET_V_W + ATT_Q_W + 2 * ATT_KV_W + IDX_Q_W

LANES = 128
TN = 512
T_RQ, T_RK, T_RV, T_RG, T_AQ, T_AK, T_AV, T_IQ, T_IKW, T_END = 0, 2, 4, 8, 12, 16, 17, 18, 20, 21
VMEM_LIMIT = 56 * 1024 * 1024
INT_MIN = -2 ** 31
I16_MIN, I16_MAX = -2 ** 15, 2 ** 15 - 1
HALF_RANGE = 2 ** 16
SUBLANES = 8
BF16_ROWS = 16
NEG_BIG = -1e30
LOG2E = 1.4426950408889634
VT_PAD = 16

bf16 = jnp.bfloat16
f32 = jnp.float32


def _params(sem, vmem=VMEM_LIMIT):
    return pltpu.CompilerParams(dimension_semantics=sem, vmem_limit_bytes=vmem)


def _sigmoid(x):
    return 1.0 / (1.0 + jnp.exp(-x))


def _dot(a, b):
    return jnp.dot(a, b, preferred_element_type=f32)


def _dot_nt(a, b):
    return lax.dot_general(a, b, (((1,), (1,)), ((), ())), preferred_element_type=f32)


def _rms(x, gain):
    ms = jnp.mean(x * x, axis=-1, keepdims=True)
    return x * lax.rsqrt(ms + EPS) * gain


def _ffn_kernel(x_ref, g_ref, w1g_ref, w1u_ref, w2_ref, o_ref, h_ref):
    j = pl.program_id(1)

    @pl.when(j == 0)
    def _():
        h_ref[...] = _rms(x_ref[...], g_ref[...]).astype(bf16)
        o_ref[...] = jnp.zeros_like(o_ref)

    h = h_ref[...]
    gate = _dot(h, w1g_ref[...])
    up = _dot(h, w1u_ref[...])
    act = (gate * _sigmoid(gate) * up).astype(bf16)
    o_ref[...] += _dot(act, w2_ref[...])

    @pl.when(j == pl.num_programs(1) - 1)
    def _():
        o_ref[...] = x_ref[...] + 0.5 * o_ref[...]


def _ffn(x, gain, w1, w2, tm, tf):
    m, d = x.shape
    dff = w2.shape[0]
    nf = dff // tf
    return pl.pallas_call(
        _ffn_kernel,
        grid=(m // tm, nf),
        in_specs=[
            pl.BlockSpec((tm, d), lambda i, j: (i, 0)),
            pl.BlockSpec((1, d), lambda i, j: (0, 0)),
            pl.BlockSpec((d, tf), lambda i, j: (0, j)),
            pl.BlockSpec((d, tf), lambda i, j: (0, j + nf)),
            pl.BlockSpec((tf, d), lambda i, j: (j, 0)),
        ],
        out_specs=pl.BlockSpec((tm, d), lambda i, j: (i, 0)),
        out_shape=jax.ShapeDtypeStruct((m, d), f32),
        scratch_shapes=[pltpu.VMEM((tm, d), bf16)],
        compiler_params=_params(("arbitrary", "arbitrary")),
    )(x, gain, w1, w1, w2)


def _proj_kernel(x_ref, g_ref, w_ref, cos_ref, sin_ref, qn_ref, kn_ref,
                 p_ref, k32_ref, v32_ref, ikw_ref, h_ref, *, base):
    j = pl.program_id(1)

    @pl.when(j == 0)
    def _():
        h_ref[...] = _rms(x_ref[...], g_ref[...]).astype(bf16)

    w = w_ref[...]
    heads = [slice(c * LANES, (c + 1) * LANES) for c in range(TN // LANES)]
    tm = h_ref.shape[0]
    rc = min(tm, 256)

    def segment(cond, epilogue):
        @pl.when(cond)
        def _():
            for r0 in range(0, tm, rc):
                rows = slice(r0, r0 + rc)
                epilogue(_dot_nt(h_ref[rows, :], w), rows)

    def kv_rows(rows, g):
        return pl.ds(rows.start * ATT_KV_HEADS + g, rows.stop - rows.start, stride=ATT_KV_HEADS)

    def gates(res, rows):
        p_ref[rows, :] = _sigmoid(res).astype(bf16)

    def rotary(res, rows):
        cos, sin = cos_ref[rows, :], sin_ref[rows, :]
        scale = jnp.where(j >= base + T_RK, RET_QK_DIM ** -0.5, 1.0).astype(f32)
        for sl in heads:
            x = res[:, sl]
            p_ref[rows, sl] = ((x * cos + pltpu.roll(x, LANES // 2, 1) * sin) * scale).astype(bf16)

    def plain(res, rows):
        p_ref[rows, :] = res.astype(bf16)

    def silu(res, rows):
        p_ref[rows, :] = (res * _sigmoid(res)).astype(bf16)

    def q_norm(res, rows):
        for sl in heads:
            p_ref[rows, sl] = (_rms(res[:, sl], qn_ref[...]) * (HEAD_DIM ** -0.5 * LOG2E)).astype(bf16)

    def k_norm(res, rows):
        for g, sl in enumerate(heads):
            y = _rms(res[:, sl], kn_ref[...])
            k32_ref[kv_rows(rows, g), :] = y
            p_ref[rows, sl] = y.astype(bf16)

    def value(res, rows):
        for g, sl in enumerate(heads):
            v32_ref[kv_rows(rows, g), :] = res[:, sl]
        p_ref[rows, :] = res.astype(bf16)

    def idx_kw(res, rows):
        ikw_ref[rows, :] = res[:, :LANES]
        p_ref[rows, :] = res.astype(bf16)

    segment(j < base, gates)
    segment((j >= base + T_RQ) & (j < base + T_RV), rotary)
    segment(((j >= base + T_RV) & (j < base + T_RG)) | ((j >= base + T_IQ) & (j < base + T_IKW)), plain)
    segment((j >= base + T_RG) & (j < base + T_AQ), silu)
    segment((j >= base + T_AQ) & (j < base + T_AK), q_norm)
    segment(j == base + T_AK, k_norm)
    segment(j == base + T_AV, value)
    segment(j == base + T_IKW, idx_kw)


def _proj(x, gain, w_t, cos2, sin2, qn, kn, tm):
    m, d = x.shape
    gate_row0 = MAIN_W + IDX_DIM + IDX_HEADS
    base = (w_t.shape[0] - gate_row0) // TN
    nt = base + T_END
    tab_blocks = cos2.shape[0] // tm
    kern = functools.partial(_proj_kernel, base=base)
    al = BF16_ROWS
    assert gate_row0 % al == 0
    w_row = lambda j: al * jnp.where(j < base, gate_row0 // al + j * (TN // al), (j - base) * (TN // al))
    return pl.pallas_call(
        kern,
        grid=(m // tm, nt),
        in_specs=[
            pl.BlockSpec((tm, d), lambda i, j: (i, 0)),
            pl.BlockSpec((1, d), lambda i, j: (0, 0)),
            pl.BlockSpec((pl.Element(TN), pl.Element(d)), lambda i, j: (w_row(j), 0)),
            pl.BlockSpec((tm, LANES), lambda i, j: (i % tab_blocks, 0)),
            pl.BlockSpec((tm, LANES), lambda i, j: (i % tab_blocks, 0)),
            pl.BlockSpec((1, LANES), lambda i, j: (0, 0)),
            pl.BlockSpec((1, LANES), lambda i, j: (0, 0)),
        ],
        out_specs=[
            pl.BlockSpec((tm, TN), lambda i, j: (i, j)),
            pl.BlockSpec((tm * ATT_KV_HEADS, HEAD_DIM), lambda i, j: (i, 0)),
            pl.BlockSpec((tm * ATT_KV_HEADS, HEAD_DIM), lambda i, j: (i, 0)),
            pl.BlockSpec((tm, LANES), lambda i, j: (i, 0)),
        ],
        out_shape=[
            jax.ShapeDtypeStruct((m, nt * TN), bf16),
            jax.ShapeDtypeStruct((m * ATT_KV_HEADS, HEAD_DIM), f32),
            jax.ShapeDtypeStruct((m * ATT_KV_HEADS, HEAD_DIM), f32),
            jax.ShapeDtypeStruct((m, LANES), f32),
        ],
        scratch_shapes=[pltpu.VMEM((tm, d), bf16)],
        compiler_params=_params(("arbitrary", "arbitrary")),
    )(x, gain, w_t, cos2, sin2, qn, kn)


def _ret_kernel(lg_ref, q_ref, k_ref, v_ref, sg_ref, gn_ref, og_ref, st_ref, state_ref, *, chunk):
    c = pl.program_id(1)

    @pl.when(c == 0)
    def _():
        state_ref[...] = jnp.zeros_like(state_ref)

    ii = lax.broadcasted_iota(jnp.int32, (chunk, chunk), 0)
    jj = lax.broadcasted_iota(jnp.int32, (chunk, chunk), 1)
    rel = (ii - jj).astype(f32)
    i1 = lax.broadcasted_iota(jnp.int32, (chunk, 1), 0).astype(f32)
    for h in range(RET_HEADS):
        lg = lg_ref[h]
        qk_cols = slice(h * RET_QK_DIM, (h + 1) * RET_QK_DIM)
        v_cols = slice(h * RET_V_DIM, (h + 1) * RET_V_DIM)
        q, k, v = q_ref[:, qk_cols], k_ref[:, qk_cols], v_ref[:, v_cols]
        dmat = jnp.where(rel >= 0, jnp.exp(lg * jnp.maximum(rel, 0.0)), 0.0)
        scores = _dot_nt(q, k) * dmat
        inner = _dot(scores.astype(bf16), v)
        q_dec = jnp.exp(lg * (i1 + 1.0))
        k_dec = jnp.exp(lg * (chunk - 1.0 - i1))
        state = state_ref[h]
        cross = _dot((q.astype(f32) * q_dec).astype(bf16), state.astype(bf16))
        kd = (k.astype(f32) * k_dec).astype(bf16)
        new_state = jnp.exp(lg * chunk) * state + lax.dot_general(
            kd, v, (((0,), (0,)), ((), ())), preferred_element_type=f32)
        state_ref[h] = new_state
        o = _rms(inner + cross, gn_ref[...])
        og_ref[:, v_cols] = (o * sg_ref[:, v_cols].astype(f32)).astype(bf16)

    @pl.when(c == pl.num_programs(1) - 1)
    def _():
        st_ref[0] = state_ref[...]


def _retention(p, log_g, gn, batch, seq, base, chunk):
    nc = seq // chunk
    qb = (base + T_RQ) * TN // RET_QK_W
    kb = (base + T_RK) * TN // RET_QK_W
    vb = (base + T_RV) * TN // RET_V_W
    gb = (base + T_RG) * TN // RET_V_W
    kern = functools.partial(_ret_kernel, chunk=chunk)
    grid_spec = pltpu.PrefetchScalarGridSpec(
        num_scalar_prefetch=1,
        grid=(batch, nc),
        in_specs=[
            pl.BlockSpec((chunk, RET_QK_W), lambda b, c, lg: (b * nc + c, qb)),
            pl.BlockSpec((chunk, RET_QK_W), lambda b, c, lg: (b * nc + c, kb)),
            pl.BlockSpec((chunk, RET_V_W), lambda b, c, lg: (b * nc + c, vb)),
            pl.BlockSpec((chunk, RET_V_W), lambda b, c, lg: (b * nc + c, gb)),
            pl.BlockSpec((1, RET_V_DIM), lambda b, c, lg: (0, 0)),
        ],
        out_specs=[
            pl.BlockSpec((chunk, RET_V_W), lambda b, c, lg: (b * nc + c, 0)),
            pl.BlockSpec((1, RET_HEADS, RET_QK_DIM, RET_V_DIM), lambda b, c, lg: (b, 0, 0, 0)),
        ],
        scratch_shapes=[pltpu.VMEM((RET_HEADS, RET_QK_DIM, RET_V_DIM), f32)],
    )
    return pl.pallas_call(
        kern,
        grid_spec=grid_spec,
        out_shape=[
            jax.ShapeDtypeStruct((batch * seq, RET_V_W), bf16),
            jax.ShapeDtypeStruct((batch, RET_HEADS, RET_QK_DIM, RET_V_DIM), f32),
        ],
        compiler_params=_params(("arbitrary", "arbitrary")),
    )(log_g, p, p, p, p, gn)


def _sortable_key(score):
    bits = lax.bitcast_convert_type(score, jnp.int32)
    return jnp.where(bits >= 0, bits, bits ^ jnp.int32(0x7FFFFFFF))


def _kth_largest_key(count_ge, n_keep, shape):
    def body(it, t_u):
        bit = jnp.left_shift(jnp.int32(1), 31 - it)
        cand_u = t_u | bit
        cnt = count_ge(cand_u ^ jnp.int32(INT_MIN))
        return jnp.where(cnt >= n_keep, cand_u, t_u)
    t_u = lax.fori_loop(0, 32, body, jnp.zeros(shape, jnp.int32))
    return t_u ^ jnp.int32(INT_MIN)


def _kth_largest_pos(count_ge, kth, shape, bits):
    def body(it, t):
        cand = t | jnp.left_shift(jnp.int32(1), bits - 1 - it)
        return jnp.where(count_ge(cand) >= kth, cand, t)
    return lax.fori_loop(0, bits, body, jnp.zeros(shape, jnp.int32))


def _kth_largest_16(count_ge, n_keep, shape):
    def body(it, t_u):
        cand_u = t_u | jnp.left_shift(jnp.int32(1), 15 - it)
        cnt = count_ge(cand_u + I16_MIN)
        return jnp.where(cnt >= n_keep, cand_u, t_u)
    t_u = lax.fori_loop(0, 16, body, jnp.zeros(shape, jnp.int32))
    return t_u + I16_MIN


def _attn_kernel(aq_ref, k_ref, v_ref, iq_ref, ikw_k_ref, ikw_q_ref, o_ref,
                 key_ref, hi_ref, lo_ref, iklo_ref, ikhi_ref, vt_ref, m_ref, acc_ref, p_ref, alpha_ref,
                 *, tq, ck, seq, n_keep):
    qb = pl.program_id(1)

    @pl.when(qb == 0)
    def _():
        ikw = ikw_k_ref[...]
        lane = lax.broadcasted_iota(jnp.int32, (seq, LANES), 1)
        iklo_ref[...] = jnp.where(lane < IDX_DIM, ikw, 0.0).astype(bf16)
        ikhi_ref[...] = jnp.where(lane >= IDX_DIM, pltpu.roll(ikw, IDX_DIM, 1), 0.0).astype(bf16)

        for c0 in range(0, seq, ck):
            v_t = jnp.transpose(v_ref[c0:c0 + ck, :].astype(f32)).astype(bf16)
            for g in range(ATT_KV_HEADS):
                r0 = g * (HEAD_DIM + VT_PAD)
                vt_ref[r0:r0 + HEAD_DIM, c0:c0 + ck] = v_t[g * HEAD_DIM:(g + 1) * HEAD_DIM]
                vt_ref[r0 + HEAD_DIM:r0 + HEAD_DIM + VT_PAD, c0:c0 + ck] = jnp.ones((VT_PAD, ck), bf16)

    n_chunks = (qb * tq + tq - 1) // ck + 1
    chunk = lambda kc: pl.ds(kc * ck if isinstance(kc, int) else pl.multiple_of(kc * ck, ck), ck)
    stack = 4
    pairs = IDX_HEADS // 2

    iw_t = jnp.transpose(ikw_q_ref[...]) * IDX_W_SCALE
    q_stacks = [jnp.concatenate([iq_ref[:, hp * LANES:(hp + 1) * LANES] for hp in range(h0, h0 + stack)], axis=0)
                for h0 in range(0, pairs, stack)]
    k_off = lax.broadcasted_iota(jnp.int32, (ck, tq), 0)
    q_pos = qb * tq + lax.broadcasted_iota(jnp.int32, (ck, tq), 1)

    def score_body(kc, c):
        scores = jnp.zeros((ck, tq), f32)
        for si, qs in enumerate(q_stacks):
            for sub, ik_ref in enumerate((iklo_ref, ikhi_ref)):
                s = jnp.maximum(_dot_nt(ik_ref[chunk(kc), :], qs), 0.0)
                for i in range(stack):
                    row = IDX_DIM + 2 * (si * stack + i) + sub
                    scores = scores + s[:, i * tq:(i + 1) * tq] * iw_t[row:row + 1, :]
        visible = kc * ck + k_off <= q_pos
        key_ref[chunk(kc), :] = jnp.where(visible, _sortable_key(scores), jnp.int32(INT_MIN))
        return c
    lax.fori_loop(0, n_chunks, score_body, 0)

    i16 = jnp.int16
    sub = BF16_ROWS

    def count16_ge(half_ref):
        def count(t):
            t_b = jnp.broadcast_to(t, (sub, tq)).astype(i16)

            def body(kc, acc):
                x = half_ref[chunk(kc), :]
                for r0 in range(0, ck, sub):
                    acc = acc + jnp.where(x[r0:r0 + sub] >= t_b, i16(1), i16(0))
                return acc
            acc = lax.fori_loop(0, n_chunks, body, jnp.zeros((sub, tq), i16))
            return jnp.sum(acc.astype(f32), axis=0, keepdims=True)
        return count

    def hi_body(kc, c):
        hi_ref[chunk(kc), :] = lax.shift_right_arithmetic(key_ref[chunk(kc), :], 16).astype(i16)
        return c
    lax.fori_loop(0, n_chunks, hi_body, 0)
    count_hi = count16_ge(hi_ref)
    hi_thr = _kth_largest_16(count_hi, float(n_keep), (1, tq))
    above = jnp.where(hi_thr >= I16_MAX, 0.0, count_hi(jnp.minimum(hi_thr + 1, I16_MAX)))

    def lo_body(kc, c):
        keys = key_ref[chunk(kc), :]
        lo = ((keys & (HALF_RANGE - 1)) + I16_MIN).astype(i16)
        same_hi = lax.shift_right_arithmetic(keys, 16) == hi_thr
        lo_ref[chunk(kc), :] = jnp.where(same_hi, lo, i16(I16_MIN))
        return c
    lax.fori_loop(0, n_chunks, lo_body, 0)
    lo_thr = _kth_largest_16(count16_ge(lo_ref), n_keep - above, (1, tq))
    thr = hi_thr * HALF_RANGE + (lo_thr - I16_MIN)
    thr = jnp.maximum(thr, jnp.int32(INT_MIN + 1))

    def count32(pred):
        def body(kc, acc):
            ind = jnp.where(pred(key_ref[chunk(kc), :], kc * ck + k_off), 1.0, 0.0)
            return acc + jnp.sum(ind.reshape(ck // SUBLANES, SUBLANES, tq), axis=0)
        acc = lax.fori_loop(0, n_chunks, body, jnp.zeros((SUBLANES, tq), f32))
        return jnp.sum(acc, axis=0, keepdims=True)

    n_ge = count32(lambda keys, pos: keys >= thr)

    @pl.when(jnp.max(n_ge) > n_keep)
    def _():
        n_gt = count32(lambda keys, pos: keys > thr)
        kth = (n_ge - n_gt) - (n_keep - n_gt) + 1.0
        cut = _kth_largest_pos(lambda c: count32(lambda keys, pos: (keys == thr) & (pos >= c)),
                               kth, (1, tq), seq.bit_length())

        def demote(kc, c):
            keys = key_ref[chunk(kc), :]
            surplus = (keys == thr) & (kc * ck + k_off > cut)
            key_ref[chunk(kc), :] = jnp.where(surplus, keys - 1, keys)
            return c
        lax.fori_loop(0, n_chunks, demote, 0)

    m_ref[...] = jnp.full_like(m_ref, NEG_BIG)
    acc_ref[...] = jnp.zeros_like(acc_ref)
    hs = GROUP
    q_sets = [jnp.concatenate([aq_ref[:, hd * HEAD_DIM:(hd + 1) * HEAD_DIM] for hd in range(h0, h0 + hs)], axis=0)
              for h0 in range(0, ATT_HEADS, hs)]

    def apply_pv(kc, si):
        g = si * hs // GROUP
        cols = slice(si * hs * tq, (si + 1) * hs * tq)
        r0 = g * (HEAD_DIM + VT_PAD)
        vt = vt_ref[r0:r0 + HEAD_DIM + VT_PAD, chunk(kc)]
        acc_ref[:, cols] = alpha_ref[:, cols] * acc_ref[:, cols] + _dot(vt, p_ref[:, cols])

    def softmax_chunk(kc, with_pv):
        bias = jnp.where(key_ref[chunk(kc), :] >= thr, 0.0, -jnp.inf)
        bias = jnp.concatenate([bias] * hs, axis=1)
        for si, qs in enumerate(q_sets):
            g = si * hs // GROUP
            cols = slice(si * hs * tq, (si + 1) * hs * tq)
            if with_pv:
                apply_pv(kc - 1, si)
            s = _dot_nt(k_ref[chunk(kc), g * HEAD_DIM:(g + 1) * HEAD_DIM], qs) + bias
            m_old = m_ref[:, cols]
            m_new = jnp.maximum(m_old, jnp.max(s, axis=0, keepdims=True))
            p_ref[:, cols] = jnp.exp2(s - m_new).astype(bf16)
            alpha_ref[:, cols] = jnp.exp2(m_old - m_new)
            m_ref[:, cols] = m_new

    softmax_chunk(0, False)

    def attn_body(kc, c):
        softmax_chunk(kc, True)
        return c
    lax.fori_loop(1, n_chunks, attn_body, 0)
    for si in range(len(q_sets)):
        apply_pv(n_chunks - 1, si)

    for hd in range(ATT_HEADS):
        cols = slice(hd * tq, (hd + 1) * tq)
        o_t = acc_ref[:HEAD_DIM, cols] / acc_ref[HEAD_DIM:HEAD_DIM + 1, cols]
        o_ref[:, hd * HEAD_DIM:(hd + 1) * HEAD_DIM] = jnp.transpose(o_t).astype(bf16)


def _prompt_attention(p, ikw, batch, seq, base, tq, n_keep):
    nq = seq // tq
    ck = tq
    kern = functools.partial(_attn_kernel, tq=tq, ck=ck, seq=seq, n_keep=n_keep)
    aq_blk = (base + T_AQ) * TN // ATT_Q_W
    iq_blk = (base + T_IQ) * TN // IDX_Q_W
    return pl.pallas_call(
        kern,
        grid=(batch, nq),
        in_specs=[
            pl.BlockSpec((tq, ATT_Q_W), lambda b, t: (b * nq + t, aq_blk)),
            pl.BlockSpec((seq, ATT_KV_W), lambda b, t: (b, base + T_AK)),
            pl.BlockSpec((seq, ATT_KV_W), lambda b, t: (b, base + T_AV)),
            pl.BlockSpec((tq, IDX_Q_W), lambda b, t: (b * nq + t, iq_blk)),
            pl.BlockSpec((seq, LANES), lambda b, t: (b, 0)),
            pl.BlockSpec((tq, LANES), lambda b, t: (b * nq + t, 0)),
        ],
        out_specs=pl.BlockSpec((tq, ATT_Q_W), lambda b, t: (b * nq + t, 0)),
        out_shape=jax.ShapeDtypeStruct((batch * seq, ATT_Q_W), bf16),
        scratch_shapes=[pltpu.VMEM((seq, tq), jnp.int32), pltpu.VMEM((seq, tq), jnp.int16),
                        pltpu.VMEM((seq, tq), jnp.int16), pltpu.VMEM((seq, LANES), bf16),
                        pltpu.VMEM((seq, LANES), bf16),
                        pltpu.VMEM((ATT_KV_HEADS * (HEAD_DIM + VT_PAD), seq), bf16),
                        pltpu.VMEM((1, ATT_HEADS * tq), f32),
                        pltpu.VMEM((HEAD_DIM + VT_PAD, ATT_HEADS * tq), f32),
                        pltpu.VMEM((ck, ATT_HEADS * tq), bf16), pltpu.VMEM((1, ATT_HEADS * tq), f32)],
        compiler_params=_params(("arbitrary", "arbitrary")),
    )(p, p, p, p, ikw, ikw)


def _oproj_kernel(og_ref, att_ref, gr_ref, ga_ref, x_ref, wr_ref, wa_ref, wo_ref, o_ref):
    ret_out = _dot(og_ref[...], wr_ref[...])
    att_out = _dot(att_ref[...], wa_ref[...])
    merged = gr_ref[...].astype(f32) * ret_out + ga_ref[...].astype(f32) * att_out
    o_ref[...] = x_ref[...] + _dot(merged.astype(bf16), wo_ref[...])


def _oproj(og, att, p, x, wr, wa, wo, tm):
    m, d = x.shape
    const = lambda shape: pl.BlockSpec(shape, lambda i: (0, 0), pipeline_mode=pl.Buffered(1))
    return pl.pallas_call(
        _oproj_kernel,
        grid=(m // tm,),
        in_specs=[
            pl.BlockSpec((tm, RET_V_W), lambda i: (i, 0)),
            pl.BlockSpec((tm, ATT_Q_W), lambda i: (i, 0)),
            pl.BlockSpec((tm, d), lambda i: (i, 0)),
            pl.BlockSpec((tm, d), lambda i: (i, 1)),
            pl.BlockSpec((tm, d), lambda i: (i, 0)),
            const(wr.shape), const(wa.shape), const(wo.shape),
        ],
        out_specs=pl.BlockSpec((tm, d), lambda i: (i, 0)),
        out_shape=jax.ShapeDtypeStruct((m, d), f32),
        compiler_params=_params(("arbitrary",)),
    )(og, att, p, p, x, wr, wa, wo)


def _dec_ret_kernel(lg_ref, q_ref, k_ref, v_ref, sg_ref, gn_ref, st_ref, og_ref, nst_ref):
    for h in range(RET_HEADS):
        lg = lg_ref[h]
        gamma = jnp.exp(lg)
        q = q_ref[0, :, h * RET_QK_DIM:(h + 1) * RET_QK_DIM]
        k = k_ref[0, :, h * RET_QK_DIM:(h + 1) * RET_QK_DIM]
        v = v_ref[0, :, h * RET_V_DIM:(h + 1) * RET_V_DIM]
        sg = sg_ref[0, :, h * RET_V_DIM:(h + 1) * RET_V_DIM]
        state = st_ref[0, 0, h]
        qk = jnp.sum(q * k, axis=1, keepdims=True)
        q8 = jnp.broadcast_to((q * gamma).astype(bf16), (8, RET_QK_DIM))
        cross = _dot(q8, state.astype(bf16))[0:1]
        k_col = jnp.transpose(jnp.broadcast_to(k, (8, RET_QK_DIM)))[:, 0:1]
        nst_ref[0, h] = gamma * state + k_col * v
        o = _rms(qk * v + cross, gn_ref[...])
        og_ref[0, :, h * RET_V_DIM:(h + 1) * RET_V_DIM] = o * sg


def _decode_retention(q, k, v, sg, gn, state, log_g):
    db = q.shape[0]
    row = lambda w: pl.BlockSpec((1, 1, w), lambda b, lg: (b, 0, 0))
    grid_spec = pltpu.PrefetchScalarGridSpec(
        num_scalar_prefetch=1,
        grid=(db,),
        in_specs=[row(RET_QK_W), row(RET_QK_W), row(RET_V_W), row(RET_V_W),
                  pl.BlockSpec((1, RET_V_DIM), lambda b, lg: (0, 0)),
                  pl.BlockSpec((1, 1, RET_HEADS, RET_QK_DIM, RET_V_DIM), lambda b, lg: (0, b, 0, 0, 0))],
        out_specs=[row(RET_V_W),
                   pl.BlockSpec((1, RET_HEADS, RET_QK_DIM, RET_V_DIM), lambda b, lg: (b, 0, 0, 0))],
    )
    return pl.pallas_call(
        _dec_ret_kernel,
        grid_spec=grid_spec,
        out_shape=[jax.ShapeDtypeStruct((db, 1, RET_V_W), f32),
                   jax.ShapeDtypeStruct((db, RET_HEADS, RET_QK_DIM, RET_V_DIM), f32)],
        compiler_params=_params(("arbitrary",)),
    )(log_g, q, k, v, sg, gn, state)


def _dec_score_kernel(pt_ref, qi_ref, w_ref, pool_ref, s_ref, buf_ref, sem_ref, *, n_pages):
    b = pl.program_id(0)
    slot = b % 2

    def page_copy(seq, p, slot_):
        dst = buf_ref.at[slot_, :, pl.ds(pl.multiple_of(p * PAGE_SIZE, PAGE_SIZE), PAGE_SIZE)]
        return pltpu.make_async_copy(pool_ref.at[pt_ref[seq, p]], dst, sem_ref.at[slot_])

    def start_all(seq, slot_):
        def body(p, c):
            page_copy(seq, p, slot_).start()
            return c
        lax.fori_loop(0, n_pages, body, 0)

    @pl.when(b == 0)
    def _():
        start_all(0, 0)

    @pl.when(b + 1 < pl.num_programs(0))
    def _():
        start_all(b + 1, 1 - slot)

    def wait_body(p, c):
        page_copy(b, p, slot).wait()
        return c
    lax.fori_loop(0, n_pages, wait_body, 0)

    s = jnp.maximum(_dot(qi_ref[0], buf_ref[slot].astype(bf16)), 0.0)
    row = jnp.sum(s * w_ref[0], axis=0, keepdims=True)
    for p in range(n_pages):
        s_ref[0, p:p + 1, :] = row[:, p * PAGE_SIZE:(p + 1) * PAGE_SIZE]


def _decode_scores(page_table, qi, w, pool_ikt):
    db, n_pages = page_table.shape
    kern = functools.partial(_dec_score_kernel, n_pages=n_pages)
    grid_spec = pltpu.PrefetchScalarGridSpec(
        num_scalar_prefetch=1,
        grid=(db,),
        in_specs=[
            pl.BlockSpec((1, IDX_HEADS, IDX_DIM), lambda b, pt: (b, 0, 0)),
            pl.BlockSpec((1, IDX_HEADS, 1), lambda b, pt: (b, 0, 0)),
            pl.BlockSpec(memory_space=pl.ANY),
        ],
        out_specs=pl.BlockSpec((1, n_pages, PAGE_SIZE), lambda b, pt: (b, 0, 0)),
        scratch_shapes=[pltpu.VMEM((2, IDX_DIM, n_pages * PAGE_SIZE), f32), pltpu.SemaphoreType.DMA((2,))],
    )
    return pl.pallas_call(
        kern,
        grid_spec=grid_spec,
        out_shape=jax.ShapeDtypeStruct((db, n_pages, PAGE_SIZE), f32),
        compiler_params=_params(("arbitrary",)),
    )(page_table, qi, w, pool_ikt)


def _dec_select_kernel(s_ref, qi_ref, w_ref, ikn_ref, idx_ref, cnt_ref, nsel_ref, rank_ref, sel_ref,
                       *, n_keep, kmax):
    db, n_pages, _ = s_ref.shape
    ikn = ikn_ref[...].astype(bf16).astype(f32)
    dots = jnp.sum(qi_ref[...].astype(f32) * ikn, axis=2, keepdims=True)
    s_new = jnp.sum(jnp.maximum(dots, 0.0) * w_ref[...], axis=1, keepdims=True)
    keys = _sortable_key(s_ref[...])
    key_new = _sortable_key(s_new)

    def count_ge(t):
        c = jnp.sum(jnp.where(keys >= t, 1.0, 0.0), axis=1, keepdims=True)
        return jnp.sum(c, axis=2, keepdims=True) + jnp.where(key_new >= t, 1.0, 0.0)

    thr = _kth_largest_key(count_ge, n_keep, (db, 1, 1))
    sel_ref[...] = jnp.where(keys >= thr, 1.0, 0.0)
    nsel_ref[...] = jnp.broadcast_to(jnp.where(key_new >= thr, 1.0, 0.0), (db, 1, LANES))

    def count3(mask):
        c = jnp.sum(jnp.where(mask, 1.0, 0.0), axis=1, keepdims=True)
        return jnp.sum(c, axis=2, keepdims=True)

    n_ge = count_ge(thr)

    @pl.when(jnp.max(n_ge) > n_keep)
    def _():
        pos = (lax.broadcasted_iota(jnp.int32, keys.shape, 1) * PAGE_SIZE
               + lax.broadcasted_iota(jnp.int32, keys.shape, 2))
        quota = n_keep - (count3(keys > thr) + jnp.where(key_new > thr, 1.0, 0.0))
        eq = keys == thr
        ties_past = count3(eq)
        kth = ties_past - jnp.minimum(quota, ties_past) + 1.0
        cut = _kth_largest_pos(lambda c: count3(eq & (pos >= c)), kth, (db, 1, 1),
                               (n_pages * PAGE_SIZE).bit_length())
        sel_ref[...] = jnp.where((keys > thr) | (eq & (pos <= cut)), 1.0, 0.0)
        new_sel = (key_new > thr) | ((key_new == thr) & (quota > ties_past))
        nsel_ref[...] = jnp.broadcast_to(jnp.where(new_sel, 1.0, 0.0), (db, 1, LANES))

    pp = lax.broadcasted_iota(jnp.int32, (n_pages, n_pages), 0)
    pq = lax.broadcasted_iota(jnp.int32, (n_pages, n_pages), 1)
    earlier_page = jnp.where(pq < pp, 1.0, 0.0).astype(bf16)
    oo = lax.broadcasted_iota(jnp.int32, (PAGE_SIZE, PAGE_SIZE), 0)
    oq = lax.broadcasted_iota(jnp.int32, (PAGE_SIZE, PAGE_SIZE), 1)
    earlier_off = jnp.where(oo < oq, 1.0, 0.0).astype(bf16)
    r_iota = lax.broadcasted_iota(jnp.int32, (kmax, PAGE_SIZE), 0).astype(f32)
    lane = lax.broadcasted_iota(jnp.int32, (kmax, PAGE_SIZE), 1).astype(f32)
    for b in range(db):
        sb = sel_ref[b]
        per_page = jnp.sum(sb, axis=1, keepdims=True)
        before = _dot(earlier_page, jnp.broadcast_to(per_page, (n_pages, PAGE_SIZE)).astype(bf16))
        within = _dot(sb.astype(bf16), earlier_off)
        rank_ref[b] = jnp.where(sb > 0.0, before + within, -1.0)

        def body(p, acc, b=b):
            row = rank_ref[b, pl.ds(p, 1), :]
            return jnp.where(row == r_iota, lane + jnp.asarray(p * PAGE_SIZE, f32), acc)

        acc = lax.fori_loop(0, n_pages, body, jnp.zeros((kmax, PAGE_SIZE), f32))
        idx_ref[b] = jnp.sum(acc, axis=1, keepdims=True).astype(jnp.int32)
        total = jnp.sum(per_page, axis=0, keepdims=True)
        cnt_ref[b] = jnp.broadcast_to(total, (1, LANES)).astype(jnp.int32)


def _decode_select(scores, qi, w, ik_new, n_keep):
    db, n_pages, _ = scores.shape
    kmax = -(-n_keep // 8) * 8
    kern = functools.partial(_dec_select_kernel, n_keep=n_keep, kmax=kmax)
    full = lambda shape: pl.BlockSpec(shape, lambda i: (0,) * len(shape))
    return pl.pallas_call(
        kern,
        grid=(1,),
        in_specs=[full(scores.shape), full(qi.shape), full(w.shape), full(ik_new.shape)],
        out_specs=[full((db, kmax, 1)), full((db, 1, LANES)), full((db, 1, LANES))],
        out_shape=[jax.ShapeDtypeStruct((db, kmax, 1), jnp.int32),
                   jax.ShapeDtypeStruct((db, 1, LANES), jnp.int32),
                   jax.ShapeDtypeStruct((db, 1, LANES), f32)],
        scratch_shapes=[pltpu.VMEM((db, n_pages, PAGE_SIZE), f32), pltpu.VMEM((db, n_pages, PAGE_SIZE), f32)],
        compiler_params=_params(("arbitrary",)),
    )(scores, qi, w, ik_new)


def _dec_attn_kernel(idx_ref, pt_ref, cnt_ref, q_ref, kn_ref, vn_ref, nsel_ref, kpool_ref, vpool_ref, o_ref,
                     kbuf_ref, vbuf_ref, sem_ref, *, kmax):
    b = pl.program_id(0)
    slot = b % 2
    page_shift = PAGE_SIZE.bit_length() - 1

    def row_copies(seq, r, slot_):
        pos = idx_ref[seq, r]
        phys = pt_ref[seq, lax.shift_right_logical(pos, page_shift)]
        src0 = pl.multiple_of((pos & (PAGE_SIZE - 1)) * ATT_KV_HEADS, ATT_KV_HEADS)
        dst0 = pl.multiple_of(r * ATT_KV_HEADS, ATT_KV_HEADS)
        return (pltpu.make_async_copy(kpool_ref.at[phys, pl.ds(src0, ATT_KV_HEADS), :],
                                      kbuf_ref.at[slot_, pl.ds(dst0, ATT_KV_HEADS), :], sem_ref.at[0, slot_]),
                pltpu.make_async_copy(vpool_ref.at[phys, pl.ds(src0, ATT_KV_HEADS), :],
                                      vbuf_ref.at[slot_, pl.ds(dst0, ATT_KV_HEADS), :], sem_ref.at[1, slot_]))

    def start_all(seq, slot_):
        def body(r, c):
            ck, cv = row_copies(seq, r, slot_)
            ck.start()
            cv.start()
            return c
        lax.fori_loop(0, kmax, body, 0, unroll=8)

    @pl.when(b == 0)
    def _():
        start_all(0, 0)

    @pl.when(b + 1 < pl.num_programs(0))
    def _():
        start_all(b + 1, 1 - slot)

    def wait_body(r, c):
        ck, cv = row_copies(b, r, slot)
        ck.wait()
        cv.wait()
        return c
    lax.fori_loop(0, kmax, wait_body, 0, unroll=8)

    q = q_ref[0]
    s = _dot_nt(q, kbuf_ref[slot].astype(bf16))
    col = lax.broadcasted_iota(jnp.int32, s.shape, 1)
    head = lax.broadcasted_iota(jnp.int32, s.shape, 0)
    ok = (col % ATT_KV_HEADS == head // GROUP) & (col // ATT_KV_HEADS < cnt_ref[b])
    s = jnp.where(ok, s, -jnp.inf)
    s_n = jnp.sum(q.astype(f32) * kn_ref[0].astype(f32), axis=1, keepdims=True)
    s_n = jnp.where(nsel_ref[0][:, 0:1] > 0.0, s_n, -jnp.inf)
    m = jnp.maximum(jnp.max(s, axis=1, keepdims=True), s_n)
    p = jnp.exp2(s - m)
    p_n = jnp.exp2(s_n - m)
    l = jnp.sum(p, axis=1, keepdims=True) + p_n
    pv = _dot(p.astype(bf16), vbuf_ref[slot].astype(bf16))
    o_ref[0] = (pv + p_n.astype(bf16).astype(f32) * vn_ref[0].astype(f32)) / l


def _decode_attention(idx, page_table, cnt, q, k_new, v_new, nsel, pool_k, pool_v):
    db, kmax = idx.shape
    kern = functools.partial(_dec_attn_kernel, kmax=kmax)
    seqb = lambda shape: pl.BlockSpec(shape, lambda b, *_: (b, 0, 0))
    grid_spec = pltpu.PrefetchScalarGridSpec(
        num_scalar_prefetch=3,
        grid=(db,),
        in_specs=[
            seqb((1, ATT_HEADS, HEAD_DIM)),
            seqb((1, ATT_HEADS, HEAD_DIM)),
            seqb((1, ATT_HEADS, HEAD_DIM)),
            seqb((1, 1, LANES)),
            pl.BlockSpec(memory_space=pl.ANY),
            pl.BlockSpec(memory_space=pl.ANY),
        ],
        out_specs=seqb((1, ATT_HEADS, HEAD_DIM)),
        scratch_shapes=[pltpu.VMEM((2, kmax * ATT_KV_HEADS, HEAD_DIM), f32),
                        pltpu.VMEM((2, kmax * ATT_KV_HEADS, HEAD_DIM), f32),
                        pltpu.SemaphoreType.DMA((2, 2))],
    )
    return pl.pallas_call(
        kern,
        grid_spec=grid_spec,
        out_shape=jax.ShapeDtypeStruct((db, ATT_HEADS, HEAD_DIM), f32),
        compiler_params=_params(("arbitrary",)),
    )(idx, page_table, cnt, q, k_new, v_new, nsel, pool_k, pool_v)


def _rope_tables(pos):
    half = RET_QK_DIM // 2
    inv = ROPE_BASE ** (-jnp.arange(half, dtype=f32) / half)
    ang = pos.astype(f32)[:, None] * inv[None, :]
    cos, sin = jnp.cos(ang), jnp.sin(ang)
    return jnp.concatenate([cos, cos], axis=1), jnp.concatenate([-sin, sin], axis=1)


def _pick(n, pref):
    for t in pref:
        if n % t == 0:
            return t
    return n


def kernel(x_prompt, x_sample, state_ret, cache_k, cache_v, cache_idx_k, page_table, ffn1_norm, ffn1_w1, ffn1_w2, mix_norm, w_in, q_norm, k_norm, ret_norm, w_ret_out, w_att_out, w_o, ffn2_norm, ffn2_w1, ffn2_w2):
    batch, seq, d = x_prompt.shape
    db, ts, _ = x_sample.shape
    depth = w_in.shape[0]
    assert depth == 1 and ts == 1 and d % (2 * TN) == 0
    n_pages = page_table.shape[1]
    past = n_pages * PAGE_SIZE
    n_phys = cache_k.shape[1]
    base = 2 * (d // TN)
    m = batch * seq
    ms = BF16_ROWS

    log_g = jnp.log1p(-jnp.exp2(-5.0 - jnp.arange(RET_HEADS, dtype=f32)))
    w1a, w2a = ffn1_w1[0].astype(bf16), ffn1_w2[0].astype(bf16)
    w1b, w2b = ffn2_w1[0].astype(bf16), ffn2_w2[0].astype(bf16)
    w_pad = jnp.swapaxes(w_in[0], 0, 1).astype(bf16)
    wr, wa, wo = w_ret_out[0].astype(bf16), w_att_out[0].astype(bf16), w_o[0].astype(bf16)
    g1, g2, g3 = ffn1_norm, mix_norm, ffn2_norm
    qn, kn, gn = q_norm, k_norm, ret_norm

    tm = _pick(m, (512, 256, 128))
    tf = _pick(ffn1_w2.shape[1], (512, 256, 128))
    tmp = _pick(seq, (1024, 512, 256, 128))
    chunk = _pick(seq, (256, 128))
    tq = _pick(seq, (256, 128))
    tmo = _pick(m, (256, 128))

    xp = x_prompt.reshape(m, d)
    x1 = _ffn(xp, g1, w1a, w2a, tm, tf)
    cos_p, sin_p = _rope_tables(jnp.arange(seq))
    p, k32, v32, ikw = _proj(x1, g2, w_pad, cos_p, sin_p, qn, kn, tmp)
    og, st_p = _retention(p, log_g, gn, batch, seq, base, chunk)
    att = _prompt_attention(p, ikw, batch, seq, base, tq, min(TOPK_MAX, seq // 4))
    x2 = _oproj(og, att, p, x1, wr, wa, wo, tmo)
    yp = _ffn(x2, g3, w1b, w2b, tm, tf).reshape(batch, seq, d)

    xs = jnp.zeros((ms, d), f32).at[:db].set(x_sample.reshape(db, d))
    s1 = _ffn(xs, g1, w1a, w2a, ms, tf)
    cos_s, sin_s = _rope_tables(jnp.full((ms,), past, jnp.int32))
    ps, k32s, v32s, ikws = _proj(s1, g2, w_pad, cos_s, sin_s, qn, kn, ms)
    seg = lambda t0, t1: ps[:db, (base + t0) * TN:(base + t1) * TN]
    row3 = lambda a: a.astype(f32).reshape(db, 1, a.shape[-1])
    og_s, st_s = _decode_retention(row3(seg(T_RQ, T_RK)), row3(seg(T_RK, T_RV)), row3(seg(T_RV, T_RG)),
                                   row3(seg(T_RG, T_AQ)), gn, state_ret, log_g)
    qi = seg(T_IQ, T_IKW).reshape(db, IDX_HEADS, IDX_DIM)
    wi = (ikws[:db, IDX_DIM:IDX_DIM + IDX_HEADS] * IDX_W_SCALE).reshape(db, IDX_HEADS, 1)
    ik_new = ikws[:db, :IDX_DIM].reshape(db, 1, IDX_DIM)
    pool_ikt = jnp.swapaxes(cache_idx_k.reshape(n_phys, PAGE_SIZE, IDX_DIM), 1, 2)
    scores = _decode_scores(page_table, qi, wi, pool_ikt)
    idx, cnt, nsel = _decode_select(scores, qi, wi, ik_new, min(TOPK_MAX, (past + ts) // 4))
    aq_s = seg(T_AQ, T_AK).reshape(db, ATT_HEADS, HEAD_DIM)
    new_kv = lambda a: a[:db * ATT_KV_HEADS].reshape(db, ATT_KV_HEADS, HEAD_DIM)
    expand = lambda a: jnp.repeat(new_kv(a), GROUP, axis=1).astype(bf16)
    pool_rows = lambda c: c.reshape(n_phys, PAGE_SIZE * ATT_KV_HEADS, HEAD_DIM)
    att_s = _decode_attention(idx.reshape(db, -1), page_table, cnt[:, 0, 0], aq_s, expand(k32s), expand(v32s),
                              nsel, pool_rows(cache_k), pool_rows(cache_v))
    pad_rows = lambda a: jnp.pad(a.reshape(db, -1).astype(bf16), ((0, ms - db), (0, 0)))
    s2 = _oproj(pad_rows(og_s), pad_rows(att_s), ps, s1, wr, wa, wo, ms)
    ys = _ffn(s2, g3, w1b, w2b, ms, tf)[:db].reshape(db, ts, d)

    return (yp, ys,
            st_p[None],
            k32.reshape(1, batch, seq, ATT_KV_HEADS, HEAD_DIM),
            v32.reshape(1, batch, seq, ATT_KV_HEADS, HEAD_DIM),
            ikw[:, :IDX_DIM].reshape(1, batch, seq, IDX_DIM),
            st_s[None],
            new_kv(k32s).reshape(1, db, ts, ATT_KV_HEADS, HEAD_DIM),
            new_kv(v32s).reshape(1, db, ts, ATT_KV_HEADS, HEAD_DIM),
            ikws[:db, :IDX_DIM].reshape(1, db, ts, IDX_DIM))
```

```python
import functools

import jax
import jax.numpy as jnp
from jax import lax
from jax.experimental import pallas as pl
from jax.experimental.pallas import tpu as pltpu

RET_HEADS = 8
RET_QK_DIM = 128
RET_V_DIM = 256
ATT_HEADS = 16
ATT_KV_HEADS = 4
HEAD_DIM = 128
GROUP = ATT_HEADS // ATT_KV_HEADS
IDX_HEADS = 16
IDX_DIM = 64
IDX_W_SCALE = (IDX_HEADS ** -0.5) * (IDX_DIM ** -0.5)
TOPK_MAX = 256
PAGE_SIZE = 128
ROPE_BASE = 10000.0
EPS = 1e-6

RET_QK_W = RET_HEADS * RET_QK_DIM
RET_V_W = RET_HEADS * RET_V_DIM
ATT_Q_W = ATT_HEADS * HEAD_DIM
ATT_KV_W = ATT_KV_HEADS * HEAD_DIM
IDX_Q_W = IDX_HEADS * IDX_DIM
MAIN_W = 2 * RET_QK_W + 2 * RET_V_W + ATT_Q_W + 2 * ATT_KV_W + IDX_Q_W

LANES = 128
TN = 512
T_RQ, T_RK, T_RV, T_RG, T_AQ, T_AK, T_AV, T_IQ, T_IKW, T_END = 0, 2, 4, 8, 12, 16, 17, 18, 20, 21
VMEM_LIMIT = 56 * 1024 * 1024
INT_MIN = -2 ** 31
I16_MIN, I16_MAX = -2 ** 15, 2 ** 15 - 1
HALF_RANGE = 2 ** 16
SUBLANES = 8
BF16_ROWS = 16
NEG_BIG = -1e30
LOG2E = 1.4426950408889634
VT_PAD = 16

bf16 = jnp.bfloat16
f32 = jnp.float32


def _params(sem, vmem=VMEM_LIMIT):
    return pltpu.CompilerParams(dimension_semantics=sem, vmem_limit_bytes=vmem)


def _sigmoid(x):
    return 1.0 / (1.0 + jnp.exp(-x))


def _dot(a, b):
    return jnp.dot(a, b, preferred_element_type=f32)


def _dot_nt(a, b):
    return lax.dot_general(a, b, (((1,), (1,)), ((), ())), preferred_element_type=f32)


def _rms(x, gain):
    ms = jnp.mean(x * x, axis=-1, keepdims=True)
    return x * lax.rsqrt(ms + EPS) * gain


def _ffn_kernel(x_ref, g_ref, w1g_ref, w1u_ref, w2_ref, o_ref, h_ref):
    j = pl.program_id(1)

    @pl.when(j == 0)
    def _():
        h_ref[...] = _rms(x_ref[...], g_ref[...]).astype(bf16)
        o_ref[...] = jnp.zeros_like(o_ref)

    h = h_ref[...]
    gate = _dot(h, w1g_ref[...])
    up = _dot(h, w1u_ref[...])
    act = (gate * _sigmoid(gate) * up).astype(bf16)
    o_ref[...] += _dot(act, w2_ref[...])

    @pl.when(j == pl.num_programs(1) - 1)
    def _():
        o_ref[...] = x_ref[...] + 0.5 * o_ref[...]


def _ffn(x, gain, w1, w2, tm, tf):
    m, d = x.shape
    dff = w2.shape[0]
    nf = dff // tf
    return pl.pallas_call(
        _ffn_kernel,
        grid=(m // tm, nf),
        in_specs=[
            pl.BlockSpec((tm, d), lambda i, j: (i, 0)),
            pl.BlockSpec((1, d), lambda i, j: (0, 0)),
            pl.BlockSpec((d, tf), lambda i, j: (0, j)),
            pl.BlockSpec((d, tf), lambda i, j: (0, j + nf)),
            pl.BlockSpec((tf, d), lambda i, j: (j, 0)),
        ],
        out_specs=pl.BlockSpec((tm, d), lambda i, j: (i, 0)),
        out_shape=jax.ShapeDtypeStruct((m, d), f32),
        scratch_shapes=[pltpu.VMEM((tm, d), bf16)],
        compiler_params=_params(("arbitrary", "arbitrary")),
    )(x, gain, w1, w1, w2)


def _proj_kernel(x_ref, g_ref, w_ref, cos_ref, sin_ref, qn_ref, kn_ref,
                 p_ref, k32_ref, v32_ref, ikw_ref, h_ref, *, base):
    j = pl.program_id(1)

    @pl.when(j == 0)
    def _():
        h_ref[...] = _rms(x_ref[...], g_ref[...]).astype(bf16)

    w = w_ref[...]
    heads = [slice(c * LANES, (c + 1) * LANES) for c in range(TN // LANES)]
    tm = h_ref.shape[0]
    rc = min(tm, 256)

    def segment(cond, epilogue):
        @pl.when(cond)
        def _():
            for r0 in range(0, tm, rc):
                rows = slice(r0, r0 + rc)
                epilogue(_dot_nt(h_ref[rows, :], w), rows)

    def kv_rows(rows, g):
        return pl.ds(rows.start * ATT_KV_HEADS + g, rows.stop - rows.start, stride=ATT_KV_HEADS)

    def gates(res, rows):
        p_ref[rows, :] = _sigmoid(res).astype(bf16)

    def rotary(res, rows):
        cos, sin = cos_ref[rows, :], sin_ref[rows, :]
        scale = jnp.where(j >= base + T_RK, RET_QK_DIM ** -0.5, 1.0).astype(f32)
        for sl in heads:
            x = res[:, sl]
            p_ref[rows, sl] = ((x * cos + pltpu.roll(x, LANES // 2, 1) * sin) * scale).astype(bf16)

    def plain(res, rows):
        p_ref[rows, :] = res.astype(bf16)

    def silu(res, rows):
        p_ref[rows, :] = (res * _sigmoid(res)).astype(bf16)

    def q_norm(res, rows):
        for sl in heads:
            p_ref[rows, sl] = (_rms(res[:, sl], qn_ref[...]) * (HEAD_DIM ** -0.5 * LOG2E)).astype(bf16)

    def k_norm(res, rows):
        for g, sl in enumerate(heads):
            y = _rms(res[:, sl], kn_ref[...])
            k32_ref[kv_rows(rows, g), :] = y
            p_ref[rows, sl] = y.astype(bf16)

    def value(res, rows):
        for g, sl in enumerate(heads):
            v32_ref[kv_rows(rows, g), :] = res[:, sl]
        p_ref[rows, :] = res.astype(bf16)

    def idx_kw(res, rows):
        ikw_ref[rows, :] = res[:, :LANES]
        p_ref[rows, :] = res.astype(bf16)

    segment(j < base, gates)
    segment((j >= base + T_RQ) & (j < base + T_RV), rotary)
    segment(((j >= base + T_RV) & (j < base + T_RG)) | ((j >= base + T_IQ) & (j < base + T_IKW)), plain)
    segment((j >= base + T_RG) & (j < base + T_AQ), silu)
    segment((j >= base + T_AQ) & (j < base + T_AK), q_norm)
    segment(j == base + T_AK, k_norm)
    segment(j == base + T_AV, value)
    segment(j == base + T_IKW, idx_kw)


def _proj(x, gain, w_t, cos2, sin2, qn, kn, tm):
    m, d = x.shape
    gate_row0 = MAIN_W + IDX_DIM + IDX_HEADS
    base = (w_t.shape[0] - gate_row0) // TN
    nt = base + T_END
    tab_blocks = cos2.shape[0] // tm
    kern = functools.partial(_proj_kernel, base=base)
    al = BF16_ROWS
    assert gate_row0 % al == 0
    w_row = lambda j: al * jnp.where(j < base, gate_row0 // al + j * (TN // al), (j - base) * (TN // al))
    return pl.pallas_call(
        kern,
        grid=(m // tm, nt),
        in_specs=[
            pl.BlockSpec((tm, d), lambda i, j: (i, 0)),
            pl.BlockSpec((1, d), lambda i, j: (0, 0)),
            pl.BlockSpec((pl.Element(TN), pl.Element(d)), lambda i, j: (w_row(j), 0)),
            pl.BlockSpec((tm, LANES), lambda i, j: (i % tab_blocks, 0)),
            pl.BlockSpec((tm, LANES), lambda i, j: (i % tab_blocks, 0)),
            pl.BlockSpec((1, LANES), lambda i, j: (0, 0)),
            pl.BlockSpec((1, LANES), lambda i, j: (0, 0)),
        ],
        out_specs=[
            pl.BlockSpec((tm, TN), lambda i, j: (i, j)),
            pl.BlockSpec((tm * ATT_KV_HEADS, HEAD_DIM), lambda i, j: (i, 0)),
            pl.BlockSpec((tm * ATT_KV_HEADS, HEAD_DIM), lambda i, j: (i, 0)),
            pl.BlockSpec((tm, LANES), lambda i, j: (i, 0)),
        ],
        out_shape=[
            jax.ShapeDtypeStruct((m, nt * TN), bf16),
            jax.ShapeDtypeStruct((m * ATT_KV_HEADS, HEAD_DIM), f32),
            jax.ShapeDtypeStruct((m * ATT_KV_HEADS, HEAD_DIM), f32),
            jax.ShapeDtypeStruct((m, LANES), f32),
        ],
        scratch_shapes=[pltpu.VMEM((tm, d), bf16)],
        compiler_params=_params(("arbitrary", "arbitrary")),
    )(x, gain, w_t, cos2, sin2, qn, kn)


def _ret_kernel(lg_ref, q_ref, k_ref, v_ref, sg_ref, gn_ref, og_ref, st_ref, state_ref, dmat_ref, *, chunk):
    c = pl.program_id(1)

    @pl.when(c == 0)
    def _():
        state_ref[...] = jnp.zeros_like(state_ref)

    @pl.when((pl.program_id(0) == 0) & (c == 0))
    def _():
        ii = lax.broadcasted_iota(jnp.int32, (chunk, chunk), 0)
        jj = lax.broadcasted_iota(jnp.int32, (chunk, chunk), 1)
        rel = (ii - jj).astype(f32)
        for h in range(RET_HEADS):
            dmat_ref[h] = jnp.where(rel >= 0, jnp.exp(lg_ref[h] * jnp.maximum(rel, 0.0)), 0.0)

    i1 = lax.broadcasted_iota(jnp.int32, (chunk, 1), 0).astype(f32)
    for h in range(RET_HEADS):
        lg = lg_ref[h]
        qk_cols = slice(h * RET_QK_DIM, (h + 1) * RET_QK_DIM)
        v_cols = slice(h * RET_V_DIM, (h + 1) * RET_V_DIM)
        q, k, v = q_ref[:, qk_cols], k_ref[:, qk_cols], v_ref[:, v_cols]
        scores = _dot_nt(q, k) * dmat_ref[h]
        inner = _dot(scores.astype(bf16), v)
        q_dec = jnp.exp(lg * (i1 + 1.0))
        k_dec = jnp.exp(lg * (chunk - 1.0 - i1))
        state = state_ref[h]
        cross = _dot((q.astype(f32) * q_dec).astype(bf16), state.astype(bf16))
        kd = (k.astype(f32) * k_dec).astype(bf16)
        new_state = jnp.exp(lg * chunk) * state + lax.dot_general(
            kd, v, (((0,), (0,)), ((), ())), preferred_element_type=f32)
        state_ref[h] = new_state
        o = _rms(inner + cross, gn_ref[...])
        og_ref[:, v_cols] = (o * sg_ref[:, v_cols].astype(f32)).astype(bf16)

    @pl.when(c == pl.num_programs(1) - 1)
    def _():
        st_ref[0] = state_ref[...]


def _retention(p, log_g, gn, batch, seq, base, chunk):
    nc = seq // chunk
    qb = (base + T_RQ) * TN // RET_QK_W
    kb = (base + T_RK) * TN // RET_QK_W
    vb = (base + T_RV) * TN // RET_V_W
    gb = (base + T_RG) * TN // RET_V_W
    kern = functools.partial(_ret_kernel, chunk=chunk)
    grid_spec = pltpu.PrefetchScalarGridSpec(
        num_scalar_prefetch=1,
        grid=(batch, nc),
        in_specs=[
            pl.BlockSpec((chunk, RET_QK_W), lambda b, c, lg: (b * nc + c, qb)),
            pl.BlockSpec((chunk, RET_QK_W), lambda b, c, lg: (b * nc + c, kb)),
            pl.BlockSpec((chunk, RET_V_W), lambda b, c, lg: (b * nc + c, vb)),
            pl.BlockSpec((chunk, RET_V_W), lambda b, c, lg: (b * nc + c, gb)),
            pl.BlockSpec((1, RET_V_DIM), lambda b, c, lg: (0, 0)),
        ],
        out_specs=[
            pl.BlockSpec((chunk, RET_V_W), lambda b, c, lg: (b * nc + c, 0)),
            pl.BlockSpec((1, RET_HEADS, RET_QK_DIM, RET_V_DIM), lambda b, c, lg: (b, 0, 0, 0)),
        ],
        scratch_shapes=[pltpu.VMEM((RET_HEADS, RET_QK_DIM, RET_V_DIM), f32),
                        pltpu.VMEM((RET_HEADS, chunk, chunk), f32)],
    )
    return pl.pallas_call(
        kern,
        grid_spec=grid_spec,
        out_shape=[
            jax.ShapeDtypeStruct((batch * seq, RET_V_W), bf16),
            jax.ShapeDtypeStruct((batch, RET_HEADS, RET_QK_DIM, RET_V_DIM), f32),
        ],
        compiler_params=_params(("arbitrary", "arbitrary")),
    )(log_g, p, p, p, p, gn)


def _sortable_key(score):
    bits = lax.bitcast_convert_type(score, jnp.int32)
    return jnp.where(bits >= 0, bits, bits ^ jnp.int32(0x7FFFFFFF))


def _kth_largest_key(count_ge, n_keep, shape):
    def body(it, t_u):
        bit = jnp.left_shift(jnp.int32(1), 31 - it)
        cand_u = t_u | bit
        cnt = count_ge(cand_u ^ jnp.int32(INT_MIN))
        return jnp.where(cnt >= n_keep, cand_u, t_u)
    t_u = lax.fori_loop(0, 32, body, jnp.zeros(shape, jnp.int32))
    return t_u ^ jnp.int32(INT_MIN)


def _kth_largest_pos(count_ge, kth, shape, bits):
    def body(it, t):
        cand = t | jnp.left_shift(jnp.int32(1), bits - 1 - it)
        return jnp.where(count_ge(cand) >= kth, cand, t)
    return lax.fori_loop(0, bits, body, jnp.zeros(shape, jnp.int32))


def _kth_largest_16(count_ge, n_keep, shape):
    def body(it, t_u):
        cand_u = t_u | jnp.left_shift(jnp.int32(1), 15 - it)
        cnt = count_ge(cand_u + I16_MIN)
        return jnp.where(cnt >= n_keep, cand_u, t_u)
    t_u = lax.fori_loop(0, 16, body, jnp.zeros(shape, jnp.int32))
    return t_u + I16_MIN


def _attn_kernel(aq_ref, k_ref, v_ref, iq_ref, ikw_k_ref, ikw_q_ref, o_ref,
                 key_ref, hi_ref, lo_ref, iklo_ref, ikhi_ref, vt_ref, m_ref, acc_ref, p_ref, alpha_ref,
                 *, tq, ck, seq, n_keep):
    qb = pl.program_id(1)

    @pl.when(qb == 0)
    def _():
        ikw = ikw_k_ref[...]
        lane = lax.broadcasted_iota(jnp.int32, (seq, LANES), 1)
        iklo_ref[...] = jnp.where(lane < IDX_DIM, ikw, 0.0).astype(bf16)
        ikhi_ref[...] = jnp.where(lane >= IDX_DIM, pltpu.roll(ikw, IDX_DIM, 1), 0.0).astype(bf16)

        for c0 in range(0, seq, ck):
            v_t = jnp.transpose(v_ref[c0:c0 + ck, :].astype(f32)).astype(bf16)
            for g in range(ATT_KV_HEADS):
                r0 = g * (HEAD_DIM + VT_PAD)
                vt_ref[r0:r0 + HEAD_DIM, c0:c0 + ck] = v_t[g * HEAD_DIM:(g + 1) * HEAD_DIM]
                vt_ref[r0 + HEAD_DIM:r0 + HEAD_DIM + VT_PAD, c0:c0 + ck] = jnp.ones((VT_PAD, ck), bf16)

    n_chunks = (qb * tq + tq - 1) // ck + 1
    chunk = lambda kc: pl.ds(kc * ck if isinstance(kc, int) else pl.multiple_of(kc * ck, ck), ck)
    stack = 4
    pairs = IDX_HEADS // 2

    iw_t = jnp.transpose(ikw_q_ref[...]) * IDX_W_SCALE
    q_stacks = [jnp.concatenate([iq_ref[:, hp * LANES:(hp + 1) * LANES] for hp in range(h0, h0 + stack)], axis=0)
                for h0 in range(0, pairs, stack)]
    k_off = lax.broadcasted_iota(jnp.int32, (ck, tq), 0)
    q_pos = qb * tq + lax.broadcasted_iota(jnp.int32, (ck, tq), 1)

    def score_body(kc, c):
        scores = jnp.zeros((ck, tq), f32)
        for si, qs in enumerate(q_stacks):
            for sub, ik_ref in enumerate((iklo_ref, ikhi_ref)):
                s = jnp.maximum(_dot_nt(ik_ref[chunk(kc), :], qs), 0.0)
                for i in range(stack):
                    row = IDX_DIM + 2 * (si * stack + i) + sub
                    scores = scores + s[:, i * tq:(i + 1) * tq] * iw_t[row:row + 1, :]
        visible = kc * ck + k_off <= q_pos
        key_ref[chunk(kc), :] = jnp.where(visible, _sortable_key(scores), jnp.int32(INT_MIN))
        return c
    lax.fori_loop(0, n_chunks, score_body, 0)

    i16 = jnp.int16
    sub = BF16_ROWS

    def count16_ge(half_ref):
        def count(t):
            t_b = jnp.broadcast_to(t, (sub, tq)).astype(i16)

            def body(kc, acc):
                x = half_ref[chunk(kc), :]
                for r0 in range(0, ck, sub):
                    acc = acc + jnp.where(x[r0:r0 + sub] >= t_b, i16(1), i16(0))
                return acc
            acc = lax.fori_loop(0, n_chunks, body, jnp.zeros((sub, tq), i16))
            return jnp.sum(acc.astype(f32), axis=0, keepdims=True)
        return count

    def hi_body(kc, c):
        hi_ref[chunk(kc), :] = lax.shift_right_arithmetic(key_ref[chunk(kc), :], 16).astype(i16)
        return c
    lax.fori_loop(0, n_chunks, hi_body, 0)
    count_hi = count16_ge(hi_ref)
    hi_thr = _kth_largest_16(count_hi, float(n_keep), (1, tq))
    above = jnp.where(hi_thr >= I16_MAX, 0.0, count_hi(jnp.minimum(hi_thr + 1, I16_MAX)))

    def lo_body(kc, c):
        keys = key_ref[chunk(kc), :]
        lo = ((keys & (HALF_RANGE - 1)) + I16_MIN).astype(i16)
        same_hi = lax.shift_right_arithmetic(keys, 16) == hi_thr
        lo_ref[chunk(kc), :] = jnp.where(same_hi, lo, i16(I16_MIN))
        return c
    lax.fori_loop(0, n_chunks, lo_body, 0)
    lo_thr = _kth_largest_16(count16_ge(lo_ref), n_keep - above, (1, tq))
    thr = hi_thr * HALF_RANGE + (lo_thr - I16_MIN)
    thr = jnp.maximum(thr, jnp.int32(INT_MIN + 1))

    def count32(pred):
        def body(kc, acc):
            ind = jnp.where(pred(key_ref[chunk(kc), :], kc * ck + k_off), 1.0, 0.0)
            return acc + jnp.sum(ind.reshape(ck // SUBLANES, SUBLANES, tq), axis=0)
        acc = lax.fori_loop(0, n_chunks, body, jnp.zeros((SUBLANES, tq), f32))
        return jnp.sum(acc, axis=0, keepdims=True)

    n_ge = count32(lambda keys, pos: keys >= thr)

    @pl.when(jnp.max(n_ge) > n_keep)
    def _():
        n_gt = count32(lambda keys, pos: keys > thr)
        kth = (n_ge - n_gt) - (n_keep - n_gt) + 1.0
        cut = _kth_largest_pos(lambda c: count32(lambda keys, pos: (keys == thr) & (pos >= c)),
                               kth, (1, tq), seq.bit_length())

        def demote(kc, c):
            keys = key_ref[chunk(kc), :]
            surplus = (keys == thr) & (kc * ck + k_off > cut)
            key_ref[chunk(kc), :] = jnp.where(surplus, keys - 1, keys)
            return c
        lax.fori_loop(0, n_chunks, demote, 0)

    m_ref[...] = jnp.full_like(m_ref, NEG_BIG)
    acc_ref[...] = jnp.zeros_like(acc_ref)
    hs = GROUP
    q_sets = [jnp.concatenate([aq_ref[:, hd * HEAD_DIM:(hd + 1) * HEAD_DIM] for hd in range(h0, h0 + hs)], axis=0)
              for h0 in range(0, ATT_HEADS, hs)]

    def apply_pv(kc, si):
        g = si * hs // GROUP
        cols = slice(si * hs * tq, (si + 1) * hs * tq)
        r0 = g * (HEAD_DIM + VT_PAD)
        vt = vt_ref[r0:r0 + HEAD_DIM + VT_PAD, chunk(kc)]
        acc_ref[:, cols] = alpha_ref[:, cols] * acc_ref[:, cols] + _dot(vt, p_ref[:, cols])

    def softmax_chunk(kc, with_pv):
        bias = jnp.where(key_ref[chunk(kc), :] >= thr, 0.0, -jnp.inf)
        bias = jnp.concatenate([bias] * hs, axis=1)
        for si, qs in enumerate(q_sets):
            g = si * hs // GROUP
            cols = slice(si * hs * tq, (si + 1) * hs * tq)
            if with_pv:
                apply_pv(kc - 1, si)
            s = _dot_nt(k_ref[chunk(kc), g * HEAD_DIM:(g + 1) * HEAD_DIM], qs) + bias
            m_old = m_ref[:, cols]
            m_new = jnp.maximum(m_old, jnp.max(s, axis=0, keepdims=True))
            p_ref[:, cols] = jnp.exp2(s - m_new).astype(bf16)
            alpha_ref[:, cols] = jnp.exp2(m_old - m_new)
            m_ref[:, cols] = m_new

    softmax_chunk(0, False)

    def attn_body(kc, c):
        softmax_chunk(kc, True)
        return c
    lax.fori_loop(1, n_chunks, attn_body, 0)
    for si in range(len(q_sets)):
        apply_pv(n_chunks - 1, si)

    for hd in range(ATT_HEADS):
        cols = slice(hd * tq, (hd + 1) * tq)
        o_t = acc_ref[:HEAD_DIM, cols] / acc_ref[HEAD_DIM:HEAD_DIM + 1, cols]
        o_ref[:, hd * HEAD_DIM:(hd + 1) * HEAD_DIM] = jnp.transpose(o_t).astype(bf16)


def _prompt_attention(p, ikw, batch, seq, base, tq, n_keep):
    nq = seq // tq
    ck = tq
    kern = functools.partial(_attn_kernel, tq=tq, ck=ck, seq=seq, n_keep=n_keep)
    aq_blk = (base + T_AQ) * TN // ATT_Q_W
    iq_blk = (base + T_IQ) * TN // IDX_Q_W
    return pl.pallas_call(
        kern,
        grid=(batch, nq),
        in_specs=[
            pl.BlockSpec((tq, ATT_Q_W), lambda b, t: (b * nq + t, aq_blk)),
            pl.BlockSpec((seq, ATT_KV_W), lambda b, t: (b, base + T_AK)),
            pl.BlockSpec((seq, ATT_KV_W), lambda b, t: (b, base + T_AV)),
            pl.BlockSpec((tq, IDX_Q_W), lambda b, t: (b * nq + t, iq_blk)),
            pl.BlockSpec((seq, LANES), lambda b, t: (b, 0)),
            pl.BlockSpec((tq, LANES), lambda b, t: (b * nq + t, 0)),
        ],
        out_specs=pl.BlockSpec((tq, ATT_Q_W), lambda b, t: (b * nq + t, 0)),
        out_shape=jax.ShapeDtypeStruct((batch * seq, ATT_Q_W), bf16),
        scratch_shapes=[pltpu.VMEM((seq, tq), jnp.int32), pltpu.VMEM((seq, tq), jnp.int16),
                        pltpu.VMEM((seq, tq), jnp.int16), pltpu.VMEM((seq, LANES), bf16),
                        pltpu.VMEM((seq, LANES), bf16),
                        pltpu.VMEM((ATT_KV_HEADS * (HEAD_DIM + VT_PAD), seq), bf16),
                        pltpu.VMEM((1, ATT_HEADS * tq), f32),
                        pltpu.VMEM((HEAD_DIM + VT_PAD, ATT_HEADS * tq), f32),
                        pltpu.VMEM((ck, ATT_HEADS * tq), bf16), pltpu.VMEM((1, ATT_HEADS * tq), f32)],
        compiler_params=_params(("arbitrary", "arbitrary")),
    )(p, p, p, p, ikw, ikw)


def _oproj_kernel(og_ref, att_ref, gr_ref, ga_ref, x_ref, wr_ref, wa_ref, wo_ref, o_ref):
    ret_out = _dot(og_ref[...], wr_ref[...])
    att_out = _dot(att_ref[...], wa_ref[...])
    merged = gr_ref[...].astype(f32) * ret_out + ga_ref[...].astype(f32) * att_out
    o_ref[...] = x_ref[...] + _dot(merged.astype(bf16), wo_ref[...])


def _oproj(og, att, p, x, wr, wa, wo, tm):
    m, d = x.shape
    const = lambda shape: pl.BlockSpec(shape, lambda i: (0, 0), pipeline_mode=pl.Buffered(1))
    return pl.pallas_call(
        _oproj_kernel,
        grid=(m // tm,),
        in_specs=[
            pl.BlockSpec((tm, RET_V_W), lambda i: (i, 0)),
            pl.BlockSpec((tm, ATT_Q_W), lambda i: (i, 0)),
            pl.BlockSpec((tm, d), lambda i: (i, 0)),
            pl.BlockSpec((tm, d), lambda i: (i, 1)),
            pl.BlockSpec((tm, d), lambda i: (i, 0)),
            const(wr.shape), const(wa.shape), const(wo.shape),
        ],
        out_specs=pl.BlockSpec((tm, d), lambda i: (i, 0)),
        out_shape=jax.ShapeDtypeStruct((m, d), f32),
        compiler_params=_params(("arbitrary",)),
    )(og, att, p, p, x, wr, wa, wo)


def _dec_ret_kernel(lg_ref, q_ref, k_ref, v_ref, sg_ref, gn_ref, st_ref, og_ref, nst_ref):
    for h in range(RET_HEADS):
        lg = lg_ref[h]
        gamma = jnp.exp(lg)
        q = q_ref[0, :, h * RET_QK_DIM:(h + 1) * RET_QK_DIM]
        k = k_ref[0, :, h * RET_QK_DIM:(h + 1) * RET_QK_DIM]
        v = v_ref[0, :, h * RET_V_DIM:(h + 1) * RET_V_DIM]
        sg = sg_ref[0, :, h * RET_V_DIM:(h + 1) * RET_V_DIM]
        state = st_ref[0, 0, h]
        qk = jnp.sum(q * k, axis=1, keepdims=True)
        q8 = jnp.broadcast_to((q * gamma).astype(bf16), (8, RET_QK_DIM))
        cross = _dot(q8, state.astype(bf16))[0:1]
        k_col = jnp.transpose(jnp.broadcast_to(k, (8, RET_QK_DIM)))[:, 0:1]
        nst_ref[0, h] = gamma * state + k_col * v
        o = _rms(qk * v + cross, gn_ref[...])
        og_ref[0, :, h * RET_V_DIM:(h + 1) * RET_V_DIM] = o * sg


def _decode_retention(q, k, v, sg, gn, state, log_g):
    db = q.shape[0]
    row = lambda w: pl.BlockSpec((1, 1, w), lambda b, lg: (b, 0, 0))
    grid_spec = pltpu.PrefetchScalarGridSpec(
        num_scalar_prefetch=1,
        grid=(db,),
        in_specs=[row(RET_QK_W), row(RET_QK_W), row(RET_V_W), row(RET_V_W),
                  pl.BlockSpec((1, RET_V_DIM), lambda b, lg: (0, 0)),
                  pl.BlockSpec((1, 1, RET_HEADS, RET_QK_DIM, RET_V_DIM), lambda b, lg: (0, b, 0, 0, 0))],
        out_specs=[row(RET_V_W),
                   pl.BlockSpec((1, RET_HEADS, RET_QK_DIM, RET_V_DIM), lambda b, lg: (b, 0, 0, 0))],
    )
    return pl.pallas_call(
        _dec_ret_kernel,
        grid_spec=grid_spec,
        out_shape=[jax.ShapeDtypeStruct((db, 1, RET_V_W), f32),
                   jax.ShapeDtypeStruct((db, RET_HEADS, RET_QK_DIM, RET_V_DIM), f32)],
        compiler_params=_params(("arbitrary",)),
    )(log_g, q, k, v, sg, gn, state)


def _dec_score_kernel(pt_ref, qi_ref, w_ref, pool_ref, s_ref, buf_ref, sem_ref, *, n_pages):
    b = pl.program_id(0)
    slot = b % 2

    def page_copy(seq, p, slot_):
        dst = buf_ref.at[slot_, :, pl.ds(pl.multiple_of(p * PAGE_SIZE, PAGE_SIZE), PAGE_SIZE)]
        return pltpu.make_async_copy(pool_ref.at[pt_ref[seq, p]], dst, sem_ref.at[slot_])

    def start_all(seq, slot_):
        def body(p, c):
            page_copy(seq, p, slot_).start()
            return c
        lax.fori_loop(0, n_pages, body, 0)

    @pl.when(b == 0)
    def _():
        start_all(0, 0)

    @pl.when(b + 1 < pl.num_programs(0))
    def _():
        start_all(b + 1, 1 - slot)

    def wait_body(p, c):
        page_copy(b, p, slot).wait()
        return c
    lax.fori_loop(0, n_pages, wait_body, 0)

    s = jnp.maximum(_dot(qi_ref[0], buf_ref[slot].astype(bf16)), 0.0)
    row = jnp.sum(s * w_ref[0], axis=0, keepdims=True)
    for p in range(n_pages):
        s_ref[0, p:p + 1, :] = row[:, p * PAGE_SIZE:(p + 1) * PAGE_SIZE]


def _decode_scores(page_table, qi, w, pool_ikt):
    db, n_pages = page_table.shape
    kern = functools.partial(_dec_score_kernel, n_pages=n_pages)
    grid_spec = pltpu.PrefetchScalarGridSpec(
        num_scalar_prefetch=1,
        grid=(db,),
        in_specs=[
            pl.BlockSpec((1, IDX_HEADS, IDX_DIM), lambda b, pt: (b, 0, 0)),
            pl.BlockSpec((1, IDX_HEADS, 1), lambda b, pt: (b, 0, 0)),
            pl.BlockSpec(memory_space=pl.ANY),
        ],
        out_specs=pl.BlockSpec((1, n_pages, PAGE_SIZE), lambda b, pt: (b, 0, 0)),
        scratch_shapes=[pltpu.VMEM((2, IDX_DIM, n_pages * PAGE_SIZE), f32), pltpu.SemaphoreType.DMA((2,))],
    )
    return pl.pallas_call(
        kern,
        grid_spec=grid_spec,
        out_shape=jax.ShapeDtypeStruct((db, n_pages, PAGE_SIZE), f32),
        compiler_params=_params(("arbitrary",)),
    )(page_table, qi, w, pool_ikt)


def _dec_select_kernel(s_ref, qi_ref, w_ref, ikn_ref, idx_ref, cnt_ref, nsel_ref, rank_ref, sel_ref,
                       *, n_keep, kmax):
    db, n_pages, _ = s_ref.shape
    ikn = ikn_ref[...].astype(bf16).astype(f32)
    dots = jnp.sum(qi_ref[...].astype(f32) * ikn, axis=2, keepdims=True)
    s_new = jnp.sum(jnp.maximum(dots, 0.0) * w_ref[...], axis=1, keepdims=True)
    keys = _sortable_key(s_ref[...])
    key_new = _sortable_key(s_new)

    def count_ge(t):
        c = jnp.sum(jnp.where(keys >= t, 1.0, 0.0), axis=1, keepdims=True)
        return jnp.sum(c, axis=2, keepdims=True) + jnp.where(key_new >= t, 1.0, 0.0)

    thr = _kth_largest_key(count_ge, n_keep, (db, 1, 1))
    sel_ref[...] = jnp.where(keys >= thr, 1.0, 0.0)
    nsel_ref[...] = jnp.broadcast_to(jnp.where(key_new >= thr, 1.0, 0.0), (db, 1, LANES))

    def count3(mask):
        c = jnp.sum(jnp.where(mask, 1.0, 0.0), axis=1, keepdims=True)
        return jnp.sum(c, axis=2, keepdims=True)

    n_ge = count_ge(thr)

    @pl.when(jnp.max(n_ge) > n_keep)
    def _():
        pos = (lax.broadcasted_iota(jnp.int32, keys.shape, 1) * PAGE_SIZE
               + lax.broadcasted_iota(jnp.int32, keys.shape, 2))
        quota = n_keep - (count3(keys > thr) + jnp.where(key_new > thr, 1.0, 0.0))
        eq = keys == thr
        ties_past = count3(eq)
        kth = ties_past - jnp.minimum(quota, ties_past) + 1.0
        cut = _kth_largest_pos(lambda c: count3(eq & (pos >= c)), kth, (db, 1, 1),
                               (n_pages * PAGE_SIZE).bit_length())
        sel_ref[...] = jnp.where((keys > thr) | (eq & (pos <= cut)), 1.0, 0.0)
        new_sel = (key_new > thr) | ((key_new == thr) & (quota > ties_past))
        nsel_ref[...] = jnp.broadcast_to(jnp.where(new_sel, 1.0, 0.0), (db, 1, LANES))

    pp = lax.broadcasted_iota(jnp.int32, (n_pages, n_pages), 0)
    pq = lax.broadcasted_iota(jnp.int32, (n_pages, n_pages), 1)
    earlier_page = jnp.where(pq < pp, 1.0, 0.0).astype(bf16)
    oo = lax.broadcasted_iota(jnp.int32, (PAGE_SIZE, PAGE_SIZE), 0)
    oq = lax.broadcasted_iota(jnp.int32, (PAGE_SIZE, PAGE_SIZE), 1)
    earlier_off = jnp.where(oo < oq, 1.0, 0.0).astype(bf16)
    r_iota = lax.broadcasted_iota(jnp.int32, (kmax, PAGE_SIZE), 0).astype(f32)
    lane = lax.broadcasted_iota(jnp.int32, (kmax, PAGE_SIZE), 1).astype(f32)
    for b in range(db):
        sb = sel_ref[b]
        per_page = jnp.sum(sb, axis=1, keepdims=True)
        before = _dot(earlier_page, jnp.broadcast_to(per_page, (n_pages, PAGE_SIZE)).astype(bf16))
        within = _dot(sb.astype(bf16), earlier_off)
        rank_ref[b] = jnp.where(sb > 0.0, before + within, -1.0)

        def body(p, acc, b=b):
            row = rank_ref[b, pl.ds(p, 1), :]
            return jnp.where(row == r_iota, lane + jnp.asarray(p * PAGE_SIZE, f32), acc)

        acc = lax.fori_loop(0, n_pages, body, jnp.zeros((kmax, PAGE_SIZE), f32))
        idx_ref[b] = jnp.sum(acc, axis=1, keepdims=True).astype(jnp.int32)
        total = jnp.sum(per_page, axis=0, keepdims=True)
        cnt_ref[b] = jnp.broadcast_to(total, (1, LANES)).astype(jnp.int32)


def _decode_select(scores, qi, w, ik_new, n_keep):
    db, n_pages, _ = scores.shape
    kmax = -(-n_keep // 8) * 8
    kern = functools.partial(_dec_select_kernel, n_keep=n_keep, kmax=kmax)
    full = lambda shape: pl.BlockSpec(shape, lambda i: (0,) * len(shape))
    return pl.pallas_call(
        kern,
        grid=(1,),
        in_specs=[full(scores.shape), full(qi.shape), full(w.shape), full(ik_new.shape)],
        out_specs=[full((db, kmax, 1)), full((db, 1, LANES)), full((db, 1, LANES))],
        out_shape=[jax.ShapeDtypeStruct((db, kmax, 1), jnp.int32),
                   jax.ShapeDtypeStruct((db, 1, LANES), jnp.int32),
                   jax.ShapeDtypeStruct((db, 1, LANES), f32)],
        scratch_shapes=[pltpu.VMEM((db, n_pages, PAGE_SIZE), f32), pltpu.VMEM((db, n_pages, PAGE_SIZE), f32)],
        compiler_params=_params(("arbitrary",)),
    )(scores, qi, w, ik_new)


def _dec_attn_kernel(idx_ref, pt_ref, cnt_ref, q_ref, kn_ref, vn_ref, nsel_ref, kpool_ref, vpool_ref, o_ref,
                     kbuf_ref, vbuf_ref, sem_ref, *, kmax):
    b = pl.program_id(0)
    slot = b % 2
    page_shift = PAGE_SIZE.bit_length() - 1

    def row_copies(seq, r, slot_):
        pos = idx_ref[seq, r]
        phys = pt_ref[seq, lax.shift_right_logical(pos, page_shift)]
        src0 = pl.multiple_of((pos & (PAGE_SIZE - 1)) * ATT_KV_HEADS, ATT_KV_HEADS)
        dst0 = pl.multiple_of(r * ATT_KV_HEADS, ATT_KV_HEADS)
        return (pltpu.make_async_copy(kpool_ref.at[phys, pl.ds(src0, ATT_KV_HEADS), :],
                                      kbuf_ref.at[slot_, pl.ds(dst0, ATT_KV_HEADS), :], sem_ref.at[0, slot_]),
                pltpu.make_async_copy(vpool_ref.at[phys, pl.ds(src0, ATT_KV_HEADS), :],
                                      vbuf_ref.at[slot_, pl.ds(dst0, ATT_KV_HEADS), :], sem_ref.at[1, slot_]))

    def start_all(seq, slot_):
        def body(r, c):
            ck, cv = row_copies(seq, r, slot_)
            ck.start()
            cv.start()
            return c
        lax.fori_loop(0, kmax, body, 0, unroll=8)

    @pl.when(b == 0)
    def _():
        start_all(0, 0)

    @pl.when(b + 1 < pl.num_programs(0))
    def _():
        start_all(b + 1, 1 - slot)

    def wait_body(r, c):
        ck, cv = row_copies(b, r, slot)
        ck.wait()
        cv.wait()
        return c
    lax.fori_loop(0, kmax, wait_body, 0, unroll=8)

    q = q_ref[0]
    s = _dot_nt(q, kbuf_ref[slot].astype(bf16))
    col = lax.broadcasted_iota(jnp.int32, s.shape, 1)
    head = lax.broadcasted_iota(jnp.int32, s.shape, 0)
    ok = (col % ATT_KV_HEADS == head // GROUP) & (col // ATT_KV_HEADS < cnt_ref[b])
    s = jnp.where(ok, s, -jnp.inf)
    s_n = jnp.sum(q.astype(f32) * kn_ref[0].astype(f32), axis=1, keepdims=True)
    s_n = jnp.where(nsel_ref[0][:, 0:1] > 0.0, s_n, -jnp.inf)
    m = jnp.maximum(jnp.max(s, axis=1, keepdims=True), s_n)
    p = jnp.exp2(s - m)
    p_n = jnp.exp2(s_n - m)
    l = jnp.sum(p, axis=1, keepdims=True) + p_n
    pv = _dot(p.astype(bf16), vbuf_ref[slot].astype(bf16))
    o_ref[0] = (pv + p_n.astype(bf16).astype(f32) * vn_ref[0].astype(f32)) / l


def _decode_attention(idx, page_table, cnt, q, k_new, v_new, nsel, pool_k, pool_v):
    db, kmax = idx.shape
    kern = functools.partial(_dec_attn_kernel, kmax=kmax)
    seqb = lambda shape: pl.BlockSpec(shape, lambda b, *_: (b, 0, 0))
    grid_spec = pltpu.PrefetchScalarGridSpec(
        num_scalar_prefetch=3,
        grid=(db,),
        in_specs=[
            seqb((1, ATT_HEADS, HEAD_DIM)),
            seqb((1, ATT_HEADS, HEAD_DIM)),
            seqb((1, ATT_HEADS, HEAD_DIM)),
            seqb((1, 1, LANES)),
            pl.BlockSpec(memory_space=pl.ANY),
            pl.BlockSpec(memory_space=pl.ANY),
        ],
        out_specs=seqb((1, ATT_HEADS, HEAD_DIM)),
        scratch_shapes=[pltpu.VMEM((2, kmax * ATT_KV_HEADS, HEAD_DIM), f32),
                        pltpu.VMEM((2, kmax * ATT_KV_HEADS, HEAD_DIM), f32),
                        pltpu.SemaphoreType.DMA((2, 2))],
    )
    return pl.pallas_call(
        kern,
        grid_spec=grid_spec,
        out_shape=jax.ShapeDtypeStruct((db, ATT_HEADS, HEAD_DIM), f32),
        compiler_params=_params(("arbitrary",)),
    )(idx, page_table, cnt, q, k_new, v_new, nsel, pool_k, pool_v)


def _rope_tables(pos):
    half = RET_QK_DIM // 2
    inv = ROPE_BASE ** (-jnp.arange(half, dtype=f32) / half)
    ang = pos.astype(f32)[:, None] * inv[None, :]
    cos, sin = jnp.cos(ang), jnp.sin(ang)
    return jnp.concatenate([cos, cos], axis=1), jnp.concatenate([-sin, sin], axis=1)


def _pick(n, pref):
    for t in pref:
        if n % t == 0:
            return t
    return n


def kernel(x_prompt, x_sample, state_ret, cache_k, cache_v, cache_idx_k, page_table, ffn1_norm, ffn1_w1, ffn1_w2, mix_norm, w_in, q_norm, k_norm, ret_norm, w_ret_out, w_att_out, w_o, ffn2_norm, ffn2_w1, ffn2_w2):
    batch, seq, d = x_prompt.shape
    db, ts, _ = x_sample.shape
    depth = w_in.shape[0]
    assert depth == 1 and ts == 1 and d % (2 * TN) == 0
    n_pages = page_table.shape[1]
    past = n_pages * PAGE_SIZE
    n_phys = cache_k.shape[1]
    base = 2 * (d // TN)
    m = batch * seq
    ms = BF16_ROWS

    log_g = jnp.log1p(-jnp.exp2(-5.0 - jnp.arange(RET_HEADS, dtype=f32)))
    w1a, w2a = ffn1_w1[0].astype(bf16), ffn1_w2[0].astype(bf16)
    w1b, w2b = ffn2_w1[0].astype(bf16), ffn2_w2[0].astype(bf16)
    w_pad = jnp.swapaxes(w_in[0], 0, 1).astype(bf16)
    wr, wa, wo = w_ret_out[0].astype(bf16), w_att_out[0].astype(bf16), w_o[0].astype(bf16)
    g1, g2, g3 = ffn1_norm, mix_norm, ffn2_norm
    qn, kn, gn = q_norm, k_norm, ret_norm

    tm = _pick(m, (512, 256, 128))
    tf = _pick(ffn1_w2.shape[1], (512, 256, 128))
    tmp = _pick(seq, (1024, 512, 256, 128))
    chunk = _pick(seq, (256, 128))
    tq = _pick(seq, (256, 128))
    tmo = _pick(m, (256, 128))

    xp = x_prompt.reshape(m, d)
    x1 = _ffn(xp, g1, w1a, w2a, tm, tf)
    cos_p, sin_p = _rope_tables(jnp.arange(seq))
    p, k32, v32, ikw = _proj(x1, g2, w_pad, cos_p, sin_p, qn, kn, tmp)
    og, st_p = _retention(p, log_g, gn, batch, seq, base, chunk)
    att = _prompt_attention(p, ikw, batch, seq, base, tq, min(TOPK_MAX, seq // 4))
    x2 = _oproj(og, att, p, x1, wr, wa, wo, tmo)
    yp = _ffn(x2, g3, w1b, w2b, tm, tf).reshape(batch, seq, d)

    xs = jnp.zeros((ms, d), f32).at[:db].set(x_sample.reshape(db, d))
    s1 = _ffn(xs, g1, w1a, w2a, ms, tf)
    cos_s, sin_s = _rope_tables(jnp.full((ms,), past, jnp.int32))
    ps, k32s, v32s, ikws = _proj(s1, g2, w_pad, cos_s, sin_s, qn, kn, ms)
    seg = lambda t0, t1: ps[:db, (base + t0) * TN:(base + t1) * TN]
    row3 = lambda a: a.astype(f32).reshape(db, 1, a.shape[-1])
    og_s, st_s = _decode_retention(row3(seg(T_RQ, T_RK)), row3(seg(T_RK, T_RV)), row3(seg(T_RV, T_RG)),
                                   row3(seg(T_RG, T_AQ)), gn, state_ret, log_g)
    qi = seg(T_IQ, T_IKW).reshape(db, IDX_HEADS, IDX_DIM)
    wi = (ikws[:db, IDX_DIM:IDX_DIM + IDX_HEADS] * IDX_W_SCALE).reshape(db, IDX_HEADS, 1)
    ik_new = ikws[:db, :IDX_DIM].reshape(db, 1, IDX_DIM)
    pool_ikt = jnp.swapaxes(cache_idx_k.reshape(n_phys, PAGE_SIZE, IDX_DIM), 1, 2)
    scores = _decode_scores(page_table, qi, wi, pool_ikt)
    idx, cnt, nsel = _decode_select(scores, qi, wi, ik_new, min(TOPK_MAX, (past + ts) // 4))
    aq_s = seg(T_AQ, T_AK).reshape(db, ATT_HEADS, HEAD_DIM)
    new_kv = lambda a: a[:db * ATT_KV_HEADS].reshape(db, ATT_KV_HEADS, HEAD_DIM)
    expand = lambda a: jnp.repeat(new_kv(a), GROUP, axis=1).astype(bf16)
    pool_rows = lambda c: c.reshape(n_phys, PAGE_SIZE * ATT_KV_HEADS, HEAD_DIM)
    att_s = _decode_attention(idx.reshape(db, -1), page_table, cnt[:, 0, 0], aq_s, expand(k32s), expand(v32s),
                              nsel, pool_rows(cache_k), pool_rows(cache_v))
    pad_rows = lambda a: jnp.pad(a.reshape(db, -1).astype(bf16), ((0, ms - db), (0, 0)))
    s2 = _oproj(pad_rows(og_s), pad_rows(att_s), ps, s1, wr, wa, wo, ms)
    ys = _ffn(s2, g3, w1b, w2b, ms, tf)[:db].reshape(db, ts, d)

    return (yp, ys,
            st_p[None],
            k32.reshape(1, batch, seq, ATT_KV_HEADS, HEAD_DIM),
            v32.reshape(1, batch, seq, ATT_KV_HEADS, HEAD_DIM),
            ikw[:, :IDX_DIM].reshape(1, batch, seq, IDX_DIM),
            st_s[None],
            new_kv(k32s).reshape(1, db, ts, ATT_KV_HEADS, HEAD_DIM),
            new_kv(v32s).reshape(1, db, ts, ATT_KV_HEADS, HEAD_DIM),
            ikws[:db, :IDX_DIM].reshape(1, db, ts, IDX_DIM))
```

```python
import functools

import jax
import jax.numpy as jnp
from jax import lax
from jax.experimental import pallas as pl
from jax.experimental.pallas import tpu as pltpu

RET_HEADS = 8
RET_QK_DIM = 128
RET_V_DIM = 256
ATT_HEADS = 16
ATT_KV_HEADS = 4
HEAD_DIM = 128
GROUP = ATT_HEADS // ATT_KV_HEADS
IDX_HEADS = 16
IDX_DIM = 64
IDX_W_SCALE = (IDX_HEADS ** -0.5) * (IDX_DIM ** -0.5)
TOPK_MAX = 256
PAGE_SIZE = 128
ROPE_BASE = 10000.0
EPS = 1e-6

RET_QK_W = RET_HEADS * RET_QK_DIM
RET_V_W = RET_HEADS * RET_V_DIM
ATT_Q_W = ATT_HEADS * HEAD_DIM
ATT_KV_W = ATT_KV_HEADS * HEAD_DIM
IDX_Q_W = IDX_HEADS * IDX_DIM
MAIN_W = 2 * RET_QK_W + 2 * RET_V_W + ATT_Q_W + 2 * ATT_KV_W + IDX_Q_W

LANES = 128
TN = 512
T_RQ, T_RK, T_RV, T_RG, T_AQ, T_AK, T_AV, T_IQ, T_IKW, T_END = 0, 2, 4, 8, 12, 16, 17, 18, 20, 21
VMEM_LIMIT = 56 * 1024 * 1024
INT_MIN = -2 ** 31
I16_MIN, I16_MAX = -2 ** 15, 2 ** 15 - 1
HALF_RANGE = 2 ** 16
SUBLANES = 8
BF16_ROWS = 16
NEG_BIG = -1e30
LOG2E = 1.4426950408889634
VT_PAD = 16

bf16 = jnp.bfloat16
f32 = jnp.float32


def _params(sem, vmem=VMEM_LIMIT):
    return pltpu.CompilerParams(dimension_semantics=sem, vmem_limit_bytes=vmem)


def _sigmoid(x):
    return 1.0 / (1.0 + jnp.exp(-x))


def _dot(a, b):
    return jnp.dot(a, b, preferred_element_type=f32)


def _dot_nt(a, b):
    return lax.dot_general(a, b, (((1,), (1,)), ((), ())), preferred_element_type=f32)


def _rms(x, gain):
    ms = jnp.mean(x * x, axis=-1, keepdims=True)
    return x * lax.rsqrt(ms + EPS) * gain


def _ffn_kernel(x_ref, g_ref, w1g_ref, w1u_ref, w2_ref, o_ref, h_ref):
    j = pl.program_id(1)

    @pl.when(j == 0)
    def _():
        h_ref[...] = _rms(x_ref[...], g_ref[...]).astype(bf16)
        o_ref[...] = jnp.zeros_like(o_ref)

    h = h_ref[...]
    gate = _dot(h, w1g_ref[...])
    up = _dot(h, w1u_ref[...])
    act = (gate * _sigmoid(gate) * up).astype(bf16)
    o_ref[...] += _dot(act, w2_ref[...])

    @pl.when(j == pl.num_programs(1) - 1)
    def _():
        o_ref[...] = x_ref[...] + 0.5 * o_ref[...]


def _ffn(x, gain, w1, w2, tm, tf):
    m, d = x.shape
    dff = w2.shape[0]
    nf = dff // tf
    return pl.pallas_call(
        _ffn_kernel,
        grid=(m // tm, nf),
        in_specs=[
            pl.BlockSpec((tm, d), lambda i, j: (i, 0)),
            pl.BlockSpec((1, d), lambda i, j: (0, 0)),
            pl.BlockSpec((d, tf), lambda i, j: (0, j)),
            pl.BlockSpec((d, tf), lambda i, j: (0, j + nf)),
            pl.BlockSpec((tf, d), lambda i, j: (j, 0)),
        ],
        out_specs=pl.BlockSpec((tm, d), lambda i, j: (i, 0)),
        out_shape=jax.ShapeDtypeStruct((m, d), f32),
        scratch_shapes=[pltpu.VMEM((tm, d), bf16)],
        compiler_params=_params(("arbitrary", "arbitrary")),
    )(x, gain, w1, w1, w2)


def _proj_kernel(x_ref, g_ref, w_ref, cos_ref, sin_ref, qn_ref, kn_ref,
                 p_ref, k32_ref, v32_ref, ikw_ref, h_ref, *, base):
    j = pl.program_id(1)

    @pl.when(j == 0)
    def _():
        h_ref[...] = _rms(x_ref[...], g_ref[...]).astype(bf16)

    w = w_ref[...]
    heads = [slice(c * LANES, (c + 1) * LANES) for c in range(TN // LANES)]
    tm = h_ref.shape[0]
    rc = min(tm, 256)

    def segment(cond, epilogue):
        @pl.when(cond)
        def _():
            for r0 in range(0, tm, rc):
                rows = slice(r0, r0 + rc)
                epilogue(_dot_nt(h_ref[rows, :], w), rows)

    def kv_rows(rows, g):
        return pl.ds(rows.start * ATT_KV_HEADS + g, rows.stop - rows.start, stride=ATT_KV_HEADS)

    def gates(res, rows):
        p_ref[rows, :] = _sigmoid(res).astype(bf16)

    def rotary(res, rows):
        cos, sin = cos_ref[rows, :], sin_ref[rows, :]
        scale = jnp.where(j >= base + T_RK, RET_QK_DIM ** -0.5, 1.0).astype(f32)
        for sl in heads:
            x = res[:, sl]
            p_ref[rows, sl] = ((x * cos + pltpu.roll(x, LANES // 2, 1) * sin) * scale).astype(bf16)

    def plain(res, rows):
        p_ref[rows, :] = res.astype(bf16)

    def silu(res, rows):
        p_ref[rows, :] = (res * _sigmoid(res)).astype(bf16)

    def q_norm(res, rows):
        for sl in heads:
            p_ref[rows, sl] = (_rms(res[:, sl], qn_ref[...]) * (HEAD_DIM ** -0.5 * LOG2E)).astype(bf16)

    def k_norm(res, rows):
        for g, sl in enumerate(heads):
            y = _rms(res[:, sl], kn_ref[...])
            k32_ref[kv_rows(rows, g), :] = y
            p_ref[rows, sl] = y.astype(bf16)

    def value(res, rows):
        for g, sl in enumerate(heads):
            v32_ref[kv_rows(rows, g), :] = res[:, sl]
        p_ref[rows, :] = res.astype(bf16)

    def idx_kw(res, rows):
        ikw_ref[rows, :] = res[:, :LANES]
        p_ref[rows, :] = res.astype(bf16)

    segment(j < base, gates)
    segment((j >= base + T_RQ) & (j < base + T_RV), rotary)
    segment(((j >= base + T_RV) & (j < base + T_RG)) | ((j >= base + T_IQ) & (j < base + T_IKW)), plain)
    segment((j >= base + T_RG) & (j < base + T_AQ), silu)
    segment((j >= base + T_AQ) & (j < base + T_AK), q_norm)
    segment(j == base + T_AK, k_norm)
    segment(j == base + T_AV, value)
    segment(j == base + T_IKW, idx_kw)


def _proj(x, gain, w_t, cos2, sin2, qn, kn, tm):
    m, d = x.shape
    gate_row0 = MAIN_W + IDX_DIM + IDX_HEADS
    base = (w_t.shape[0] - gate_row0) // TN
    nt = base + T_END
    tab_blocks = cos2.shape[0] // tm
    kern = functools.partial(_proj_kernel, base=base)
    al = BF16_ROWS
    assert gate_row0 % al == 0
    w_row = lambda j: al * jnp.where(j < base, gate_row0 // al + j * (TN // al), (j - base) * (TN // al))
    return pl.pallas_call(
        kern,
        grid=(m // tm, nt),
        in_specs=[
            pl.BlockSpec((tm, d), lambda i, j: (i, 0)),
            pl.BlockSpec((1, d), lambda i, j: (0, 0)),
            pl.BlockSpec((pl.Element(TN), pl.Element(d)), lambda i, j: (w_row(j), 0)),
            pl.BlockSpec((tm, LANES), lambda i, j: (i % tab_blocks, 0)),
            pl.BlockSpec((tm, LANES), lambda i, j: (i % tab_blocks, 0)),
            pl.BlockSpec((1, LANES), lambda i, j: (0, 0)),
            pl.BlockSpec((1, LANES), lambda i, j: (0, 0)),
        ],
        out_specs=[
            pl.BlockSpec((tm, TN), lambda i, j: (i, j)),
            pl.BlockSpec((tm * ATT_KV_HEADS, HEAD_DIM), lambda i, j: (i, 0)),
            pl.BlockSpec((tm * ATT_KV_HEADS, HEAD_DIM), lambda i, j: (i, 0)),
            pl.BlockSpec((tm, LANES), lambda i, j: (i, 0)),
        ],
        out_shape=[
            jax.ShapeDtypeStruct((m, nt * TN), bf16),
            jax.ShapeDtypeStruct((m * ATT_KV_HEADS, HEAD_DIM), f32),
            jax.ShapeDtypeStruct((m * ATT_KV_HEADS, HEAD_DIM), f32),
            jax.ShapeDtypeStruct((m, LANES), f32),
        ],
        scratch_shapes=[pltpu.VMEM((tm, d), bf16)],
        compiler_params=_params(("arbitrary", "arbitrary")),
    )(x, gain, w_t, cos2, sin2, qn, kn)


def _ret_kernel(lg_ref, q_ref, k_ref, v_ref, sg_ref, gn_ref, og_ref, st_ref, state_ref, *, chunk):
    c = pl.program_id(1)

    @pl.when(c == 0)
    def _():
        state_ref[...] = jnp.zeros_like(state_ref)

    ii = lax.broadcasted_iota(jnp.int32, (chunk, chunk), 0)
    jj = lax.broadcasted_iota(jnp.int32, (chunk, chunk), 1)
    rel = (ii - jj).astype(f32)
    i1 = lax.broadcasted_iota(jnp.int32, (chunk, 1), 0).astype(f32)
    for h in range(RET_HEADS):
        lg = lg_ref[h]
        qk_cols = slice(h * RET_QK_DIM, (h + 1) * RET_QK_DIM)
        v_cols = slice(h * RET_V_DIM, (h + 1) * RET_V_DIM)
        q, k, v = q_ref[:, qk_cols], k_ref[:, qk_cols], v_ref[:, v_cols]
        dmat = jnp.where(rel >= 0, jnp.exp(lg * jnp.maximum(rel, 0.0)), 0.0)
        scores = _dot_nt(q, k) * dmat
        inner = _dot(scores.astype(bf16), v)
        q_dec = jnp.exp(lg * (i1 + 1.0))
        k_dec = jnp.exp(lg * (chunk - 1.0 - i1))
        state = state_ref[h]
        cross = _dot((q.astype(f32) * q_dec).astype(bf16), state.astype(bf16))
        kd = (k.astype(f32) * k_dec).astype(bf16)
        new_state = jnp.exp(lg * chunk) * state + lax.dot_general(
            kd, v, (((0,), (0,)), ((), ())), preferred_element_type=f32)
        state_ref[h] = new_state
        o = _rms(inner + cross, gn_ref[...])
        og_ref[:, v_cols] = (o * sg_ref[:, v_cols].astype(f32)).astype(bf16)

    @pl.when(c == pl.num_programs(1) - 1)
    def _():
        st_ref[0] = state_ref[...]


def _retention(p, log_g, gn, batch, seq, base, chunk):
    nc = seq // chunk
    qb = (base + T_RQ) * TN // RET_QK_W
    kb = (base + T_RK) * TN // RET_QK_W
    vb = (base + T_RV) * TN // RET_V_W
    gb = (base + T_RG) * TN // RET_V_W
    kern = functools.partial(_ret_kernel, chunk=chunk)
    grid_spec = pltpu.PrefetchScalarGridSpec(
        num_scalar_prefetch=1,
        grid=(batch, nc),
        in_specs=[
            pl.BlockSpec((chunk, RET_QK_W), lambda b, c, lg: (b * nc + c, qb)),
            pl.BlockSpec((chunk, RET_QK_W), lambda b, c, lg: (b * nc + c, kb)),
            pl.BlockSpec((chunk, RET_V_W), lambda b, c, lg: (b * nc + c, vb)),
            pl.BlockSpec((chunk, RET_V_W), lambda b, c, lg: (b * nc + c, gb)),
            pl.BlockSpec((1, RET_V_DIM), lambda b, c, lg: (0, 0)),
        ],
        out_specs=[
            pl.BlockSpec((chunk, RET_V_W), lambda b, c, lg: (b * nc + c, 0)),
            pl.BlockSpec((1, RET_HEADS, RET_QK_DIM, RET_V_DIM), lambda b, c, lg: (b, 0, 0, 0)),
        ],
        scratch_shapes=[pltpu.VMEM((RET_HEADS, RET_QK_DIM, RET_V_DIM), f32)],
    )
    return pl.pallas_call(
        kern,
        grid_spec=grid_spec,
        out_shape=[
            jax.ShapeDtypeStruct((batch * seq, RET_V_W), bf16),
            jax.ShapeDtypeStruct((batch, RET_HEADS, RET_QK_DIM, RET_V_DIM), f32),
        ],
        compiler_params=_params(("arbitrary", "arbitrary")),
    )(log_g, p, p, p, p, gn)


def _sortable_key(score):
    bits = lax.bitcast_convert_type(score, jnp.int32)
    return jnp.where(bits >= 0, bits, bits ^ jnp.int32(0x7FFFFFFF))


def _kth_largest_key(count_ge, n_keep, shape):
    def body(it, t_u):
        bit = jnp.left_shift(jnp.int32(1), 31 - it)
        cand_u = t_u | bit
        cnt = count_ge(cand_u ^ jnp.int32(INT_MIN))
        return jnp.where(cnt >= n_keep, cand_u, t_u)
    t_u = lax.fori_loop(0, 32, body, jnp.zeros(shape, jnp.int32))
    return t_u ^ jnp.int32(INT_MIN)


def _kth_largest_pos(count_ge, kth, shape, bits):
    def body(it, t):
        cand = t | jnp.left_shift(jnp.int32(1), bits - 1 - it)
        return jnp.where(count_ge(cand) >= kth, cand, t)
    return lax.fori_loop(0, bits, body, jnp.zeros(shape, jnp.int32))


def _kth_largest_16(count_ge, n_keep, shape):
    def body(it, t_u):
        cand_u = t_u | jnp.left_shift(jnp.int32(1), 15 - it)
        cnt = count_ge(cand_u + I16_MIN)
        return jnp.where(cnt >= n_keep, cand_u, t_u)
    t_u = lax.fori_loop(0, 16, body, jnp.zeros(shape, jnp.int32))
    return t_u + I16_MIN


def _attn_kernel(aq_ref, k_ref, v_ref, iq_ref, ikw_k_ref, ikw_q_ref, o_ref,
                 key_ref, hi_ref, lo_ref, iklo_ref, ikhi_ref, vt_ref, m_ref, acc_ref, p_ref, alpha_ref,
                 *, tq, ck, seq, n_keep):
    qb = pl.program_id(1)

    @pl.when(qb == 0)
    def _():
        ikw = ikw_k_ref[...]
        lane = lax.broadcasted_iota(jnp.int32, (seq, LANES), 1)
        iklo_ref[...] = jnp.where(lane < IDX_DIM, ikw, 0.0).astype(bf16)
        ikhi_ref[...] = jnp.where(lane >= IDX_DIM, pltpu.roll(ikw, IDX_DIM, 1), 0.0).astype(bf16)

        for c0 in range(0, seq, ck):
            v_t = jnp.transpose(v_ref[c0:c0 + ck, :].astype(f32)).astype(bf16)
            for g in range(ATT_KV_HEADS):
                r0 = g * (HEAD_DIM + VT_PAD)
                vt_ref[r0:r0 + HEAD_DIM, c0:c0 + ck] = v_t[g * HEAD_DIM:(g + 1) * HEAD_DIM]
                vt_ref[r0 + HEAD_DIM:r0 + HEAD_DIM + VT_PAD, c0:c0 + ck] = jnp.ones((VT_PAD, ck), bf16)

    n_chunks = (qb * tq + tq - 1) // ck + 1
    chunk = lambda kc: pl.ds(kc * ck if isinstance(kc, int) else pl.multiple_of(kc * ck, ck), ck)
    stack = 4
    pairs = IDX_HEADS // 2

    iw_t = jnp.transpose(ikw_q_ref[...]) * IDX_W_SCALE
    q_stacks = [jnp.concatenate([iq_ref[:, hp * LANES:(hp + 1) * LANES] for hp in range(h0, h0 + stack)], axis=0)
                for h0 in range(0, pairs, stack)]
    k_off = lax.broadcasted_iota(jnp.int32, (ck, tq), 0)
    q_pos = qb * tq + lax.broadcasted_iota(jnp.int32, (ck, tq), 1)

    def score_body(kc, c):
        scores = jnp.zeros((ck, tq), f32)
        for si, qs in enumerate(q_stacks):
            for sub, ik_ref in enumerate((iklo_ref, ikhi_ref)):
                s = jnp.maximum(_dot_nt(ik_ref[chunk(kc), :], qs), 0.0)
                for i in range(stack):
                    row = IDX_DIM + 2 * (si * stack + i) + sub
                    scores = scores + s[:, i * tq:(i + 1) * tq] * iw_t[row:row + 1, :]
        visible = kc * ck + k_off <= q_pos
        key_ref[chunk(kc), :] = jnp.where(visible, _sortable_key(scores), jnp.int32(INT_MIN))
        return c
    lax.fori_loop(0, n_chunks, score_body, 0)

    i16 = jnp.int16
    sub = BF16_ROWS

    def count16_ge(half_ref):
        def count(t):
            t_b = jnp.broadcast_to(t, (sub, tq)).astype(i16)

            def body(kc, acc):
                x = half_ref[chunk(kc), :]
                for r0 in range(0, ck, sub):
                    acc = acc + jnp.where(x[r0:r0 + sub] >= t_b, i16(1), i16(0))
                return acc
            acc = lax.fori_loop(0, n_chunks, body, jnp.zeros((sub, tq), i16))
            return jnp.sum(acc.astype(f32), axis=0, keepdims=True)
        return count

    def hi_body(kc, c):
        hi_ref[chunk(kc), :] = lax.shift_right_arithmetic(key_ref[chunk(kc), :], 16).astype(i16)
        return c
    lax.fori_loop(0, n_chunks, hi_body, 0)
    count_hi = count16_ge(hi_ref)
    hi_thr = _kth_largest_16(count_hi, float(n_keep), (1, tq))
    above = jnp.where(hi_thr >= I16_MAX, 0.0, count_hi(jnp.minimum(hi_thr + 1, I16_MAX)))

    def lo_body(kc, c):
        keys = key_ref[chunk(kc), :]
        lo = ((keys & (HALF_RANGE - 1)) + I16_MIN).astype(i16)
        same_hi = lax.shift_right_arithmetic(keys, 16) == hi_thr
        lo_ref[chunk(kc), :] = jnp.where(same_hi, lo, i16(I16_MIN))
        return c
    lax.fori_loop(0, n_chunks, lo_body, 0)
    lo_thr = _kth_largest_16(count16_ge(lo_ref), n_keep - above, (1, tq))
    thr = hi_thr * HALF_RANGE + (lo_thr - I16_MIN)
    thr = jnp.maximum(thr, jnp.int32(INT_MIN + 1))

    def count32(pred):
        def body(kc, acc):
            ind = jnp.where(pred(key_ref[chunk(kc), :], kc * ck + k_off), 1.0, 0.0)
            return acc + jnp.sum(ind.reshape(ck // SUBLANES, SUBLANES, tq), axis=0)
        acc = lax.fori_loop(0, n_chunks, body, jnp.zeros((SUBLANES, tq), f32))
        return jnp.sum(acc, axis=0, keepdims=True)

    n_ge = count32(lambda keys, pos: keys >= thr)

    @pl.when(jnp.max(n_ge) > n_keep)
    def _():
        n_gt = count32(lambda keys, pos: keys > thr)
        kth = (n_ge - n_gt) - (n_keep - n_gt) + 1.0
        cut = _kth_largest_pos(lambda c: count32(lambda keys, pos: (keys == thr) & (pos >= c)),
                               kth, (1, tq), seq.bit_length())

        def demote(kc, c):
            keys = key_ref[chunk(kc), :]
            surplus = (keys == thr) & (kc * ck + k_off > cut)
            key_ref[chunk(kc), :] = jnp.where(surplus, keys - 1, keys)
            return c
        lax.fori_loop(0, n_chunks, demote, 0)

    m_ref[...] = jnp.full_like(m_ref, NEG_BIG)
    acc_ref[...] = jnp.zeros_like(acc_ref)
    hs = GROUP
    q_sets = [jnp.concatenate([aq_ref[:, hd * HEAD_DIM:(hd + 1) * HEAD_DIM] for hd in range(h0, h0 + hs)], axis=0)
              for h0 in range(0, ATT_HEADS, hs)]

    def apply_pv(kc, si):
        g = si * hs // GROUP
        cols = slice(si * hs * tq, (si + 1) * hs * tq)
        r0 = g * (HEAD_DIM + VT_PAD)
        vt = vt_ref[r0:r0 + HEAD_DIM + VT_PAD, chunk(kc)]
        acc_ref[:, cols] = alpha_ref[:, cols] * acc_ref[:, cols] + _dot(vt, p_ref[:, cols])

    def softmax_chunk(kc, with_pv):
        bias = jnp.where(key_ref[chunk(kc), :] >= thr, 0.0, -jnp.inf)
        bias = jnp.concatenate([bias] * hs, axis=1)
        for si, qs in enumerate(q_sets):
            g = si * hs // GROUP
            cols = slice(si * hs * tq, (si + 1) * hs * tq)
            if with_pv:
                apply_pv(kc - 1, si)
            s = _dot_nt(k_ref[chunk(kc), g * HEAD_DIM:(g + 1) * HEAD_DIM], qs) + bias
            m_old = m_ref[:, cols]
            m_new = jnp.maximum(m_old, jnp.max(s, axis=0, keepdims=True))
            p_ref[:, cols] = jnp.exp2(s - m_new).astype(bf16)
            alpha_ref[:, cols] = jnp.exp2(m_old - m_new)
            m_ref[:, cols] = m_new

    softmax_chunk(0, False)

    def attn_body(kc, c):
        softmax_chunk(kc, True)
        return c
    lax.fori_loop(1, n_chunks, attn_body, 0)
    for si in range(len(q_sets)):
        apply_pv(n_chunks - 1, si)

    for hd in range(ATT_HEADS):
        cols = slice(hd * tq, (hd + 1) * tq)
        o_t = acc_ref[:HEAD_DIM, cols] / acc_ref[HEAD_DIM:HEAD_DIM + 1, cols]
        o_ref[:, hd * HEAD_DIM:(hd + 1) * HEAD_DIM] = jnp.transpose(o_t).astype(bf16)


def _prompt_attention(p, ikw, batch, seq, base, tq, n_keep):
    nq = seq // tq
    ck = tq
    kern = functools.partial(_attn_kernel, tq=tq, ck=ck, seq=seq, n_keep=n_keep)
    aq_blk = (base + T_AQ) * TN // ATT_Q_W
    iq_blk = (base + T_IQ) * TN // IDX_Q_W
    return pl.pallas_call(
        kern,
        grid=(batch, nq),
        in_specs=[
            pl.BlockSpec((tq, ATT_Q_W), lambda b, t: (b * nq + t, aq_blk)),
            pl.BlockSpec((seq, ATT_KV_W), lambda b, t: (b, base + T_AK)),
            pl.BlockSpec((seq, ATT_KV_W), lambda b, t: (b, base + T_AV)),
            pl.BlockSpec((tq, IDX_Q_W), lambda b, t: (b * nq + t, iq_blk)),
            pl.BlockSpec((seq, LANES), lambda b, t: (b, 0)),
            pl.BlockSpec((tq, LANES), lambda b, t: (b * nq + t, 0)),
        ],
        out_specs=pl.BlockSpec((tq, ATT_Q_W), lambda b, t: (b * nq + t, 0)),
        out_shape=jax.ShapeDtypeStruct((batch * seq, ATT_Q_W), bf16),
        scratch_shapes=[pltpu.VMEM((seq, tq), jnp.int32), pltpu.VMEM((seq, tq), jnp.int16),
                        pltpu.VMEM((seq, tq), jnp.int16), pltpu.VMEM((seq, LANES), bf16),
                        pltpu.VMEM((seq, LANES), bf16),
                        pltpu.VMEM((ATT_KV_HEADS * (HEAD_DIM + VT_PAD), seq), bf16),
                        pltpu.VMEM((1, ATT_HEADS * tq), f32),
                        pltpu.VMEM((HEAD_DIM + VT_PAD, ATT_HEADS * tq), f32),
                        pltpu.VMEM((ck, ATT_HEADS * tq), bf16), pltpu.VMEM((1, ATT_HEADS * tq), f32)],
        compiler_params=_params(("arbitrary", "arbitrary")),
    )(p, p, p, p, ikw, ikw)


def _oproj_kernel(og_ref, att_ref, gr_ref, ga_ref, x_ref, wr_ref, wa_ref, wo_ref, o_ref):
    ret_out = _dot(og_ref[...], wr_ref[...])
    att_out = _dot(att_ref[...], wa_ref[...])
    merged = gr_ref[...].astype(f32) * ret_out + ga_ref[...].astype(f32) * att_out
    o_ref[...] = x_ref[...] + _dot(merged.astype(bf16), wo_ref[...])


def _oproj(og, att, p, x, wr, wa, wo, tm):
    m, d = x.shape
    const = lambda shape: pl.BlockSpec(shape, lambda i: (0, 0), pipeline_mode=pl.Buffered(1))
    return pl.pallas_call(
        _oproj_kernel,
        grid=(m // tm,),
        in_specs=[
            pl.BlockSpec((tm, RET_V_W), lambda i: (i, 0)),
            pl.BlockSpec((tm, ATT_Q_W), lambda i: (i, 0)),
            pl.BlockSpec((tm, d), lambda i: (i, 0)),
            pl.BlockSpec((tm, d), lambda i: (i, 1)),
            pl.BlockSpec((tm, d), lambda i: (i, 0)),
            const(wr.shape), const(wa.shape), const(wo.shape),
        ],
        out_specs=pl.BlockSpec((tm, d), lambda i: (i, 0)),
        out_shape=jax.ShapeDtypeStruct((m, d), f32),
        compiler_params=_params(("arbitrary",)),
    )(og, att, p, p, x, wr, wa, wo)


def _dec_ret_kernel(lg_ref, q_ref, k_ref, v_ref, sg_ref, gn_ref, st_ref, og_ref, nst_ref):
    for h in range(RET_HEADS):
        lg = lg_ref[h]
        gamma = jnp.exp(lg)
        q = q_ref[0, :, h * RET_QK_DIM:(h + 1) * RET_QK_DIM]
        k = k_ref[0, :, h * RET_QK_DIM:(h + 1) * RET_QK_DIM]
        v = v_ref[0, :, h * RET_V_DIM:(h + 1) * RET_V_DIM]
        sg = sg_ref[0, :, h * RET_V_DIM:(h + 1) * RET_V_DIM]
        state = st_ref[0, 0, h]
        qk = jnp.sum(q * k, axis=1, keepdims=True)
        q8 = jnp.broadcast_to((q * gamma).astype(bf16), (8, RET_QK_DIM))
        cross = _dot(q8, state.astype(bf16))[0:1]
        k_col = jnp.transpose(jnp.broadcast_to(k, (8, RET_QK_DIM)))[:, 0:1]
        nst_ref[0, h] = gamma * state + k_col * v
        o = _rms(qk * v + cross, gn_ref[...])
        og_ref[0, :, h * RET_V_DIM:(h + 1) * RET_V_DIM] = o * sg


def _decode_retention(q, k, v, sg, gn, state, log_g):
    db = q.shape[0]
    row = lambda w: pl.BlockSpec((1, 1, w), lambda b, lg: (b, 0, 0))
    grid_spec = pltpu.PrefetchScalarGridSpec(
        num_scalar_prefetch=1,
        grid=(db,),
        in_specs=[row(RET_QK_W), row(RET_QK_W), row(RET_V_W), row(RET_V_W),
                  pl.BlockSpec((1, RET_V_DIM), lambda b, lg: (0, 0)),
                  pl.BlockSpec((1, 1, RET_HEADS, RET_QK_DIM, RET_V_DIM), lambda b, lg: (0, b, 0, 0, 0))],
        out_specs=[row(RET_V_W),
                   pl.BlockSpec((1, RET_HEADS, RET_QK_DIM, RET_V_DIM), lambda b, lg: (b, 0, 0, 0))],
    )
    return pl.pallas_call(
        _dec_ret_kernel,
        grid_spec=grid_spec,
        out_shape=[jax.ShapeDtypeStruct((db, 1, RET_V_W), f32),
                   jax.ShapeDtypeStruct((db, RET_HEADS, RET_QK_DIM, RET_V_DIM), f32)],
        compiler_params=_params(("arbitrary",)),
    )(log_g, q, k, v, sg, gn, state)


def _dec_score_kernel(pt_ref, qi_ref, w_ref, pool_ref, s_ref, buf_ref, sem_ref, *, n_pages):
    b = pl.program_id(0)
    slot = b % 2

    def page_copy(seq, p, slot_):
        dst = buf_ref.at[slot_, :, pl.ds(pl.multiple_of(p * PAGE_SIZE, PAGE_SIZE), PAGE_SIZE)]
        return pltpu.make_async_copy(pool_ref.at[pt_ref[seq, p]], dst, sem_ref.at[slot_])

    def start_all(seq, slot_):
        def body(p, c):
            page_copy(seq, p, slot_).start()
            return c
        lax.fori_loop(0, n_pages, body, 0)

    @pl.when(b == 0)
    def _():
        start_all(0, 0)

    @pl.when(b + 1 < pl.num_programs(0))
    def _():
        start_all(b + 1, 1 - slot)

    def wait_body(p, c):
        page_copy(b, p, slot).wait()
        return c
    lax.fori_loop(0, n_pages, wait_body, 0)

    s = jnp.maximum(_dot(qi_ref[0], buf_ref[slot].astype(bf16)), 0.0)
    row = jnp.sum(s * w_ref[0], axis=0, keepdims=True)
    for p in range(n_pages):
        s_ref[0, p:p + 1, :] = row[:, p * PAGE_SIZE:(p + 1) * PAGE_SIZE]


def _decode_scores(page_table, qi, w, pool_ikt):
    db, n_pages = page_table.shape
    kern = functools.partial(_dec_score_kernel, n_pages=n_pages)
    grid_spec = pltpu.PrefetchScalarGridSpec(
        num_scalar_prefetch=1,
        grid=(db,),
        in_specs=[
            pl.BlockSpec((1, IDX_HEADS, IDX_DIM), lambda b, pt: (b, 0, 0)),
            pl.BlockSpec((1, IDX_HEADS, 1), lambda b, pt: (b, 0, 0)),
            pl.BlockSpec(memory_space=pl.ANY),
        ],
        out_specs=pl.BlockSpec((1, n_pages, PAGE_SIZE), lambda b, pt: (b, 0, 0)),
        scratch_shapes=[pltpu.VMEM((2, IDX_DIM, n_pages * PAGE_SIZE), f32), pltpu.SemaphoreType.DMA((2,))],
    )
    return pl.pallas_call(
        kern,
        grid_spec=grid_spec,
        out_shape=jax.ShapeDtypeStruct((db, n_pages, PAGE_SIZE), f32),
        compiler_params=_params(("arbitrary",)),
    )(page_table, qi, w, pool_ikt)


def _dec_select_kernel(s_ref, qi_ref, w_ref, ikn_ref, idx_ref, cnt_ref, nsel_ref, rank_ref, sel_ref,
                       *, n_keep, kmax):
    db, n_pages, _ = s_ref.shape
    ikn = ikn_ref[...].astype(bf16).astype(f32)
    dots = jnp.sum(qi_ref[...].astype(f32) * ikn, axis=2, keepdims=True)
    s_new = jnp.sum(jnp.maximum(dots, 0.0) * w_ref[...], axis=1, keepdims=True)
    keys = _sortable_key(s_ref[...])
    key_new = _sortable_key(s_new)

    def count_ge(t):
        c = jnp.sum(jnp.where(keys >= t, 1.0, 0.0), axis=1, keepdims=True)
        return jnp.sum(c, axis=2, keepdims=True) + jnp.where(key_new >= t, 1.0, 0.0)

    thr = _kth_largest_key(count_ge, n_keep, (db, 1, 1))
    sel_ref[...] = jnp.where(keys >= thr, 1.0, 0.0)
    nsel_ref[...] = jnp.broadcast_to(jnp.where(key_new >= thr, 1.0, 0.0), (db, 1, LANES))

    def count3(mask):
        c = jnp.sum(jnp.where(mask, 1.0, 0.0), axis=1, keepdims=True)
        return jnp.sum(c, axis=2, keepdims=True)

    n_ge = count_ge(thr)

    @pl.when(jnp.max(n_ge) > n_keep)
    def _():
        pos = (lax.broadcasted_iota(jnp.int32, keys.shape, 1) * PAGE_SIZE
               + lax.broadcasted_iota(jnp.int32, keys.shape, 2))
        quota = n_keep - (count3(keys > thr) + jnp.where(key_new > thr, 1.0, 0.0))
        eq = keys == thr
        ties_past = count3(eq)
        kth = ties_past - jnp.minimum(quota, ties_past) + 1.0
        cut = _kth_largest_pos(lambda c: count3(eq & (pos >= c)), kth, (db, 1, 1),
                               (n_pages * PAGE_SIZE).bit_length())
        sel_ref[...] = jnp.where((keys > thr) | (eq & (pos <= cut)), 1.0, 0.0)
        new_sel = (key_new > thr) | ((key_new == thr) & (quota > ties_past))
        nsel_ref[...] = jnp.broadcast_to(jnp.where(new_sel, 1.0, 0.0), (db, 1, LANES))

    pp = lax.broadcasted_iota(jnp.int32, (n_pages, n_pages), 0)
    pq = lax.broadcasted_iota(jnp.int32, (n_pages, n_pages), 1)
    earlier_page = jnp.where(pq < pp, 1.0, 0.0)
    oo = lax.broadcasted_iota(jnp.int32, (PAGE_SIZE, PAGE_SIZE), 0)
    oq = lax.broadcasted_iota(jnp.int32, (PAGE_SIZE, PAGE_SIZE), 1)
    upto_off = jnp.where(oo <= oq, 1.0, 0.0)
    r_col = lax.broadcasted_iota(jnp.int32, (kmax, 1), 0).astype(f32)
    page_row = lax.broadcasted_iota(jnp.int32, (1, n_pages), 1).astype(f32)
    for b in range(db):
        sb = sel_ref[b]
        per_page = jnp.broadcast_to(jnp.sum(sb, axis=1, keepdims=True), (n_pages, PAGE_SIZE))
        before = _dot(earlier_page, per_page)
        before_row = jnp.transpose(before)[0:1, :]
        count_row = jnp.transpose(per_page)[0:1, :]
        in_page = jnp.where((before_row <= r_col) & (r_col < before_row + count_row), 1.0, 0.0)
        page_of = jnp.sum(in_page * page_row, axis=1, keepdims=True)
        within = r_col - jnp.sum(in_page * before_row, axis=1, keepdims=True)
        sel_row = _dot(in_page, sb)
        prefix = _dot(sel_row, upto_off)
        off = jnp.sum(jnp.where(prefix <= within, 1.0, 0.0), axis=1, keepdims=True)
        total = jnp.sum(count_row, axis=1, keepdims=True)
        pos = jnp.where(r_col < total, page_of * PAGE_SIZE + off, 0.0)
        idx_ref[b] = pos.astype(jnp.int32)
        cnt_ref[b] = jnp.broadcast_to(total, (1, LANES)).astype(jnp.int32)


def _decode_select(scores, qi, w, ik_new, n_keep):
    db, n_pages, _ = scores.shape
    kmax = -(-n_keep // 8) * 8
    kern = functools.partial(_dec_select_kernel, n_keep=n_keep, kmax=kmax)
    full = lambda shape: pl.BlockSpec(shape, lambda i: (0,) * len(shape))
    return pl.pallas_call(
        kern,
        grid=(1,),
        in_specs=[full(scores.shape), full(qi.shape), full(w.shape), full(ik_new.shape)],
        out_specs=[full((db, kmax, 1)), full((db, 1, LANES)), full((db, 1, LANES))],
        out_shape=[jax.ShapeDtypeStruct((db, kmax, 1), jnp.int32),
                   jax.ShapeDtypeStruct((db, 1, LANES), jnp.int32),
                   jax.ShapeDtypeStruct((db, 1, LANES), f32)],
        scratch_shapes=[pltpu.VMEM((db, n_pages, PAGE_SIZE), f32), pltpu.VMEM((db, n_pages, PAGE_SIZE), f32)],
        compiler_params=_params(("arbitrary",)),
    )(scores, qi, w, ik_new)


def _dec_attn_kernel(idx_ref, pt_ref, cnt_ref, q_ref, kn_ref, vn_ref, nsel_ref, kpool_ref, vpool_ref, o_ref,
                     kbuf_ref, vbuf_ref, sem_ref, *, kmax):
    b = pl.program_id(0)
    slot = b % 2
    page_shift = PAGE_SIZE.bit_length() - 1

    def row_copies(seq, r, slot_):
        pos = idx_ref[seq, r]
        phys = pt_ref[seq, lax.shift_right_logical(pos, page_shift)]
        src0 = pl.multiple_of((pos & (PAGE_SIZE - 1)) * ATT_KV_HEADS, ATT_KV_HEADS)
        dst0 = pl.multiple_of(r * ATT_KV_HEADS, ATT_KV_HEADS)
        return (pltpu.make_async_copy(kpool_ref.at[phys, pl.ds(src0, ATT_KV_HEADS), :],
                                      kbuf_ref.at[slot_, pl.ds(dst0, ATT_KV_HEADS), :], sem_ref.at[0, slot_]),
                pltpu.make_async_copy(vpool_ref.at[phys, pl.ds(src0, ATT_KV_HEADS), :],
                                      vbuf_ref.at[slot_, pl.ds(dst0, ATT_KV_HEADS), :], sem_ref.at[1, slot_]))

    def start_all(seq, slot_):
        def body(r, c):
            ck, cv = row_copies(seq, r, slot_)
            ck.start()
            cv.start()
            return c
        lax.fori_loop(0, kmax, body, 0, unroll=8)

    @pl.when(b == 0)
    def _():
        start_all(0, 0)

    @pl.when(b + 1 < pl.num_programs(0))
    def _():
        start_all(b + 1, 1 - slot)

    def wait_body(r, c):
        ck, cv = row_copies(b, r, slot)
        ck.wait()
        cv.wait()
        return c
    lax.fori_loop(0, kmax, wait_body, 0, unroll=8)

    q = q_ref[0]
    s = _dot_nt(q, kbuf_ref[slot].astype(bf16))
    col = lax.broadcasted_iota(jnp.int32, s.shape, 1)
    head = lax.broadcasted_iota(jnp.int32, s.shape, 0)
    ok = (col % ATT_KV_HEADS == head // GROUP) & (col // ATT_KV_HEADS < cnt_ref[b])
    s = jnp.where(ok, s, -jnp.inf)
    s_n = jnp.sum(q.astype(f32) * kn_ref[0].astype(f32), axis=1, keepdims=True)
    s_n = jnp.where(nsel_ref[0][:, 0:1] > 0.0, s_n, -jnp.inf)
    m = jnp.maximum(jnp.max(s, axis=1, keepdims=True), s_n)
    p = jnp.exp2(s - m)
    p_n = jnp.exp2(s_n - m)
    l = jnp.sum(p, axis=1, keepdims=True) + p_n
    pv = _dot(p.astype(bf16), vbuf_ref[slot].astype(bf16))
    o_ref[0] = (pv + p_n.astype(bf16).astype(f32) * vn_ref[0].astype(f32)) / l


def _decode_attention(idx, page_table, cnt, q, k_new, v_new, nsel, pool_k, pool_v):
    db, kmax = idx.shape
    kern = functools.partial(_dec_attn_kernel, kmax=kmax)
    seqb = lambda shape: pl.BlockSpec(shape, lambda b, *_: (b, 0, 0))
    grid_spec = pltpu.PrefetchScalarGridSpec(
        num_scalar_prefetch=3,
        grid=(db,),
        in_specs=[
            seqb((1, ATT_HEADS, HEAD_DIM)),
            seqb((1, ATT_HEADS, HEAD_DIM)),
            seqb((1, ATT_HEADS, HEAD_DIM)),
            seqb((1, 1, LANES)),
            pl.BlockSpec(memory_space=pl.ANY),
            pl.BlockSpec(memory_space=pl.ANY),
        ],
        out_specs=seqb((1, ATT_HEADS, HEAD_DIM)),
        scratch_shapes=[pltpu.VMEM((2, kmax * ATT_KV_HEADS, HEAD_DIM), f32),
                        pltpu.VMEM((2, kmax * ATT_KV_HEADS, HEAD_DIM), f32),
                        pltpu.SemaphoreType.DMA((2, 2))],
    )
    return pl.pallas_call(
        kern,
        grid_spec=grid_spec,
        out_shape=jax.ShapeDtypeStruct((db, ATT_HEADS, HEAD_DIM), f32),
        compiler_params=_params(("arbitrary",)),
    )(idx, page_table, cnt, q, k_new, v_new, nsel, pool_k, pool_v)


def _rope_tables(pos):
    half = RET_QK_DIM // 2
    inv = ROPE_BASE ** (-jnp.arange(half, dtype=f32) / half)
    ang = pos.astype(f32)[:, None] * inv[None, :]
    cos, sin = jnp.cos(ang), jnp.sin(ang)
    return jnp.concatenate([cos, cos], axis=1), jnp.concatenate([-sin, sin], axis=1)


def _pick(n, pref):
    for t in pref:
        if n % t == 0:
            return t
    return n


def kernel(x_prompt, x_sample, state_ret, cache_k, cache_v, cache_idx_k, page_table, ffn1_norm, ffn1_w1, ffn1_w2, mix_norm, w_in, q_norm, k_norm, ret_norm, w_ret_out, w_att_out, w_o, ffn2_norm, ffn2_w1, ffn2_w2):
    batch, seq, d = x_prompt.shape
    db, ts, _ = x_sample.shape
    depth = w_in.shape[0]
    assert depth == 1 and ts == 1 and d % (2 * TN) == 0
    n_pages = page_table.shape[1]
    past = n_pages * PAGE_SIZE
    n_phys = cache_k.shape[1]
    base = 2 * (d // TN)
    m = batch * seq
    ms = BF16_ROWS

    log_g = jnp.log1p(-jnp.exp2(-5.0 - jnp.arange(RET_HEADS, dtype=f32)))
    w1a, w2a = ffn1_w1[0].astype(bf16), ffn1_w2[0].astype(bf16)
    w1b, w2b = ffn2_w1[0].astype(bf16), ffn2_w2[0].astype(bf16)
    w_pad = jnp.swapaxes(w_in[0], 0, 1).astype(bf16)
    wr, wa, wo = w_ret_out[0].astype(bf16), w_att_out[0].astype(bf16), w_o[0].astype(bf16)
    g1, g2, g3 = ffn1_norm, mix_norm, ffn2_norm
    qn, kn, gn = q_norm, k_norm, ret_norm

    tm = _pick(m, (512, 256, 128))
    tf = _pick(ffn1_w2.shape[1], (512, 256, 128))
    tmp = _pick(seq, (1024, 512, 256, 128))
    chunk = _pick(seq, (256, 128))
    tq = _pick(seq, (256, 128))
    tmo = _pick(m, (256, 128))

    xp = x_prompt.reshape(m, d)
    x1 = _ffn(xp, g1, w1a, w2a, tm, tf)
    cos_p, sin_p = _rope_tables(jnp.arange(seq))
    p, k32, v32, ikw = _proj(x1, g2, w_pad, cos_p, sin_p, qn, kn, tmp)
    og, st_p = _retention(p, log_g, gn, batch, seq, base, chunk)
    att = _prompt_attention(p, ikw, batch, seq, base, tq, min(TOPK_MAX, seq // 4))
    x2 = _oproj(og, att, p, x1, wr, wa, wo, tmo)
    yp = _ffn(x2, g3, w1b, w2b, tm, tf).reshape(batch, seq, d)

    xs = jnp.zeros((ms, d), f32).at[:db].set(x_sample.reshape(db, d))
    s1 = _ffn(xs, g1, w1a, w2a, ms, tf)
    cos_s, sin_s = _rope_tables(jnp.full((ms,), past, jnp.int32))
    ps, k32s, v32s, ikws = _proj(s1, g2, w_pad, cos_s, sin_s, qn, kn, ms)
    seg = lambda t0, t1: ps[:db, (base + t0) * TN:(base + t1) * TN]
    row3 = lambda a: a.astype(f32).reshape(db, 1, a.shape[-1])
    og_s, st_s = _decode_retention(row3(seg(T_RQ, T_RK)), row3(seg(T_RK, T_RV)), row3(seg(T_RV, T_RG)),
                                   row3(seg(T_RG, T_AQ)), gn, state_ret, log_g)
    qi = seg(T_IQ, T_IKW).reshape(db, IDX_HEADS, IDX_DIM)
    wi = (ikws[:db, IDX_DIM:IDX_DIM + IDX_HEADS] * IDX_W_SCALE).reshape(db, IDX_HEADS, 1)
    ik_new = ikws[:db, :IDX_DIM].reshape(db, 1, IDX_DIM)
    pool_ikt = jnp.swapaxes(cache_idx_k.reshape(n_phys, PAGE_SIZE, IDX_DIM), 1, 2)
    scores = _decode_scores(page_table, qi, wi, pool_ikt)
    idx, cnt, nsel = _decode_select(scores, qi, wi, ik_new, min(TOPK_MAX, (past + ts) // 4))
    aq_s = seg(T_AQ, T_AK).reshape(db, ATT_HEADS, HEAD_DIM)
    new_kv = lambda a: a[:db * ATT_KV_HEADS].reshape(db, ATT_KV_HEADS, HEAD_DIM)
    expand = lambda a: jnp.repeat(new_kv(a), GROUP, axis=1).astype(bf16)
    pool_rows = lambda c: c.reshape(n_phys, PAGE_SIZE * ATT_KV_HEADS, HEAD_DIM)
    att_s = _decode_attention(idx.reshape(db, -1), page_table, cnt[:, 0, 0], aq_s, expand(k32s), expand(v32s),
                              nsel, pool_rows(cache_k), pool_rows(cache_v))
    pad_rows = lambda a: jnp.pad(a.reshape(db, -1).astype(bf16), ((0, ms - db), (0, 0)))
    s2 = _oproj(pad_rows(og_s), pad_rows(att_s), ps, s1, wr, wa, wo, ms)
    ys = _ffn(s2, g3, w1b, w2b, ms, tf)[:db].reshape(db, ts, d)

    return (yp, ys,
            st_p[None],
            k32.reshape(1, batch, seq, ATT_KV_HEADS, HEAD_DIM),
            v32.reshape(1, batch, seq, ATT_KV_HEADS, HEAD_DIM),
            ikw[:, :IDX_DIM].reshape(1, batch, seq, IDX_DIM),
            st_s[None],
            new_kv(k32s).reshape(1, db, ts, ATT_KV_HEADS, HEAD_DIM),
            new_kv(v32s).reshape(1, db, ts, ATT_KV_HEADS, HEAD_DIM),
            ikws[:db, :IDX_DIM].reshape(1, db, ts, IDX_DIM))
```
